```python
import math
import jax
import jax.numpy as jnp
from jax import lax
import numpy as np

D_MODEL = 2048
BATCH = 16
SEQ = 256
DEPTH = 2
DEC_BATCH = 4
DEC_SEQ = 1024
PAST_LEN = 512

GRID_W = 64
SSD_D = D_MODEL // 2
SSD_P = 64
SSD_H = SSD_D // SSD_P
SSD_G = 2
SSD_N = 128
SSD_CHUNK = 128
GLA_H = 4
GLA_VW = D_MODEL // 4
GLA_DV = GLA_VW // GLA_H
GLA_DK = GLA_DV // 2
GLA_KW = GLA_H * GLA_DK
GLA_RANK = 16
GLA_TAU = 16.0
GLA_CHUNK = 64
LRU_W = D_MODEL // 4
LRU_NB = 8
LRU_BW = LRU_W // LRU_NB
LRU_C = 8.0
CONV_K = 4
CONV_PAD_L = 1
CONV_PAD_R = CONV_K - 1 - CONV_PAD_L
D_MIX = SSD_D + GLA_VW + LRU_W
SSD_CONV_CH = SSD_D + 2 * SSD_G * SSD_N
IN_SPLITS = (SSD_D, SSD_CONV_CH, SSD_H, GLA_KW, GLA_KW, GLA_VW, GLA_VW, GLA_RANK, LRU_W, LRU_W)
IN_COLS = sum(IN_SPLITS)
FF_DENSE = ((8 * D_MODEL // 3 + 127) // 128) * 128
N_EXPERTS = 8
TOP_K = 2
FF_EXPERT = 7 * D_MODEL // 2
EPS = 1e-6
F32 = jnp.float32

kernel_name = 'hybrid_ssd_gla_rglru_diffusion_step'


def _split_cols(x, sizes):
    out, start = [], 0
    for s in sizes:
        out.append(x[..., start:start + s])
        start += s
    return out


def rmsnorm(x, g):
    xf = x.astype(F32)
    y = xf * lax.rsqrt(jnp.mean(xf * xf, axis=-1, keepdims=True) + EPS)
    return (y * g.astype(F32)).astype(x.dtype)


def adaln(cvec, w, b):
    m = jax.nn.silu(cvec.astype(F32)) @ w.astype(F32) + b.astype(F32)
    return jnp.split(m[..., None, :], 6, axis=-1)


def centred_conv(x, w, b, grid_rows):
    bsz, l, ch = x.shape
    xs = x if grid_rows is None else x.reshape(bsz, grid_rows, GRID_W, ch)
    ax = xs.ndim - 2
    n = xs.shape[ax]
    pad = [(0, 0)] * xs.ndim
    pad[ax] = (CONV_PAD_L, CONV_PAD_R)
    xp = jnp.pad(xs, pad)
    w = w.astype(F32)
    y = b.astype(F32) + sum(w[k] * lax.slice_in_dim(xp, k, k + n, axis=ax) for k in range(CONV_K))
    return y.reshape(bsz, l, ch)


def segsum(a):
    cs = jnp.cumsum(a, axis=-1)
    t = a.shape[-1]
    diff = cs[..., :, None] - cs[..., None, :]
    return jnp.where(jnp.tril(jnp.ones((t, t), dtype=bool)), diff, -jnp.inf)


def ssd_chunked(x, a, bm, cm, h0):
    bsz, l, nh, hp = x.shape
    ng, ns = bm.shape[2], bm.shape[3]
    ne = nh // ng
    nc = l // SSD_CHUNK
    x = x.reshape(bsz, nc, SSD_CHUNK, ng, ne, hp)
    bm = bm.reshape(bsz, nc, SSD_CHUNK, ng, ns)
    cm = cm.reshape(bsz, nc, SSD_CHUNK, ng, ns)
    a = jnp.moveaxis(a.reshape(bsz, nc, SSD_CHUNK, ng, ne), 2, -1)
    a_cum = jnp.cumsum(a, axis=-1)
    w = jnp.einsum('bctgn,bcsgn->bcgts', cm, bm)[:, :, :, None] * jnp.exp(segsum(a))
    y = jnp.einsum('bcgets,bcsgep->bctgep', w, x)
    decay_s = jnp.exp(a_cum[..., -1:] - a_cum)
    states = jnp.einsum('bcsgn,bcges,bcsgep->bcgepn', bm, decay_s, x)
    states = jnp.concatenate([h0.reshape(bsz, 1, ng, ne, hp, ns), states], axis=1)
    a_chunk = jnp.pad(jnp.moveaxis(a_cum[..., -1], 1, -1), [(0, 0)] * 3 + [(1, 0)])
    new_states = jnp.einsum('bgezc,bcgepn->bzgepn', jnp.exp(segsum(a_chunk)), states)
    y = y + jnp.einsum('bctgn,bcgepn,bcget->bctgep', cm, new_states[:, :-1], jnp.exp(a_cum))
    return y.reshape(bsz, l, nh, hp), new_states[:, -1].reshape(bsz, nh, hp, ns)


def ssd_mixer(z, xbc, dt_raw, p, h0, grid_rows):
    bsz, l, _ = z.shape
    xbc = jax.nn.silu(centred_conv(xbc, p['ssd_conv_w'], p['ssd_conv_b'], grid_rows))
    xs, bm, cm = _split_cols(xbc, (SSD_D, SSD_G * SSD_N, SSD_G * SSD_N))
    xs = xs.reshape(bsz, l, SSD_H, SSD_P)
    bm = bm.reshape(bsz, l, SSD_G, SSD_N)
    cm = cm.reshape(bsz, l, SSD_G, SSD_N)
    y = p['ssd_D'].astype(F32)[:, None] * xs
    finals = []
    for d in range(2):
        dt = jax.nn.softplus(dt_raw + p['ssd_dt_bias'][d].astype(F32))
        a = -jnp.exp(p['ssd_A_log'][d].astype(F32)) * dt
        args = (xs * dt[..., None], a, bm, cm)
        if d == 1:
            args = tuple(jnp.flip(t, axis=1) for t in args)
        yd, fd = ssd_chunked(*args, h0[:, d])
        y = y + (jnp.flip(yd, axis=1) if d == 1 else yd)
        finals.append(fd)
    y = y.reshape(bsz, l, SSD_D) * jax.nn.silu(z)
    return rmsnorm(y, p['ssd_norm_g']), jnp.stack(finals, axis=1)


def gla_chunked(q, k, v, log_a, s0):
    bsz, l, nh, dk = q.shape
    dv = v.shape[-1]
    nc = l // GLA_CHUNK
    q, k, log_a = (t.reshape(bsz, nc, GLA_CHUNK, nh, dk) for t in (q, k, log_a))
    v = v.reshape(bsz, nc, GLA_CHUNK, nh, dv)
    bc = jnp.cumsum(log_a, axis=2)
    mask = jnp.tril(jnp.ones((GLA_CHUNK, GLA_CHUNK), dtype=bool))[:, :, None, None]
    diff = bc[:, :, :, None] - bc[:, :, None, :]
    decay = jnp.exp(jnp.where(mask, diff, -jnp.inf))
    attn = jnp.einsum('bcthk,bcshk,bctshk->bctsh', q, k, decay)
    o = jnp.einsum('bctsh,bcshv->bcthv', attn, v)
    last = bc[:, :, -1]
    ds = jnp.einsum('bcshk,bcshv->bchkv', k * jnp.exp(last[:, :, None] - bc), v)

    def step(s, inp):
        g_c, ds_c = inp
        return g_c[..., None] * s + ds_c, s

    s_fin, s_prev = lax.scan(step, s0, (jnp.moveaxis(jnp.exp(last), 1, 0), jnp.moveaxis(ds, 1, 0)))
    o = o + jnp.einsum('bcthk,cbhkv->bcthv', q * jnp.exp(bc), s_prev)
    return o.reshape(bsz, l, nh, dv), s_fin


def gla_mixer(q, k, v, g, graw, p, s0):
    bsz, l, _ = q.shape
    q = q.reshape(bsz, l, GLA_H, GLA_DK) * (GLA_DK ** -0.5)
    k = k.reshape(bsz, l, GLA_H, GLA_DK)
    v = v.reshape(bsz, l, GLA_H, GLA_DV)
    o = 0.0
    finals = []
    for d in range(2):
        logit = graw @ p['gla_gate_w'][d].astype(F32) + p['gla_gate_b'][d].astype(F32)
        log_a = jax.nn.log_sigmoid(logit).reshape(bsz, l, GLA_H, GLA_DK) / GLA_TAU
        args = (q, k, v, log_a)
        if d == 1:
            args = tuple(jnp.flip(t, axis=1) for t in args)
        od, fd = gla_chunked(*args, s0[:, d])
        o = o + (jnp.flip(od, axis=1) if d == 1 else od)
        finals.append(fd)
    o = rmsnorm(o, p['gla_norm_g']).reshape(bsz, l, GLA_VW) * jax.nn.silu(g)
    return o, jnp.stack(finals, axis=1)


def _lin_combine(left, right):
    a_l, b_l = left
    a_r, b_r = right
    return a_l * a_r, a_r * b_l + b_r


def linear_scan(a, u, h0):
    u = u.at[:, 0].add(a[:, 0] * h0)
    _, h = lax.associative_scan(_lin_combine, (a, u), axis=1)
    return h, h[:, -1]


def rglru_mixer(xb, gb, p, h0, grid_rows):
    bsz, l, _ = xb.shape
    xr = centred_conv(xb, p['lru_conv_w'], p['lru_conv_b'], grid_rows)
    xblk = xr.reshape(bsz, l, LRU_NB, LRU_BW)
    y = 0.0
    finals = []
    for d in range(2):
        r = jax.nn.sigmoid(jnp.einsum('blni,nij->blnj', xblk, p['lru_wa'][d].astype(F32)).reshape(bsz, l, LRU_W)
                           + p['lru_ba'][d].astype(F32))
        i = jax.nn.sigmoid(jnp.einsum('blni,nij->blnj', xblk, p['lru_wx'][d].astype(F32)).reshape(bsz, l, LRU_W)
                           + p['lru_bx'][d].astype(F32))
        log_a = -LRU_C * r * jax.nn.softplus(-p['lru_lambda'][d].astype(F32))
        a = jnp.exp(log_a)
        u = jnp.sqrt(-jnp.expm1(2.0 * log_a)) * (i * xr)
        if d == 1:
            a, u = jnp.flip(a, axis=1), jnp.flip(u, axis=1)
        hd, fd = linear_scan(a, u, h0[:, d])
        y = y + (jnp.flip(hd, axis=1) if d == 1 else hd)
        finals.append(fd)
    return y * jax.nn.gelu(gb), jnp.stack(finals, axis=1)


def mixer_block(h, p, init, grid_rows):
    proj = (h @ p['in_w']).astype(F32)
    z, xbc, dt_raw, q, k, v, g, graw, xb, gb = _split_cols(proj, IN_SPLITS)
    h_ssd, h_gla, h_lru = (s.astype(F32) for s in init)
    y_ssd, s_ssd = ssd_mixer(z, xbc, dt_raw, p, h_ssd, grid_rows)
    y_gla, s_gla = gla_mixer(q, k, v, g, graw, p, h_gla)
    y_lru, s_lru = rglru_mixer(xb, gb, p, h_lru, grid_rows)
    y = jnp.concatenate([y_ssd, y_gla, y_lru], axis=-1).astype(h.dtype) @ p['out_w']
    return y, (s_ssd, s_gla, s_lru)


def swiglu(x, w1, w3, w2):
    return (jax.nn.silu(x @ w1) * (x @ w3)) @ w2


def moe_swiglu(x, router, w1, w3, w2):
    bsz, l, d = x.shape
    xt = x.reshape(bsz * l, d)
    logits = xt.astype(F32) @ router.astype(F32)
    top_v, top_i = lax.top_k(logits, TOP_K)
    gates = jax.nn.softmax(top_v, axis=-1)
    combine = jnp.einsum('tk,tke->te', gates, jax.nn.one_hot(top_i, N_EXPERTS, dtype=F32))
    out = jnp.zeros((bsz * l, d), F32)
    for e in range(N_EXPERTS):
        out = out + combine[:, e:e + 1] * swiglu(xt, w1[e], w3[e], w2[e]).astype(F32)
    return out.astype(x.dtype).reshape(bsz, l, d)


def trunk_layer(x, mods, p, ffn_w, is_moe, init, grid_rows):
    sh1, sc1, g1, sh2, sc2, g2 = mods
    h = (rmsnorm(x, p['norm1_g']) * (1.0 + sc1) + sh1).astype(x.dtype)
    y, states = mixer_block(h, p, init, grid_rows)
    x = x + (g1 * y).astype(x.dtype)
    h = (rmsnorm(x, p['norm2_g']) * (1.0 + sc2) + sh2).astype(x.dtype)
    f = moe_swiglu(h, *ffn_w) if is_moe else swiglu(h, *ffn_w)
    x = x + (g2 * f).astype(x.dtype)
    return x, states


def setup_inputs(seed: int = 0) -> dict:
    key = jax.random.key(seed)
    ks = list(jax.random.split(key, 48))

    def nrm(shape, scale):
        return scale * jax.random.normal(ks.pop(), shape, F32)

    def unif(shape, lo, hi):
        return jax.random.uniform(ks.pop(), shape, F32, lo, hi)

    d = D_MODEL
    n_even = (DEPTH + 1) // 2
    n_odd = DEPTH // 2
    dt0 = jnp.exp(unif((DEPTH, 2, SSD_H), math.log(1e-3), math.log(1e-1)))
    ssd_dt_bias = dt0 + jnp.log(-jnp.expm1(-dt0))
    a0 = unif((DEPTH, 2, LRU_W), 0.9, 0.999)
    s = a0 ** (1.0 / LRU_C)
    lru_lambda = jnp.log(s) - jnp.log1p(-s)
    return {
        'x_prompt': nrm((BATCH, SEQ, d), 1.0),
        'x_sample': nrm((DEC_BATCH, DEC_SEQ, d), 1.0),
        'state_ssd': nrm((DEC_BATCH, DEPTH, 2, SSD_H, SSD_P, SSD_N), 0.2),
        'state_gla': nrm((DEC_BATCH, DEPTH, 2, GLA_H, GLA_DK, GLA_DV), 1.0),
        'state_lru': nrm((DEC_BATCH, DEPTH, 2, LRU_W), 0.5),
        'c': nrm((DEC_BATCH, d), 1.0),
        'c_ctx': nrm((d,), 1.0),
        'mod_w': nrm((DEPTH, d, 6 * d), 0.5 * d ** -0.5),
        'mod_b': nrm((DEPTH, 6 * d), 0.02),
        'norm1_g': 1.0 + nrm((DEPTH, d), 0.02),
        'norm2_g': 1.0 + nrm((DEPTH, d), 0.02),
        'in_w': nrm((DEPTH, d, IN_COLS), d ** -0.5),
        'ssd_conv_w': nrm((DEPTH, CONV_K, SSD_CONV_CH), CONV_K ** -0.5),
        'ssd_conv_b': nrm((DEPTH, SSD_CONV_CH), 0.02),
        'ssd_A_log': jnp.log(unif((DEPTH, 2, SSD_H), 1.0, 16.0)),
        'ssd_dt_bias': ssd_dt_bias,
        'ssd_D': 1.0 + nrm((DEPTH, SSD_H), 0.1),
        'ssd_norm_g': 1.0 + nrm((DEPTH, SSD_D), 0.02),
        'gla_gate_w': nrm((DEPTH, 2, GLA_RANK, GLA_KW), GLA_RANK ** -0.5),
        'gla_gate_b': nrm((DEPTH, 2, GLA_KW), 0.1),
        'gla_norm_g': 1.0 + nrm((DEPTH, GLA_DV), 0.02),
        'lru_conv_w': nrm((DEPTH, CONV_K, LRU_W), CONV_K ** -0.5),
        'lru_conv_b': nrm((DEPTH, LRU_W), 0.02),
        'lru_wa': nrm((DEPTH, 2, LRU_NB, LRU_BW, LRU_BW), LRU_BW ** -0.5),
        'lru_ba': nrm((DEPTH, 2, LRU_W), 0.02),
        'lru_wx': nrm((DEPTH, 2, LRU_NB, LRU_BW, LRU_BW), LRU_BW ** -0.5),
        'lru_bx': nrm((DEPTH, 2, LRU_W), 0.02),
        'lru_lambda': lru_lambda,
        'out_w': nrm((DEPTH, D_MIX, d), D_MIX ** -0.5),
        'ffn_w1': nrm((n_even, d, FF_DENSE), d ** -0.5),
        'ffn_w3': nrm((n_even, d, FF_DENSE), d ** -0.5),
        'ffn_w2': nrm((n_even, FF_DENSE, d), FF_DENSE ** -0.5),
        'moe_router': nrm((n_odd, d, N_EXPERTS), d ** -0.5),
        'moe_w1': nrm((n_odd, N_EXPERTS, d, FF_EXPERT), d ** -0.5),
        'moe_w3': nrm((n_odd, N_EXPERTS, d, FF_EXPERT), d ** -0.5),
        'moe_w2': nrm((n_odd, N_EXPERTS, FF_EXPERT, d), FF_EXPERT ** -0.5),
        'final_norm_g': 1.0 + nrm((d,), 0.02),
    }


def reference(x_prompt, x_sample, state_ssd, state_gla, state_lru, c, c_ctx,
              mod_w, mod_b, norm1_g, norm2_g, in_w,
              ssd_conv_w, ssd_conv_b, ssd_A_log, ssd_dt_bias, ssd_D, ssd_norm_g,
              gla_gate_w, gla_gate_b, gla_norm_g,
              lru_conv_w, lru_conv_b, lru_wa, lru_ba, lru_wx, lru_bx, lru_lambda,
              out_w, ffn_w1, ffn_w3, ffn_w2, moe_router, moe_w1, moe_w3, moe_w2, final_norm_g):
    grid_rows = x_sample.shape[1] // GRID_W
    bp = x_prompt.shape[0]
    zero_init = (jnp.zeros((bp, 2, SSD_H, SSD_P, SSD_N), F32),
                 jnp.zeros((bp, 2, GLA_H, GLA_DK, GLA_DV), F32),
                 jnp.zeros((bp, 2, LRU_W), F32))
    xp, xs = x_prompt, x_sample
    new_ssd, new_gla, new_lru = [], [], []
    for i in range(DEPTH):
        p = {
            'in_w': in_w[i], 'out_w': out_w[i], 'norm1_g': norm1_g[i], 'norm2_g': norm2_g[i],
            'ssd_conv_w': ssd_conv_w[i], 'ssd_conv_b': ssd_conv_b[i], 'ssd_A_log': ssd_A_log[i],
            'ssd_dt_bias': ssd_dt_bias[i], 'ssd_D': ssd_D[i], 'ssd_norm_g': ssd_norm_g[i],
            'gla_gate_w': gla_gate_w[i], 'gla_gate_b': gla_gate_b[i], 'gla_norm_g': gla_norm_g[i],
            'lru_conv_w': lru_conv_w[i], 'lru_conv_b': lru_conv_b[i], 'lru_wa': lru_wa[i],
            'lru_ba': lru_ba[i], 'lru_wx': lru_wx[i], 'lru_bx': lru_bx[i], 'lru_lambda': lru_lambda[i],
        }
        j = i // 2
        is_moe = (i % 2 == 1)
        if is_moe:
            ffn_w = (moe_router[j], moe_w1[j], moe_w3[j], moe_w2[j])
        else:
            ffn_w = (ffn_w1[j], ffn_w3[j], ffn_w2[j])
        xp, (s_ssd, s_gla, s_lru) = trunk_layer(xp, adaln(c_ctx, mod_w[i], mod_b[i]), p, ffn_w, is_moe,
                                                zero_init, None)
        new_ssd.append(s_ssd)
        new_gla.append(s_gla)
        new_lru.append(s_lru)
        xs, _ = trunk_layer(xs, adaln(c, mod_w[i], mod_b[i]), p, ffn_w, is_moe,
                            (state_ssd[:, i], state_gla[:, i], state_lru[:, i]), grid_rows)
    y_prompt = rmsnorm(xp, final_norm_g)
    y_sample = rmsnorm(xs, final_norm_g)
    return (y_prompt, y_sample, jnp.stack(new_ssd, axis=1), jnp.stack(new_gla, axis=1), jnp.stack(new_lru, axis=1))
```

```python
import functools
import math

import jax
import jax.numpy as jnp
from jax import lax
from jax.experimental import pallas as pl
from jax.experimental.pallas import tpu as pltpu

F32 = jnp.float32
BF16 = jnp.bfloat16

D_MODEL = 2048
DEPTH = 2
GRID_W = 64
SSD_D = D_MODEL // 2
SSD_P = 64
SSD_H = SSD_D // SSD_P
SSD_G = 2
SSD_N = 128
GLA_H = 4
GLA_VW = D_MODEL // 4
GLA_DV = GLA_VW // GLA_H
GLA_DK = GLA_DV // 2
GLA_KW = GLA_H * GLA_DK
GLA_RANK = 16
GLA_TAU = 16.0
LRU_W = D_MODEL // 4
LRU_NB = 8
LRU_BW = LRU_W // LRU_NB
LRU_C = 8.0
SSD_CONV_CH = SSD_D + 2 * SSD_G * SSD_N
N_EXPERTS = 8
TOP_K = 2
EPS = 1e-6

LANE = 128
SSD_T = 128
GLA_T = 64
MOD_ROWS = 8
VMEM_LIMIT = 56 * 1024 * 1024

COL_Z = 0
COL_XS = 1024
COL_BC = 2048
COL_Q = 2560
COL_K = 2816
COL_V = 3072
COL_G = 3584
COL_XB = 4096
COL_GB = 4608
COL_SMALL = 5120
IN_PAD = 5632


def _cparams(n_axes):
    return pltpu.CompilerParams(dimension_semantics=("arbitrary",) * n_axes,
                                vmem_limit_bytes=VMEM_LIMIT)


def _bdot(a, b):
    return jnp.dot(a.astype(BF16), b.astype(BF16), preferred_element_type=F32)


def _bdot_nt(a, b):
    return lax.dot_general(a.astype(BF16), b.astype(BF16), (((1,), (1,)), ((), ())),
                           preferred_element_type=F32)


def _bdot_tn(a, b):
    return lax.dot_general(a.astype(BF16), b.astype(BF16), (((0,), (0,)), ((), ())),
                           preferred_element_type=F32)


def _split2(a):
    a1 = a.astype(BF16)
    a2 = (a - a1.astype(F32)).astype(BF16)
    return a1, a2


def _split3(a):
    a1 = a.astype(BF16)
    r = a - a1.astype(F32)
    a2 = r.astype(BF16)
    a3 = (r - a2.astype(F32)).astype(BF16)
    return a1, a2, a3


def _exact_lhs_dot(m_bf16, a):
    a1, a2, a3 = _split3(a)
    f = lambda z: jnp.dot(m_bf16, z, preferred_element_type=F32)
    return f(a1) + f(a2) + f(a3)


def _dot3(a, b):
    a1, a2 = _split2(a)
    b1, b2 = _split2(b)
    f = lambda x, y: jnp.dot(x, y, preferred_element_type=F32)
    return f(a1, b1) + (f(a1, b2) + f(a2, b1))


def _softplus(x):
    return jnp.maximum(x, 0.0) + jnp.log1p(jnp.exp(-jnp.abs(x)))


def _silu(x):
    return x * jax.nn.sigmoid(x)


def _mod_row(m, tm, n_prompt_rows, dec_seq):
    r0 = m * tm
    return jnp.where(r0 < n_prompt_rows, 0, 1 + (r0 - n_prompt_rows) // dec_seq)


def _adaln_kernel(c_ref, w_ref, b_ref, o_ref):
    s = _silu(c_ref[...])
    o_ref[...] = _dot3(s, w_ref[...]) + b_ref[...]


def _adaln(cvec8, mod_w, mod_b):
    depth, d, n = mod_w.shape
    tn = 1024
    return pl.pallas_call(
        _adaln_kernel,
        grid=(depth, n // tn),
        in_specs=[pl.BlockSpec((MOD_ROWS, d), lambda i, j: (0, 0)),
                  pl.BlockSpec((None, d, tn), lambda i, j: (i, 0, j)),
                  pl.BlockSpec((None, 1, tn), lambda i, j: (i, 0, j))],
        out_specs=pl.BlockSpec((None, MOD_ROWS, tn), lambda i, j: (i, 0, j)),
        out_shape=jax.ShapeDtypeStruct((depth, MOD_ROWS, n), F32),
        compiler_params=_cparams(2),
        name="adaln",
    )(cvec8, mod_w, mod_b.reshape(depth, 1, n))


def _rms(x):
    return x * lax.rsqrt(jnp.mean(x * x, axis=-1, keepdims=True) + EPS)


def _norm_mod_kernel(x_ref, g_ref, sh_ref, sc_ref, o_ref):
    y = _rms(x_ref[...]) * g_ref[...]
    o_ref[...] = (y * (1.0 + sc_ref[...]) + sh_ref[...]).astype(o_ref.dtype)


def _norm_mod_route_kernel(x_ref, g_ref, sh_ref, sc_ref, r_ref, o_ref, route_ref):
    y = _rms(x_ref[...]) * g_ref[...]
    h = y * (1.0 + sc_ref[...]) + sh_ref[...]
    o_ref[...] = h.astype(o_ref.dtype)
    logits = _dot3(h, r_ref[...])
    lane = lax.broadcasted_iota(jnp.int32, logits.shape, 1)
    neg = jnp.float32(-jnp.inf)
    lg = jnp.where(lane < N_EXPERTS, logits, neg)
    m1 = jnp.max(lg, axis=-1, keepdims=True)
    i1 = jnp.min(jnp.where(lg == m1, lane, LANE), axis=-1, keepdims=True)
    lg2 = jnp.where(lane == i1, neg, lg)
    m2 = jnp.max(lg2, axis=-1, keepdims=True)
    i2 = jnp.min(jnp.where(lg2 == m2, lane, LANE), axis=-1, keepdims=True)
    e2 = jnp.exp(m2 - m1)
    den = 1.0 + e2
    g1 = 1.0 / den
    g2 = e2 / den
    out = jnp.where(lane == 0, i1.astype(F32),
                    jnp.where(lane == 1, i2.astype(F32),
                              jnp.where(lane == 2, g1, jnp.where(lane == 3, g2, 0.0))))
    route_ref[...] = out


def _norm_mod(x, g, mods3, layer, which, n_prompt_rows, dec_seq, router=None):
    m_rows, d = x.shape
    tm = 512
    base = layer * MOD_ROWS * 6

    def mod_spec(k):
        return pl.BlockSpec((None, 1, d),
                            lambda m: (base + _mod_row(m, tm, n_prompt_rows, dec_seq) * 6 + k, 0, 0))

    in_specs = [pl.BlockSpec((tm, d), lambda m: (m, 0)),
                pl.BlockSpec((1, d), lambda m: (0, 0)),
                mod_spec(3 * which), mod_spec(3 * which + 1)]
    args = [x, g.reshape(1, d), mods3, mods3]
    h_spec = pl.BlockSpec((tm, d), lambda m: (m, 0))
    h_shape = jax.ShapeDtypeStruct((m_rows, d), BF16)
    if router is None:
        return pl.pallas_call(
            _norm_mod_kernel, grid=(m_rows // tm,), in_specs=in_specs, out_specs=h_spec,
            out_shape=h_shape, compiler_params=_cparams(1), name="norm_mod")(*args)
    r_pad = jnp.zeros((d, LANE), F32).at[:, :N_EXPERTS].set(router.astype(F32))
    return pl.pallas_call(
        _norm_mod_route_kernel, grid=(m_rows // tm,),
        in_specs=in_specs + [pl.BlockSpec((d, LANE), lambda m: (0, 0))],
        out_specs=(h_spec, pl.BlockSpec((tm, LANE), lambda m: (m, 0))),
        out_shape=(h_shape, jax.ShapeDtypeStruct((m_rows, LANE), F32)),
        compiler_params=_cparams(1), name="norm_mod_route")(*args, r_pad)


def _final_norm_kernel(x_ref, g_ref, o_ref):
    o_ref[...] = _rms(x_ref[...]) * g_ref[...]


def _final_norm(x, g):
    m_rows, d = x.shape
    tm = 512
    return pl.pallas_call(
        _final_norm_kernel, grid=(m_rows // tm,),
        in_specs=[pl.BlockSpec((tm, d), lambda m: (m, 0)), pl.BlockSpec((1, d), lambda m: (0, 0))],
        out_specs=pl.BlockSpec((tm, d), lambda m: (m, 0)),
        out_shape=jax.ShapeDtypeStruct((m_rows, d), F32),
        compiler_params=_cparams(1), name="final_norm")(x, g.reshape(1, d))


def _in_proj_kernel(x_ref, w_ref, o_ref):
    o_ref[...] = jnp.dot(x_ref[...], w_ref[...], preferred_element_type=F32)


def _in_proj(h, w_bf16):
    m_rows, k = h.shape
    n = w_bf16.shape[1]
    tm, tn = 1024, 1408
    return pl.pallas_call(
        _in_proj_kernel, grid=(n // tn, m_rows // tm),
        in_specs=[pl.BlockSpec((tm, k), lambda j, m: (m, 0)),
                  pl.BlockSpec((k, tn), lambda j, m: (0, j))],
        out_specs=pl.BlockSpec((tm, tn), lambda j, m: (m, j)),
        out_shape=jax.ShapeDtypeStruct((m_rows, n), F32),
        compiler_params=_cparams(2), name="in_proj")(h, w_bf16)


def _out_proj_kernel(y1_ref, y2_ref, y3_ref, w_ref, res_ref, gate_ref, o_ref, wbf_ref):
    @pl.when(pl.program_id(1) == 0)
    def _():
        wbf_ref[...] = w_ref[...].astype(BF16)

    k1 = y1_ref.shape[1]
    k2 = k1 + y2_ref.shape[1]
    acc = jnp.dot(y1_ref[...], wbf_ref[0:k1, :], preferred_element_type=F32)
    acc += jnp.dot(y2_ref[...], wbf_ref[k1:k2, :], preferred_element_type=F32)
    acc += jnp.dot(y3_ref[...], wbf_ref[k2:, :], preferred_element_type=F32)
    o_ref[...] = res_ref[...] + gate_ref[...] * acc


def _out_proj(y1, y2, y3, w, res, mods3, layer, n_prompt_rows, dec_seq):
    m_rows, d = res.shape
    k = w.shape[0]
    tm, tn = 512, 512
    base = layer * MOD_ROWS * 6
    gate_blocks = d // tn
    return pl.pallas_call(
        _out_proj_kernel, grid=(d // tn, m_rows // tm),
        in_specs=[pl.BlockSpec((tm, y1.shape[1]), lambda j, m: (m, 0)),
                  pl.BlockSpec((tm, y2.shape[1]), lambda j, m: (m, 0)),
                  pl.BlockSpec((tm, y3.shape[1]), lambda j, m: (m, 0)),
                  pl.BlockSpec((k, tn), lambda j, m: (0, j)),
                  pl.BlockSpec((tm, tn), lambda j, m: (m, j)),
                  pl.BlockSpec((None, 1, tn),
                               lambda j, m: (base + _mod_row(m, tm, n_prompt_rows, dec_seq) * 6 + 2, 0, j))],
        out_specs=pl.BlockSpec((tm, tn), lambda j, m: (m, j)),
        out_shape=jax.ShapeDtypeStruct((m_rows, d), F32),
        scratch_shapes=[pltpu.VMEM((k, tn), BF16)],
        compiler_params=_cparams(2), name="out_proj")(y1, y2, y3, w, res, mods3)


def _ffn_up_kernel(x_ref, w1_ref, w3_ref, o_ref, w1bf_ref, w3bf_ref):
    @pl.when(pl.program_id(1) == 0)
    def _():
        w1bf_ref[...] = w1_ref[...].astype(BF16)
        w3bf_ref[...] = w3_ref[...].astype(BF16)

    a = jnp.dot(x_ref[...], w1bf_ref[...], preferred_element_type=F32)
    b = jnp.dot(x_ref[...], w3bf_ref[...], preferred_element_type=F32)
    o_ref[...] = (_silu(a) * b).astype(o_ref.dtype)


def _ffn_up(h, w1, w3):
    m_rows, d = h.shape
    f = w1.shape[1]
    tm, tf = 512, 512
    return pl.pallas_call(
        _ffn_up_kernel, grid=(pl.cdiv(f, tf), m_rows // tm),
        in_specs=[pl.BlockSpec((tm, d), lambda j, m: (m, 0)),
                  pl.BlockSpec((d, tf), lambda j, m: (0, j)),
                  pl.BlockSpec((d, tf), lambda j, m: (0, j))],
        out_specs=pl.BlockSpec((tm, tf), lambda j, m: (m, j)),
        out_shape=jax.ShapeDtypeStruct((m_rows, f), BF16),
        scratch_shapes=[pltpu.VMEM((d, tf), BF16), pltpu.VMEM((d, tf), BF16)],
        compiler_params=_cparams(2), name="ffn_up")(h, w1, w3)


def _ffn_down_kernel(g_ref, w_ref, res_ref, gate_ref, o_ref, wbf_ref):
    @pl.when(pl.program_id(1) == 0)
    def _():
        wbf_ref[...] = w_ref[...].astype(BF16)

    acc = jnp.dot(g_ref[...], wbf_ref[...], preferred_element_type=F32)
    o_ref[...] = res_ref[...] + gate_ref[...] * acc


def _ffn_down(g, w2, res, mods3, layer, n_prompt_rows, dec_seq):
    m_rows, d = res.shape
    f = w2.shape[0]
    tm, tn = 512, 512
    base = layer * MOD_ROWS * 6
    return pl.pallas_call(
        _ffn_down_kernel, grid=(d // tn, m_rows // tm),
        in_specs=[pl.BlockSpec((tm, f), lambda j, m: (m, 0)),
                  pl.BlockSpec((f, tn), lambda j, m: (0, j)),
                  pl.BlockSpec((tm, tn), lambda j, m: (m, j)),
                  pl.BlockSpec((None, 1, tn),
                               lambda j, m: (base + _mod_row(m, tm, n_prompt_rows, dec_seq) * 6 + 5, 0, j))],
        out_specs=pl.BlockSpec((tm, tn), lambda j, m: (m, j)),
        out_shape=jax.ShapeDtypeStruct((m_rows, d), F32),
        scratch_shapes=[pltpu.VMEM((f, tn), BF16)],
        compiler_params=_cparams(2), name="ffn_down")(g, w2, res, mods3)


MOE_TM = 256


def _moe_up_kernel(te_ref, nu_ref, x_ref, w1_ref, w3_ref, o_ref, w1bf_ref, w3bf_ref):
    m = pl.program_id(1)
    prev = te_ref[jnp.maximum(m - 1, 0)]

    @pl.when((m == 0) | (te_ref[m] != prev))
    def _():
        w1bf_ref[...] = w1_ref[...].astype(BF16)
        w3bf_ref[...] = w3_ref[...].astype(BF16)

    @pl.when(m < nu_ref[0])
    def _():
        a = jnp.dot(x_ref[...], w1bf_ref[...], preferred_element_type=F32)
        b = jnp.dot(x_ref[...], w3bf_ref[...], preferred_element_type=F32)
        o_ref[...] = (_silu(a) * b).astype(o_ref.dtype)

    @pl.when(m >= nu_ref[0])
    def _():
        o_ref[...] = jnp.zeros_like(o_ref)


def _moe_up(xs, w1, w3, tile_expert, n_used):
    n_rows, d = xs.shape
    f = w1.shape[2]
    tm, tf = MOE_TM, 1024
    grid_spec = pltpu.PrefetchScalarGridSpec(
        num_scalar_prefetch=2, grid=(f // tf, n_rows // tm),
        in_specs=[pl.BlockSpec((tm, d), lambda j, m, te, nu: (m, 0)),
                  pl.BlockSpec((None, d, tf), lambda j, m, te, nu: (te[m], 0, j)),
                  pl.BlockSpec((None, d, tf), lambda j, m, te, nu: (te[m], 0, j))],
        out_specs=pl.BlockSpec((tm, tf), lambda j, m, te, nu: (m, j)),
        scratch_shapes=[pltpu.VMEM((d, tf), BF16), pltpu.VMEM((d, tf), BF16)])
    return pl.pallas_call(
        _moe_up_kernel, grid_spec=grid_spec,
        out_shape=jax.ShapeDtypeStruct((n_rows, f), BF16),
        compiler_params=_cparams(2), name="moe_up")(tile_expert, n_used, xs, w1, w3)


def _moe_down_kernel(te_ref, nu_ref, g_ref, w_ref, rg_ref, o_ref, wbf_ref):
    m = pl.program_id(1)
    prev = te_ref[jnp.maximum(m - 1, 0)]

    @pl.when((m == 0) | (te_ref[m] != prev))
    def _():
        wbf_ref[...] = w_ref[...].astype(BF16)

    @pl.when(m < nu_ref[0])
    def _():
        acc = jnp.dot(g_ref[...], wbf_ref[...], preferred_element_type=F32)
        o_ref[...] = rg_ref[:, 0:1] * acc

    @pl.when(m >= nu_ref[0])
    def _():
        o_ref[...] = jnp.zeros_like(o_ref)


def _moe_down(g, w2, row_gate, tile_expert, n_used):
    n_rows, f = g.shape
    d = w2.shape[2]
    tm, tn = MOE_TM, 512
    grid_spec = pltpu.PrefetchScalarGridSpec(
        num_scalar_prefetch=2, grid=(d // tn, n_rows // tm),
        in_specs=[pl.BlockSpec((tm, f), lambda j, m, te, nu: (m, 0)),
                  pl.BlockSpec((None, f, tn), lambda j, m, te, nu: (te[m], 0, j)),
                  pl.BlockSpec((tm, LANE), lambda j, m, te, nu: (m, 0))],
        out_specs=pl.BlockSpec((tm, tn), lambda j, m, te, nu: (m, j)),
        scratch_shapes=[pltpu.VMEM((f, tn), BF16)])
    return pl.pallas_call(
        _moe_down_kernel, grid_spec=grid_spec,
        out_shape=jax.ShapeDtypeStruct((n_rows, d), F32),
        compiler_params=_cparams(2), name="moe_down")(tile_expert, n_used, g, w2, row_gate)


def _combine_kernel(x_ref, ya_ref, yb_ref, gate_ref, o_ref):
    o_ref[...] = x_ref[...] + gate_ref[...] * (ya_ref[...] + yb_ref[...])


def _moe_combine(x, ya, yb, mods3, layer, n_prompt_rows, dec_seq):
    m_rows, d = x.shape
    tm = 512
    base = layer * MOD_ROWS * 6
    row = pl.BlockSpec((tm, d), lambda m: (m, 0))
    return pl.pallas_call(
        _combine_kernel, grid=(m_rows // tm,),
        in_specs=[row, row, row,
                  pl.BlockSpec((None, 1, d),
                               lambda m: (base + _mod_row(m, tm, n_prompt_rows, dec_seq) * 6 + 5, 0, 0))],
        out_specs=row, out_shape=jax.ShapeDtypeStruct((m_rows, d), F32),
        compiler_params=_cparams(1), name="moe_combine")(x, ya, yb, mods3)


def _moe_ffn(x, h, route, w1, w3, w2, mods3, layer, n_prompt_rows, dec_seq):
    m_rows, d = x.shape
    tm = MOE_TM
    n_slots = m_rows * TOP_K
    n_rows = n_slots + N_EXPERTS * tm
    n_tiles = n_rows // tm
    top_i = route[:, 0:TOP_K].astype(jnp.int32)
    top_g = route[:, TOP_K:2 * TOP_K]
    flat_e = top_i.reshape(n_slots)
    onehot = (flat_e[:, None] == jnp.arange(N_EXPERTS, dtype=jnp.int32)[None, :]).astype(jnp.int32)
    counts = jnp.sum(onehot, axis=0)
    rank = jnp.sum((jnp.cumsum(onehot, axis=0) - onehot) * onehot, axis=1)
    padded = ((counts + tm - 1) // tm) * tm
    group_end = jnp.cumsum(padded)
    group_start = group_end - padded
    pos = group_start[flat_e] + rank
    src_token = jnp.zeros((n_rows,), jnp.int32).at[pos].set(jnp.arange(n_slots, dtype=jnp.int32) // TOP_K)
    gate_sorted = jnp.zeros((n_rows,), F32).at[pos].set(top_g.reshape(n_slots))
    n_used = (group_end[-1] // tm).astype(jnp.int32)
    tile_start = jnp.minimum(jnp.arange(n_tiles, dtype=jnp.int32), n_used - 1) * tm
    tile_expert = jnp.sum((tile_start[:, None] >= group_end[None, :]).astype(jnp.int32), axis=1)
    tile_expert = jnp.minimum(tile_expert, N_EXPERTS - 1).astype(jnp.int32)
    xs = jnp.take(h, src_token, axis=0)
    row_gate = jnp.broadcast_to(gate_sorted[:, None], (n_rows, LANE))
    g = _moe_up(xs, w1, w3, tile_expert, n_used.reshape(1))
    ys = _moe_down(g, w2, row_gate, tile_expert, n_used.reshape(1))
    pos2 = pos.reshape(m_rows, TOP_K)
    ya = jnp.take(ys, pos2[:, 0], axis=0)
    yb = jnp.take(ys, pos2[:, 1], axis=0)
    return _moe_combine(x, ya, yb, mods3, layer, n_prompt_rows, dec_seq)


def _conv_cols(x_ref, w_ref, b_ref, o_ref, seg, act):
    length, ch = x_ref.shape
    t = lax.broadcasted_iota(jnp.int32, (length, LANE), 0) % seg

    def body(cb, carry):
        c0 = pl.multiple_of(cb * LANE, LANE)
        x = x_ref[:, pl.ds(c0, LANE)]
        w = w_ref[:, pl.ds(c0, LANE)]
        xm1 = jnp.where(t >= 1, pltpu.roll(x, 1, 0), 0.0)
        xp1 = jnp.where(t < seg - 1, pltpu.roll(x, length - 1, 0), 0.0)
        xp2 = jnp.where(t < seg - 2, pltpu.roll(x, length - 2, 0), 0.0)
        y = b_ref[:, pl.ds(c0, LANE)] + (w[0:1] * xm1 + w[1:2] * x + w[2:3] * xp1 + w[3:4] * xp2)
        if act:
            y = _silu(y)
        o_ref[:, pl.ds(c0, LANE)] = y
        return carry

    lax.fori_loop(0, ch // LANE, body, 0)


def _tri_masks(t_len):
    r = lax.broadcasted_iota(jnp.int32, (t_len, t_len), 0)
    c = lax.broadcasted_iota(jnp.int32, (t_len, t_len), 1)
    return r >= c, r <= c


def _ssd_kernel(z_ref, xs_ref, bc_ref, small_ref, wx_ref, bx_ref, wbc_ref, bbc_ref, dtb_ref, nega_ref,
                dvec_ref, ng_ref, h0_ref, y_ref, hout_ref, xc_ref, bcc_ref, yacc_ref, hst_ref, *, seg):
    length = z_ref.shape[0]
    t_len = SSD_T
    n_chunks = length // t_len
    _conv_cols(xs_ref, wx_ref, bx_ref, xc_ref, seg, True)
    _conv_cols(bc_ref, wbc_ref, bbc_ref, bcc_ref, seg, True)
    hst_ref[...] = h0_ref[...]
    yacc_ref[...] = xc_ref[...] * dvec_ref[...]
    lower, upper = _tri_masks(t_len)
    tri = (lower.astype(BF16), upper.astype(BF16))
    masks = (lower, upper)
    gn = SSD_N

    def chunk(c, d):
        r0 = pl.multiple_of(c * t_len, t_len)
        dtv = _softplus(small_ref[pl.ds(r0, t_len), :] + dtb_ref[d])
        a = nega_ref[d] * dtv
        cs = _exact_lhs_dot(tri[d], a)
        cst = cs.T
        end = t_len - 1 if d == 0 else 0
        bcv = bcc_ref[pl.ds(r0, t_len), :]
        gmat = []
        for g in range(SSD_G):
            bg = bcv[:, g * gn:(g + 1) * gn]
            cg = bcv[:, (SSD_G + g) * gn:(SSD_G + g + 1) * gn]
            gmat.append((_bdot_nt(cg, bg), bg, cg))
        for hp in range(SSD_H // 2):
            ys = []
            for h in (2 * hp, 2 * hp + 1):
                gm, bg, cg = gmat[h // (SSD_H // SSD_G)]
                col = cs[:, h:h + 1]
                row = cst[h:h + 1, :]
                lmat = jnp.where(masks[d], jnp.exp(col - row), 0.0)
                u = xc_ref[pl.ds(r0, t_len), h * SSD_P:(h + 1) * SSD_P] * dtv[:, h:h + 1]
                hs = hst_ref[d, h]
                y = _bdot(gm * lmat, u) + jnp.exp(col) * _bdot_nt(cg, hs)
                cend = cst[h:h + 1, end:end + 1]
                hst_ref[d, h] = jnp.exp(cend) * hs + _bdot_tn(u * jnp.exp(cend - col), bg)
                ys.append(y)
            yacc_ref[pl.ds(r0, t_len), hp * LANE:(hp + 1) * LANE] += jnp.concatenate(ys, axis=1)

    def body(j, carry):
        chunk(j, 0)
        chunk(n_chunks - 1 - j, 1)
        return carry

    lax.fori_loop(0, n_chunks, body, 0)
    hout_ref[...] = hst_ref[...]

    def finish(c, carry):
        r0 = pl.multiple_of(c * t_len, t_len)
        y = yacc_ref[pl.ds(r0, t_len), :] * _silu(z_ref[pl.ds(r0, t_len), :])
        y_ref[pl.ds(r0, t_len), :] = (_rms(y) * ng_ref[...]).astype(y_ref.dtype)
        return carry

    lax.fori_loop(0, n_chunks, finish, 0)


def _seq_spec(length, width, col_block, row_off):
    return pl.BlockSpec((length, width), lambda b: (b + row_off, col_block))


def _full_spec(shape):
    nd = len(shape)
    return pl.BlockSpec(shape, lambda b: (0,) * nd)


def _ssd_call(proj, p, h0, n_seq, length, row_off, seg):
    dtb = jnp.zeros((2, 1, LANE), F32).at[:, 0, :SSD_H].set(p['ssd_dt_bias'].astype(F32))
    nega = jnp.zeros((2, 1, LANE), F32).at[:, 0, :SSD_H].set(-jnp.exp(p['ssd_A_log'].astype(F32)))
    dvec = jnp.repeat(p['ssd_D'].astype(F32), SSD_P).reshape(1, SSD_D)
    cw = p['ssd_conv_w'].astype(F32)
    cb = p['ssd_conv_b'].astype(F32).reshape(1, SSD_CONV_CH)
    st_shape = (2, SSD_H, SSD_P, SSD_N)
    st_spec = pl.BlockSpec((None,) + st_shape, lambda b: (b, 0, 0, 0, 0))
    return pl.pallas_call(
        functools.partial(_ssd_kernel, seg=seg), grid=(n_seq,),
        in_specs=[_seq_spec(length, SSD_D, COL_Z // SSD_D, row_off),
                  _seq_spec(length, SSD_D, COL_XS // SSD_D, row_off),
                  _seq_spec(length, 512, COL_BC // 512, row_off),
                  _seq_spec(length, LANE, COL_SMALL // LANE, row_off),
                  _full_spec((4, SSD_D)), _full_spec((1, SSD_D)),
                  _full_spec((4, 512)), _full_spec((1, 512)),
                  _full_spec((2, 1, LANE)), _full_spec((2, 1, LANE)),
                  _full_spec((1, SSD_D)), _full_spec((1, SSD_D)), st_spec],
        out_specs=(pl.BlockSpec((length, SSD_D), lambda b: (b, 0)), st_spec),
        out_shape=(jax.ShapeDtypeStruct((n_seq * length, SSD_D), BF16),
                   jax.ShapeDtypeStruct((n_seq,) + st_shape, F32)),
        scratch_shapes=[pltpu.VMEM((length, SSD_D), F32), pltpu.VMEM((length, 512), F32),
                        pltpu.VMEM((length, SSD_D), F32), pltpu.VMEM(st_shape, F32)],
        compiler_params=_cparams(1), name="ssd_mixer",
    )(proj, proj, proj, proj, cw[:, :SSD_D], cb[:, :SSD_D], cw[:, SSD_D:], cb[:, SSD_D:], dtb, nega,
      dvec, p['ssd_norm_g'].astype(F32).reshape(1, SSD_D), h0)


def _log_sigmoid(x):
    return jnp.minimum(x, 0.0) - jnp.log1p(jnp.exp(-jnp.abs(x)))


def _gla_kernel(q_ref, k_ref, v_ref, g_ref, small_ref, gw_ref, gb_ref, ng_ref, s0_ref, y_ref, sout_ref,
                oacc_ref, st_ref):
    length = q_ref.shape[0]
    t_len = GLA_T
    n_chunks = length // t_len
    st_ref[...] = s0_ref[...]
    oacc_ref[...] = jnp.zeros_like(oacc_ref)
    lower, upper = _tri_masks(t_len)
    tri = (lower.astype(BF16), upper.astype(BF16))
    masks = (lower, upper)
    scale = GLA_DK ** -0.5

    def chunk(c, d):
        r0 = pl.multiple_of(c * t_len, t_len)
        logit = _dot3(small_ref[pl.ds(r0, t_len), :], gw_ref[d]) + gb_ref[d]
        la = _log_sigmoid(logit) / GLA_TAU
        bc = _exact_lhs_dot(tri[d], la)
        end = t_len - 1 if d == 0 else 0
        mid = bc[t_len // 2:t_len // 2 + 1, :]
        tot = bc[end:end + 1, :]
        q = q_ref[pl.ds(r0, t_len), :] * scale
        k = k_ref[pl.ds(r0, t_len), :]
        qt = q * jnp.exp(bc - mid)
        kt = k * jnp.exp(mid - bc)
        qs = q * jnp.exp(bc)
        kd = k * jnp.exp(tot - bc)
        etot = jnp.exp(tot)
        for h in range(GLA_H):
            ks = slice(h * GLA_DK, (h + 1) * GLA_DK)
            vh = v_ref[pl.ds(r0, t_len), h * GLA_DV:(h + 1) * GLA_DV]
            att = jnp.where(masks[d], _bdot_nt(qt[:, ks], kt[:, ks]), 0.0)
            stt = st_ref[d, h]
            o = _bdot(att, vh) + _bdot_nt(qs[:, ks], stt)
            st_ref[d, h] = stt * etot[:, ks] + _bdot_tn(vh, kd[:, ks])
            oacc_ref[pl.ds(r0, t_len), h * GLA_DV:(h + 1) * GLA_DV] += o

    def body(j, carry):
        chunk(j, 0)
        chunk(n_chunks - 1 - j, 1)
        return carry

    lax.fori_loop(0, n_chunks, body, 0)
    sout_ref[...] = st_ref[...]

    def finish(c, carry):
        r0 = pl.multiple_of(c * t_len, t_len)
        gate = _silu(g_ref[pl.ds(r0, t_len), :])
        for h in range(GLA_H):
            vs = slice(h * GLA_DV, (h + 1) * GLA_DV)
            o = _rms(oacc_ref[pl.ds(r0, t_len), vs]) * ng_ref[...]
            y_ref[pl.ds(r0, t_len), vs] = (o * gate[:, vs]).astype(y_ref.dtype)
        return carry

    lax.fori_loop(0, n_chunks, finish, 0)


def _gla_call(proj, p, s0t, n_seq, length, row_off):
    gw = jnp.zeros((2, LANE, GLA_KW), F32).at[:, GLA_RANK:2 * GLA_RANK, :].set(p['gla_gate_w'].astype(F32))
    gb = p['gla_gate_b'].astype(F32).reshape(2, 1, GLA_KW)
    st_shape = (2, GLA_H, GLA_DV, GLA_DK)
    st_spec = pl.BlockSpec((None,) + st_shape, lambda b: (b, 0, 0, 0, 0))
    return pl.pallas_call(
        _gla_kernel, grid=(n_seq,),
        in_specs=[_seq_spec(length, GLA_KW, COL_Q // GLA_KW, row_off),
                  _seq_spec(length, GLA_KW, COL_K // GLA_KW, row_off),
                  _seq_spec(length, GLA_VW, COL_V // GLA_VW, row_off),
                  _seq_spec(length, GLA_VW, COL_G // GLA_VW, row_off),
                  _seq_spec(length, LANE, COL_SMALL // LANE, row_off),
                  _full_spec((2, LANE, GLA_KW)), _full_spec((2, 1, GLA_KW)), _full_spec((1, GLA_DV)),
                  st_spec],
        out_specs=(pl.BlockSpec((length, GLA_VW), lambda b: (b, 0)), st_spec),
        out_shape=(jax.ShapeDtypeStruct((n_seq * length, GLA_VW), BF16),
                   jax.ShapeDtypeStruct((n_seq,) + st_shape, F32)),
        scratch_shapes=[pltpu.VMEM((length, GLA_VW), F32), pltpu.VMEM(st_shape, F32)],
        compiler_params=_cparams(1), name="gla_mixer",
    )(proj, proj, proj, proj, proj, gw, gb, p['gla_norm_g'].astype(F32).reshape(1, GLA_DV), s0t)


def _lru_kernel(xb_ref, gb_ref, cw_ref, cb_ref, w_ref, bias_ref, sp_ref, h0_ref, y_ref, hout_ref,
                xr_ref, a_ref, u_ref, *, seg):
    length = xb_ref.shape[0]
    blk = 256
    _conv_cols(xb_ref, cw_ref, cb_ref, xr_ref, seg, False)

    def gates(c, carry):
        r0 = pl.multiple_of(c * blk, blk)
        xr = xr_ref[pl.ds(r0, blk), :]
        xbf = xr.astype(BF16)
        for d in range(2):
            pre = jnp.dot(xbf, w_ref[d], preferred_element_type=F32) + bias_ref[d]
            r = jax.nn.sigmoid(pre[:, :LRU_W])
            i = jax.nn.sigmoid(pre[:, LRU_W:])
            log_a = (-LRU_C) * r * sp_ref[d]
            a = jnp.exp(log_a)
            a_ref[d, pl.ds(r0, blk), :] = a
            u_ref[d, pl.ds(r0, blk), :] = jnp.sqrt(-jnp.tanh(log_a) * (a * a + 1.0)) * (i * xr)
        return carry

    lax.fori_loop(0, length // blk, gates, 0)

    def scan(t, carry):
        hf, hb = carry
        tb = length - 1 - t
        hf = a_ref[0, pl.ds(t, 1), :] * hf + u_ref[0, pl.ds(t, 1), :]
        hb = a_ref[1, pl.ds(tb, 1), :] * hb + u_ref[1, pl.ds(tb, 1), :]
        u_ref[0, pl.ds(t, 1), :] = hf
        u_ref[1, pl.ds(tb, 1), :] = hb
        return hf, hb

    hf, hb = lax.fori_loop(0, length, scan, (h0_ref[0], h0_ref[1]))
    hout_ref[0] = hf
    hout_ref[1] = hb

    def finish(c, carry):
        r0 = pl.multiple_of(c * blk, blk)
        y = (u_ref[0, pl.ds(r0, blk), :] + u_ref[1, pl.ds(r0, blk), :]) * jax.nn.gelu(gb_ref[pl.ds(r0, blk), :])
        y_ref[pl.ds(r0, blk), :] = y.astype(y_ref.dtype)
        return carry

    lax.fori_loop(0, length // blk, finish, 0)


def _block_diag(w):
    nb, bw, _ = w.shape
    eye = jnp.eye(nb, dtype=w.dtype)
    return (eye[:, None, :, None] * w[:, :, None, :]).reshape(nb * bw, nb * bw)


def _lru_call(proj, p, h0, n_seq, length, row_off, seg):
    w = jnp.stack([jnp.concatenate([_block_diag(p['lru_wa'][d].astype(F32)),
                                    _block_diag(p['lru_wx'][d].astype(F32))], axis=1) for d in range(2)])
    bias = jnp.stack([jnp.concatenate([p['lru_ba'][d], p['lru_bx'][d]]) for d in range(2)]).astype(F32)
    sp = jax.nn.softplus(-p['lru_lambda'].astype(F32)).reshape(2, 1, LRU_W)
    st_shape = (2, 1, LRU_W)
    st_spec = pl.BlockSpec((None,) + st_shape, lambda b: (b, 0, 0, 0))
    return pl.pallas_call(
        functools.partial(_lru_kernel, seg=seg), grid=(n_seq,),
        in_specs=[_seq_spec(length, LRU_W, COL_XB // LRU_W, row_off),
                  _seq_spec(length, LRU_W, COL_GB // LRU_W, row_off),
                  _full_spec((4, LRU_W)), _full_spec((1, LRU_W)),
                  _full_spec((2, LRU_W, 2 * LRU_W)), _full_spec((2, 1, 2 * LRU_W)),
                  _full_spec((2, 1, LRU_W)), st_spec],
        out_specs=(pl.BlockSpec((length, LRU_W), lambda b: (b, 0)), st_spec),
        out_shape=(jax.ShapeDtypeStruct((n_seq * length, LRU_W), BF16),
                   jax.ShapeDtypeStruct((n_seq,) + st_shape, F32)),
        scratch_shapes=[pltpu.VMEM((length, LRU_W), F32), pltpu.VMEM((2, length, LRU_W), F32),
                        pltpu.VMEM((2, length, LRU_W), F32)],
        compiler_params=_cparams(1), name="lru_mixer",
    )(proj, proj, p['lru_conv_w'].astype(F32), p['lru_conv_b'].astype(F32).reshape(1, LRU_W),
      w.astype(BF16), bias.reshape(2, 1, 2 * LRU_W), sp, h0)


def _permute_in_w(w):
    d = w.shape[0]
    o_dt = SSD_D + SSD_CONV_CH
    o_q = o_dt + SSD_H
    o_graw = o_q + 2 * GLA_KW + 2 * GLA_VW
    o_xb = o_graw + GLA_RANK
    cols = [w[:, :o_dt], w[:, o_q:o_graw], w[:, o_xb:], w[:, o_dt:o_q], w[:, o_graw:o_xb]]
    used = COL_SMALL + SSD_H + GLA_RANK
    cols.append(jnp.zeros((d, IN_PAD - used), w.dtype))
    return jnp.concatenate(cols, axis=1).astype(BF16)


def kernel(x_prompt, x_sample, state_ssd, state_gla, state_lru, c, c_ctx, mod_w, mod_b, norm1_g, norm2_g, in_w, ssd_conv_w, ssd_conv_b, ssd_A_log, ssd_dt_bias, ssd_D, ssd_norm_g, gla_gate_w, gla_gate_b, gla_norm_g, lru_conv_w, lru_conv_b, lru_wa, lru_ba, lru_wx, lru_bx, lru_lambda, out_w, ffn_w1, ffn_w3, ffn_w2, moe_router, moe_w1, moe_w3, moe_w2, final_norm_g):
    bp, lp, d = x_prompt.shape
    bs, ls, _ = x_sample.shape
    n_p = bp * lp
    n_s = bs * ls
    depth = in_w.shape[0]
    assert n_p % ls == 0 and 1 + bs <= MOD_ROWS

    cvec = jnp.zeros((MOD_ROWS, d), F32).at[0].set(c_ctx.astype(F32)).at[1:1 + bs].set(c.astype(F32))
    mods = _adaln(cvec, mod_w, mod_b)
    mods3 = mods.reshape(depth * MOD_ROWS * 6, 1, d)
    x = jnp.concatenate([x_prompt.reshape(n_p, d), x_sample.reshape(n_s, d)], axis=0)

    zero_ssd = jnp.zeros((bp, 2, SSD_H, SSD_P, SSD_N), F32)
    zero_gla = jnp.zeros((bp, 2, GLA_H, GLA_DV, GLA_DK), F32)
    zero_lru = jnp.zeros((bp, 2, 1, LRU_W), F32)
    new_ssd, new_gla, new_lru = [], [], []
    for i in range(depth):
        p = {'ssd_conv_w': ssd_conv_w[i], 'ssd_conv_b': ssd_conv_b[i], 'ssd_A_log': ssd_A_log[i],
             'ssd_dt_bias': ssd_dt_bias[i], 'ssd_D': ssd_D[i], 'ssd_norm_g': ssd_norm_g[i],
             'gla_gate_w': gla_gate_w[i], 'gla_gate_b': gla_gate_b[i], 'gla_norm_g': gla_norm_g[i],
             'lru_conv_w': lru_conv_w[i], 'lru_conv_b': lru_conv_b[i], 'lru_wa': lru_wa[i],
             'lru_ba': lru_ba[i], 'lru_wx': lru_wx[i], 'lru_bx': lru_bx[i], 'lru_lambda': lru_lambda[i]}
        h = _norm_mod(x, norm1_g[i], mods3, i, 0, n_p, ls)
        proj = _in_proj(h, _permute_in_w(in_w[i]))
        yp_ssd, s_ssd = _ssd_call(proj, p, zero_ssd, bp, lp, 0, lp)
        ys_ssd, _ = _ssd_call(proj, p, state_ssd[:, i].astype(F32), bs, ls, n_p // ls, GRID_W)
        yp_gla, s_gla = _gla_call(proj, p, zero_gla, bp, lp, 0)
        ys_gla, _ = _gla_call(proj, p, jnp.swapaxes(state_gla[:, i].astype(F32), -1, -2), bs, ls, n_p // ls)
        yp_lru, s_lru = _lru_call(proj, p, zero_lru, bp, lp, 0, lp)
        ys_lru, _ = _lru_call(proj, p, state_lru[:, i].astype(F32).reshape(bs, 2, 1, LRU_W), bs, ls,
                              n_p // ls, GRID_W)
        new_ssd.append(s_ssd)
        new_gla.append(jnp.swapaxes(s_gla, -1, -2))
        new_lru.append(s_lru.reshape(bp, 2, LRU_W))
        x = _out_proj(jnp.concatenate([yp_ssd, ys_ssd], axis=0), jnp.concatenate([yp_gla, ys_gla], axis=0),
                      jnp.concatenate([yp_lru, ys_lru], axis=0), out_w[i], x, mods3, i, n_p, ls)
        j = i // 2
        if i % 2 == 1:
            h2, route = _norm_mod(x, norm2_g[i], mods3, i, 1, n_p, ls, router=moe_router[j])
            x = _moe_ffn(x, h2, route, moe_w1[j], moe_w3[j], moe_w2[j], mods3, i, n_p, ls)
        else:
            h2 = _norm_mod(x, norm2_g[i], mods3, i, 1, n_p, ls)
            x = _ffn_down(_ffn_up(h2, ffn_w1[j], ffn_w3[j]), ffn_w2[j], x, mods3, i, n_p, ls)
    y = _final_norm(x, final_norm_g)
    return (y[:n_p].reshape(bp, lp, d), y[n_p:].reshape(bs, ls, d),
            jnp.stack(new_ssd, axis=1), jnp.stack(new_gla, axis=1), jnp.stack(new_lru, axis=1))
```

```python
import functools

import jax
import jax.numpy as jnp
from jax import lax
from jax.experimental import pallas as pl
from jax.experimental.pallas import tpu as pltpu

F32 = jnp.float32
BF16 = jnp.bfloat16

D_MODEL = 2048
GRID_W = 64
SSD_D = D_MODEL // 2
SSD_P = 64
SSD_H = SSD_D // SSD_P
SSD_G = 2
SSD_N = 128
GLA_H = 4
GLA_VW = D_MODEL // 4
GLA_DV = GLA_VW // GLA_H
GLA_DK = GLA_DV // 2
GLA_KW = GLA_H * GLA_DK
GLA_RANK = 16
GLA_TAU = 16.0
LRU_W = D_MODEL // 4
LRU_NB = 8
LRU_BW = LRU_W // LRU_NB
LRU_C = 8.0
SSD_CONV_CH = SSD_D + 2 * SSD_G * SSD_N
N_EXPERTS = 8
TOP_K = 2
EPS = 1e-6

LANE = 128
SSD_T = 128
GLA_T = 64
MOD_ROWS = 8
VMEM_LIMIT = 56 * 1024 * 1024

COL_Z = 0
COL_XS = 1024
COL_BC = 2048
COL_Q = 2560
COL_K = 2816
COL_V = 3072
COL_G = 3584
COL_XB = 4096
COL_GB = 4608
COL_SMALL = 5120
IN_PAD = 5632


def _cparams(n_axes):
    return pltpu.CompilerParams(dimension_semantics=("arbitrary",) * n_axes,
                                vmem_limit_bytes=VMEM_LIMIT)


def _bdot(a, b):
    return jnp.dot(a.astype(BF16), b.astype(BF16), preferred_element_type=F32)


def _bdot_nt(a, b):
    return lax.dot_general(a.astype(BF16), b.astype(BF16), (((1,), (1,)), ((), ())),
                           preferred_element_type=F32)


def _bdot_tn(a, b):
    return lax.dot_general(a.astype(BF16), b.astype(BF16), (((0,), (0,)), ((), ())),
                           preferred_element_type=F32)


def _split2(a):
    a1 = a.astype(BF16)
    a2 = (a - a1.astype(F32)).astype(BF16)
    return a1, a2


def _split3(a):
    a1 = a.astype(BF16)
    r = a - a1.astype(F32)
    a2 = r.astype(BF16)
    a3 = (r - a2.astype(F32)).astype(BF16)
    return a1, a2, a3


def _exact_lhs_dot(m_bf16, a):
    a1, a2, a3 = _split3(a)
    f = lambda z: jnp.dot(m_bf16, z, preferred_element_type=F32)
    return f(a1) + f(a2) + f(a3)


def _dot3(a, b):
    a1, a2 = _split2(a)
    b1, b2 = _split2(b)
    f = lambda x, y: jnp.dot(x, y, preferred_element_type=F32)
    return f(a1, b1) + (f(a1, b2) + f(a2, b1))


def _softplus(x):
    return jnp.maximum(x, 0.0) + jnp.log1p(jnp.exp(-jnp.abs(x)))


def _silu(x):
    return x * jax.nn.sigmoid(x)


def _mod_row(m, tm, n_prompt_rows, dec_seq):
    r0 = m * tm
    return jnp.where(r0 < n_prompt_rows, 0, 1 + (r0 - n_prompt_rows) // dec_seq)


def _adaln_kernel(c_ref, w_ref, b_ref, o_ref):
    s = _silu(c_ref[...])
    o_ref[...] = _dot3(s, w_ref[...]) + b_ref[...]


def _adaln(cvec8, mod_w, mod_b):
    depth, d, n = mod_w.shape
    tn = 1024
    return pl.pallas_call(
        _adaln_kernel,
        grid=(depth, n // tn),
        in_specs=[pl.BlockSpec((MOD_ROWS, d), lambda i, j: (0, 0)),
                  pl.BlockSpec((None, d, tn), lambda i, j: (i, 0, j)),
                  pl.BlockSpec((None, 1, tn), lambda i, j: (i, 0, j))],
        out_specs=pl.BlockSpec((None, MOD_ROWS, tn), lambda i, j: (i, 0, j)),
        out_shape=jax.ShapeDtypeStruct((depth, MOD_ROWS, n), F32),
        compiler_params=_cparams(2),
        name="adaln",
    )(cvec8, mod_w, mod_b.reshape(depth, 1, n))


def _rms(x):
    return x * lax.rsqrt(jnp.mean(x * x, axis=-1, keepdims=True) + EPS)


def _norm_mod_kernel(x_ref, g_ref, sh_ref, sc_ref, o_ref):
    y = _rms(x_ref[...]) * g_ref[...]
    o_ref[...] = (y * (1.0 + sc_ref[...]) + sh_ref[...]).astype(o_ref.dtype)


def _norm_mod_route_kernel(x_ref, g_ref, sh_ref, sc_ref, r_ref, o_ref, route_ref):
    y = _rms(x_ref[...]) * g_ref[...]
    h = y * (1.0 + sc_ref[...]) + sh_ref[...]
    o_ref[...] = h.astype(o_ref.dtype)
    logits = _dot3(h, r_ref[...])
    lane = lax.broadcasted_iota(jnp.int32, logits.shape, 1)
    neg = jnp.float32(-jnp.inf)
    lg = jnp.where(lane < N_EXPERTS, logits, neg)
    m1 = jnp.max(lg, axis=-1, keepdims=True)
    i1 = jnp.min(jnp.where(lg == m1, lane, LANE), axis=-1, keepdims=True)
    lg2 = jnp.where(lane == i1, neg, lg)
    m2 = jnp.max(lg2, axis=-1, keepdims=True)
    i2 = jnp.min(jnp.where(lg2 == m2, lane, LANE), axis=-1, keepdims=True)
    e2 = jnp.exp(m2 - m1)
    den = 1.0 + e2
    g1 = 1.0 / den
    g2 = e2 / den
    out = jnp.where(lane == 0, i1.astype(F32),
                    jnp.where(lane == 1, i2.astype(F32),
                              jnp.where(lane == 2, g1, jnp.where(lane == 3, g2, 0.0))))
    route_ref[...] = out


def _norm_mod(x, g, mods3, layer, which, n_prompt_rows, dec_seq, router=None):
    m_rows, d = x.shape
    tm = 512
    base = layer * MOD_ROWS * 6

    def mod_spec(k):
        return pl.BlockSpec((None, 1, d),
                            lambda m: (base + _mod_row(m, tm, n_prompt_rows, dec_seq) * 6 + k, 0, 0))

    in_specs = [pl.BlockSpec((tm, d), lambda m: (m, 0)),
                pl.BlockSpec((1, d), lambda m: (0, 0)),
                mod_spec(3 * which), mod_spec(3 * which + 1)]
    args = [x, g.reshape(1, d), mods3, mods3]
    h_spec = pl.BlockSpec((tm, d), lambda m: (m, 0))
    if router is None:
        return pl.pallas_call(
            _norm_mod_kernel, grid=(m_rows // tm,), in_specs=in_specs, out_specs=h_spec,
            out_shape=jax.ShapeDtypeStruct((m_rows, d), BF16),
            compiler_params=_cparams(1), name="norm_mod")(*args)
    r_pad = jnp.zeros((d, LANE), F32).at[:, :N_EXPERTS].set(router.astype(F32))
    return pl.pallas_call(
        _norm_mod_route_kernel, grid=(m_rows // tm,),
        in_specs=in_specs + [pl.BlockSpec((d, LANE), lambda m: (0, 0))],
        out_specs=(h_spec, pl.BlockSpec((tm, LANE), lambda m: (m, 0))),
        out_shape=(jax.ShapeDtypeStruct((m_rows, d), F32), jax.ShapeDtypeStruct((m_rows, LANE), F32)),
        compiler_params=_cparams(1), name="norm_mod_route")(*args, r_pad)


def _final_norm_kernel(x_ref, g_ref, o_ref):
    o_ref[...] = _rms(x_ref[...]) * g_ref[...]


def _final_norm(x, g, row0, n_rows):
    d = x.shape[1]
    tm = 512
    off = row0 // tm
    return pl.pallas_call(
        _final_norm_kernel, grid=(n_rows // tm,),
        in_specs=[pl.BlockSpec((tm, d), lambda m: (m + off, 0)), pl.BlockSpec((1, d), lambda m: (0, 0))],
        out_specs=pl.BlockSpec((tm, d), lambda m: (m, 0)),
        out_shape=jax.ShapeDtypeStruct((n_rows, d), F32),
        compiler_params=_cparams(1), name="final_norm")(x, g.reshape(1, d))


def _in_proj_kernel(x_ref, w_ref, o_ref):
    o_ref[...] = jnp.dot(x_ref[...], w_ref[...], preferred_element_type=F32)


def _in_proj(h, w_bf16):
    m_rows, k = h.shape
    n = w_bf16.shape[1]
    tm, tn = 1024, 1408
    return pl.pallas_call(
        _in_proj_kernel, grid=(n // tn, m_rows // tm),
        in_specs=[pl.BlockSpec((tm, k), lambda j, m: (m, 0)),
                  pl.BlockSpec((k, tn), lambda j, m: (0, j))],
        out_specs=pl.BlockSpec((tm, tn), lambda j, m: (m, j)),
        out_shape=jax.ShapeDtypeStruct((m_rows, n), F32),
        compiler_params=_cparams(2), name="in_proj")(h, w_bf16)


def _out_proj_kernel(y1_ref, y2_ref, y3_ref, w_ref, res_ref, gate_ref, o_ref, wbf_ref):
    @pl.when(pl.program_id(1) == 0)
    def _():
        wbf_ref[...] = w_ref[...].astype(BF16)

    k1 = y1_ref.shape[1]
    k2 = k1 + y2_ref.shape[1]
    acc = jnp.dot(y1_ref[...], wbf_ref[0:k1, :], preferred_element_type=F32)
    acc += jnp.dot(y2_ref[...], wbf_ref[k1:k2, :], preferred_element_type=F32)
    acc += jnp.dot(y3_ref[...], wbf_ref[k2:, :], preferred_element_type=F32)
    o_ref[...] = res_ref[...] + gate_ref[...] * acc


def _out_proj(y1, y2, y3, w, res, mods3, layer, n_prompt_rows, dec_seq):
    m_rows, d = res.shape
    k = w.shape[0]
    tm, tn = 512, 512
    base = layer * MOD_ROWS * 6
    return pl.pallas_call(
        _out_proj_kernel, grid=(d // tn, m_rows // tm),
        in_specs=[pl.BlockSpec((tm, y1.shape[1]), lambda j, m: (m, 0)),
                  pl.BlockSpec((tm, y2.shape[1]), lambda j, m: (m, 0)),
                  pl.BlockSpec((tm, y3.shape[1]), lambda j, m: (m, 0)),
                  pl.BlockSpec((k, tn), lambda j, m: (0, j)),
                  pl.BlockSpec((tm, tn), lambda j, m: (m, j)),
                  pl.BlockSpec((None, 1, tn),
                               lambda j, m: (base + _mod_row(m, tm, n_prompt_rows, dec_seq) * 6 + 2, 0, j))],
        out_specs=pl.BlockSpec((tm, tn), lambda j, m: (m, j)),
        out_shape=jax.ShapeDtypeStruct((m_rows, d), F32),
        scratch_shapes=[pltpu.VMEM((k, tn), BF16)],
        compiler_params=_cparams(2), name="out_proj")(y1, y2, y3, w, res, mods3)


def _ffn_up_kernel(x_ref, w1_ref, w3_ref, o_ref, w1bf_ref, w3bf_ref):
    @pl.when(pl.program_id(1) == 0)
    def _():
        w1bf_ref[...] = w1_ref[...].astype(BF16)
        w3bf_ref[...] = w3_ref[...].astype(BF16)

    a = jnp.dot(x_ref[...], w1bf_ref[...], preferred_element_type=F32)
    b = jnp.dot(x_ref[...], w3bf_ref[...], preferred_element_type=F32)
    o_ref[...] = (_silu(a) * b).astype(o_ref.dtype)


def _ffn_up(h, w1, w3):
    m_rows, d = h.shape
    f = w1.shape[1]
    tm, tf = 512, 512
    return pl.pallas_call(
        _ffn_up_kernel, grid=(pl.cdiv(f, tf), m_rows // tm),
        in_specs=[pl.BlockSpec((tm, d), lambda j, m: (m, 0)),
                  pl.BlockSpec((d, tf), lambda j, m: (0, j)),
                  pl.BlockSpec((d, tf), lambda j, m: (0, j))],
        out_specs=pl.BlockSpec((tm, tf), lambda j, m: (m, j)),
        out_shape=jax.ShapeDtypeStruct((m_rows, f), BF16),
        scratch_shapes=[pltpu.VMEM((d, tf), BF16), pltpu.VMEM((d, tf), BF16)],
        compiler_params=_cparams(2), name="ffn_up")(h, w1, w3)


def _ffn_down_kernel(g_ref, w_ref, res_ref, gate_ref, o_ref, wbf_ref):
    @pl.when(pl.program_id(1) == 0)
    def _():
        wbf_ref[...] = w_ref[...].astype(BF16)

    acc = jnp.dot(g_ref[...], wbf_ref[...], preferred_element_type=F32)
    o_ref[...] = res_ref[...] + gate_ref[...] * acc


def _ffn_down(g, w2, res, mods3, layer, n_prompt_rows, dec_seq):
    m_rows, d = res.shape
    f = w2.shape[0]
    tm, tn = 512, 512
    base = layer * MOD_ROWS * 6
    return pl.pallas_call(
        _ffn_down_kernel, grid=(d // tn, m_rows // tm),
        in_specs=[pl.BlockSpec((tm, f), lambda j, m: (m, 0)),
                  pl.BlockSpec((f, tn), lambda j, m: (0, j)),
                  pl.BlockSpec((tm, tn), lambda j, m: (m, j)),
                  pl.BlockSpec((None, 1, tn),
                               lambda j, m: (base + _mod_row(m, tm, n_prompt_rows, dec_seq) * 6 + 5, 0, j))],
        out_specs=pl.BlockSpec((tm, tn), lambda j, m: (m, j)),
        out_shape=jax.ShapeDtypeStruct((m_rows, d), F32),
        scratch_shapes=[pltpu.VMEM((f, tn), BF16)],
        compiler_params=_cparams(2), name="ffn_down")(g, w2, res, mods3)


MOE_TM = 256


def _moe_up_kernel(te_ref, nu_ref, x_ref, w1_ref, w3_ref, o_ref, w1bf_ref, w3bf_ref):
    m = pl.program_id(1)
    prev = te_ref[jnp.maximum(m - 1, 0)]

    @pl.when((m == 0) | (te_ref[m] != prev))
    def _():
        w1bf_ref[...] = w1_ref[...].astype(BF16)
        w3bf_ref[...] = w3_ref[...].astype(BF16)

    @pl.when(m < nu_ref[0])
    def _():
        x = x_ref[...].astype(BF16)
        a = jnp.dot(x, w1bf_ref[...], preferred_element_type=F32)
        b = jnp.dot(x, w3bf_ref[...], preferred_element_type=F32)
        o_ref[...] = (_silu(a) * b).astype(o_ref.dtype)

    @pl.when(m >= nu_ref[0])
    def _():
        o_ref[...] = jnp.zeros_like(o_ref)


def _moe_up(xs, w1, w3, tile_expert, n_used):
    n_rows, d = xs.shape
    f = w1.shape[2]
    tm, tf = MOE_TM, 1024
    grid_spec = pltpu.PrefetchScalarGridSpec(
        num_scalar_prefetch=2, grid=(f // tf, n_rows // tm),
        in_specs=[pl.BlockSpec((tm, d), lambda j, m, te, nu: (m, 0)),
                  pl.BlockSpec((None, d, tf), lambda j, m, te, nu: (te[m], 0, j)),
                  pl.BlockSpec((None, d, tf), lambda j, m, te, nu: (te[m], 0, j))],
        out_specs=pl.BlockSpec((tm, tf), lambda j, m, te, nu: (m, j)),
        scratch_shapes=[pltpu.VMEM((d, tf), BF16), pltpu.VMEM((d, tf), BF16)])
    return pl.pallas_call(
        _moe_up_kernel, grid_spec=grid_spec,
        out_shape=jax.ShapeDtypeStruct((n_rows, f), BF16),
        compiler_params=_cparams(2), name="moe_up")(tile_expert, n_used, xs, w1, w3)


def _moe_down_kernel(te_ref, nu_ref, g_ref, w_ref, rg_ref, o_ref, wbf_ref):
    m = pl.program_id(1)
    prev = te_ref[jnp.maximum(m - 1, 0)]

    @pl.when((m == 0) | (te_ref[m] != prev))
    def _():
        wbf_ref[...] = w_ref[...].astype(BF16)

    @pl.when(m < nu_ref[0])
    def _():
        acc = jnp.dot(g_ref[...], wbf_ref[...], preferred_element_type=F32)
        o_ref[...] = rg_ref[:, 0:1] * acc

    @pl.when(m >= nu_ref[0])
    def _():
        o_ref[...] = jnp.zeros_like(o_ref)


def _moe_down(g, w2, row_gate, tile_expert, n_used):
    n_rows, f = g.shape
    d = w2.shape[2]
    tm, tn = MOE_TM, 512
    grid_spec = pltpu.PrefetchScalarGridSpec(
        num_scalar_prefetch=2, grid=(d // tn, n_rows // tm),
        in_specs=[pl.BlockSpec((tm, f), lambda j, m, te, nu: (m, 0)),
                  pl.BlockSpec((None, f, tn), lambda j, m, te, nu: (te[m], 0, j)),
                  pl.BlockSpec((tm, LANE), lambda j, m, te, nu: (m, 0))],
        out_specs=pl.BlockSpec((tm, tn), lambda j, m, te, nu: (m, j)),
        scratch_shapes=[pltpu.VMEM((f, tn), BF16)])
    return pl.pallas_call(
        _moe_down_kernel, grid_spec=grid_spec,
        out_shape=jax.ShapeDtypeStruct((n_rows, d), F32),
        compiler_params=_cparams(2), name="moe_down")(tile_expert, n_used, g, w2, row_gate)


def _combine_kernel(x_ref, ya_ref, yb_ref, gate_ref, o_ref):
    o_ref[...] = x_ref[...] + gate_ref[...] * (ya_ref[...] + yb_ref[...])


def _moe_combine(x, ya, yb, mods3, layer, n_prompt_rows, dec_seq):
    m_rows, d = x.shape
    tm = 512
    base = layer * MOD_ROWS * 6
    row = pl.BlockSpec((tm, d), lambda m: (m, 0))
    return pl.pallas_call(
        _combine_kernel, grid=(m_rows // tm,),
        in_specs=[row, row, row,
                  pl.BlockSpec((None, 1, d),
                               lambda m: (base + _mod_row(m, tm, n_prompt_rows, dec_seq) * 6 + 5, 0, 0))],
        out_specs=row, out_shape=jax.ShapeDtypeStruct((m_rows, d), F32),
        compiler_params=_cparams(1), name="moe_combine")(x, ya, yb, mods3)


def _take_rows(a, idx):
    return a.at[idx].get(mode="promise_in_bounds")


def _moe_ffn(x, h, route, w1, w3, w2, mods3, layer, n_prompt_rows, dec_seq):
    m_rows, d = x.shape
    tm = MOE_TM
    n_slots = m_rows * TOP_K
    n_rows = n_slots + N_EXPERTS * tm
    n_tiles = n_rows // tm
    top_i = route[:, 0:TOP_K].astype(jnp.int32)
    top_g = route[:, TOP_K:2 * TOP_K]
    flat_e = top_i.reshape(n_slots)
    onehot = (flat_e[:, None] == jnp.arange(N_EXPERTS, dtype=jnp.int32)[None, :]).astype(jnp.int32)
    counts = jnp.sum(onehot, axis=0)
    rank = jnp.sum((jnp.cumsum(onehot, axis=0) - onehot) * onehot, axis=1)
    padded = ((counts + tm - 1) // tm) * tm
    group_end = jnp.cumsum(padded)
    group_start = group_end - padded
    pos = group_start[flat_e] + rank
    src_token = jnp.zeros((n_rows,), jnp.int32).at[pos].set(jnp.arange(n_slots, dtype=jnp.int32) // TOP_K)
    gate_sorted = jnp.zeros((n_rows,), F32).at[pos].set(top_g.reshape(n_slots))
    n_used = (group_end[-1] // tm).astype(jnp.int32)
    tile_start = jnp.minimum(jnp.arange(n_tiles, dtype=jnp.int32), n_used - 1) * tm
    tile_expert = jnp.sum((tile_start[:, None] >= group_end[None, :]).astype(jnp.int32), axis=1)
    tile_expert = jnp.minimum(tile_expert, N_EXPERTS - 1).astype(jnp.int32)
    xs = _take_rows(h, src_token)
    row_gate = jnp.broadcast_to(gate_sorted[:, None], (n_rows, LANE))
    g = _moe_up(xs, w1, w3, tile_expert, n_used.reshape(1))
    ys = _moe_down(g, w2, row_gate, tile_expert, n_used.reshape(1))
    pos2 = pos.reshape(m_rows, TOP_K)
    ya = _take_rows(ys, pos2[:, 0])
    yb = _take_rows(ys, pos2[:, 1])
    return _moe_combine(x, ya, yb, mods3, layer, n_prompt_rows, dec_seq)


def _conv_cols(x_ref, w_ref, b_ref, o_ref, seg, act):
    length, ch = x_ref.shape
    t = lax.broadcasted_iota(jnp.int32, (length, LANE), 0) % seg

    def body(cb, carry):
        c0 = pl.multiple_of(cb * LANE, LANE)
        x = x_ref[:, pl.ds(c0, LANE)]
        w = w_ref[:, pl.ds(c0, LANE)]
        xm1 = jnp.where(t >= 1, pltpu.roll(x, 1, 0), 0.0)
        xp1 = jnp.where(t < seg - 1, pltpu.roll(x, length - 1, 0), 0.0)
        xp2 = jnp.where(t < seg - 2, pltpu.roll(x, length - 2, 0), 0.0)
        y = b_ref[:, pl.ds(c0, LANE)] + (w[0:1] * xm1 + w[1:2] * x + w[2:3] * xp1 + w[3:4] * xp2)
        if act:
            y = _silu(y)
        o_ref[:, pl.ds(c0, LANE)] = y
        return carry

    lax.fori_loop(0, ch // LANE, body, 0)


def _tri_masks(t_len):
    r = lax.broadcasted_iota(jnp.int32, (t_len, t_len), 0)
    c = lax.broadcasted_iota(jnp.int32, (t_len, t_len), 1)
    return r >= c, r <= c


def _mixer_call(body, name, proj, seq_cols, consts, init, st_shape, width, scratch, *, n_seq, length,
                row_off, n_rows_total, y_prev, state_prev, state_dims, **static):
    st_nd = len(st_shape)
    in_specs = [pl.BlockSpec((length, w), lambda b, cb=cb: (b + row_off, cb)) for w, cb in seq_cols]
    args = [proj] * len(seq_cols)
    for c in consts:
        in_specs.append(pl.BlockSpec(c.shape, lambda b, nd=c.ndim: (0,) * nd))
        args.append(c)
    if init is not None:
        in_specs.append(pl.BlockSpec((None,) + st_shape, lambda b: (b,) + (0,) * st_nd))
        args.append(init)
    aliases = {}
    n_alias = 0
    if y_prev is not None:
        aliases[len(args)] = 0
        in_specs.append(pl.BlockSpec(memory_space=pl.ANY))
        args.append(y_prev)
        n_alias += 1
    out_specs = [pl.BlockSpec((length, width), lambda b: (b + row_off, 0))]
    out_shape = [jax.ShapeDtypeStruct((n_rows_total, width), BF16)]
    emit_state = state_dims is not None
    if emit_state:
        layer, depth = state_dims
        if state_prev is not None:
            aliases[len(args)] = 1
            in_specs.append(pl.BlockSpec(memory_space=pl.ANY))
            args.append(state_prev)
            n_alias += 1
        out_specs.append(pl.BlockSpec((None, None) + st_shape, lambda b: (b, layer) + (0,) * st_nd))
        out_shape.append(jax.ShapeDtypeStruct((n_seq, depth) + st_shape, F32))
    kern = functools.partial(body, n_seq_in=len(seq_cols), n_const=len(consts), has_init=init is not None,
                             n_alias=n_alias, emit_state=emit_state, **static)
    return pl.pallas_call(
        kern, grid=(n_seq,), in_specs=in_specs, out_specs=tuple(out_specs), out_shape=tuple(out_shape),
        scratch_shapes=scratch, input_output_aliases=aliases, compiler_params=_cparams(1), name=name)(*args)


def _split_refs(refs, n_seq_in, n_const, has_init, n_alias, emit_state):
    seq = refs[:n_seq_in]
    consts = refs[n_seq_in:n_seq_in + n_const]
    pos = n_seq_in + n_const
    init = refs[pos] if has_init else None
    pos += int(has_init) + n_alias
    y_ref = refs[pos]
    st_out = refs[pos + 1] if emit_state else None
    pos += 1 + int(emit_state)
    return seq, consts, init, y_ref, st_out, refs[pos:]


def _ssd_kernel(*refs, seg, **layout):
    (z_ref, xs_ref, bc_ref, small_ref), consts, h0_ref, y_ref, hout_ref, scratch = _split_refs(refs, **layout)
    wx_ref, bx_ref, wbc_ref, bbc_ref, dtb_ref, nega_ref, dvec_ref, ng_ref = consts
    xc_ref, bcc_ref, yacc_ref, stt_ref = scratch
    length = z_ref.shape[0]
    t_len = SSD_T
    n_chunks = length // t_len
    n_pairs = SSD_H // 2
    _conv_cols(xs_ref, wx_ref, bx_ref, xc_ref, seg, True)
    _conv_cols(bc_ref, wbc_ref, bbc_ref, bcc_ref, seg, True)
    if h0_ref is None:
        stt_ref[...] = jnp.zeros_like(stt_ref)
    else:
        for d in range(2):
            for hp in range(n_pairs):
                pair = jnp.concatenate([h0_ref[d, 2 * hp], h0_ref[d, 2 * hp + 1]], axis=0)
                stt_ref[d, :, hp * LANE:(hp + 1) * LANE] = pair.T
    yacc_ref[...] = xc_ref[...] * dvec_ref[...]
    lower, upper = _tri_masks(t_len)
    tri = (lower.astype(BF16), upper.astype(BF16))
    masks = (lower, upper)
    lane_lo = lax.broadcasted_iota(jnp.int32, (t_len, LANE), 1) < SSD_P
    gn = SSD_N

    def chunk(c, d):
        r0 = pl.multiple_of(c * t_len, t_len)
        dtv = _softplus(small_ref[pl.ds(r0, t_len), :] + dtb_ref[d])
        cs = _exact_lhs_dot(tri[d], nega_ref[d] * dtv)
        cst = cs.T
        dtt = dtv.T
        end = t_len - 1 if d == 0 else 0
        bcv = bcc_ref[pl.ds(r0, t_len), :]
        groups = []
        for g in range(SSD_G):
            bg = bcv[:, g * gn:(g + 1) * gn]
            cg = bcv[:, (SSD_G + g) * gn:(SSD_G + g + 1) * gn]
            groups.append((_bdot_nt(cg, bg), bg.T, cg))
        for hp in range(n_pairs):
            gm, bgt, cg = groups[(2 * hp) // (SSD_H // SSD_G)]
            cols = slice(hp * LANE, (hp + 1) * LANE)
            x = xc_ref[pl.ds(r0, t_len), cols]
            st = stt_ref[d, :, cols]
            x_lo = jnp.where(lane_lo, x, 0.0).astype(BF16)
            x_hi = jnp.where(lane_lo, 0.0, x).astype(BF16)
            s_lo = jnp.where(lane_lo, st, 0.0).astype(BF16)
            s_hi = jnp.where(lane_lo, 0.0, st).astype(BF16)
            intra, carry_in, upd, edec = [], [], [], []
            for h in (2 * hp, 2 * hp + 1):
                colx = jnp.broadcast_to(cs[:, h:h + 1], (t_len, t_len))
                row = cst[h:h + 1, :]
                dtr = dtt[h:h + 1, :]
                cend = cst[h:h + 1, end:end + 1]
                intra.append((jnp.where(masks[d], jnp.exp(colx - row), 0.0) * (gm * dtr)).astype(BF16))
                carry_in.append((cg * jnp.exp(colx)).astype(BF16))
                upd.append((bgt * (jnp.exp(cend - row) * dtr)).astype(BF16))
                edec.append(jnp.exp(cend))
            xblk = jnp.concatenate([x_lo, x_hi], axis=0)
            y = jnp.dot(jnp.concatenate(intra + carry_in, axis=1),
                        jnp.concatenate([xblk, s_lo, s_hi], axis=0), preferred_element_type=F32)
            snew = jnp.dot(jnp.concatenate(upd, axis=1), xblk, preferred_element_type=F32)
            stt_ref[d, :, cols] = st * jnp.where(lane_lo[0:1], edec[0], edec[1]) + snew
            yacc_ref[pl.ds(r0, t_len), cols] += y

    def body(j, carry):
        chunk(j, 0)
        chunk(n_chunks - 1 - j, 1)
        return carry

    lax.fori_loop(0, n_chunks, body, 0)
    if hout_ref is not None:
        for d in range(2):
            for hp in range(n_pairs):
                pair = stt_ref[d, :, hp * LANE:(hp + 1) * LANE].T
                hout_ref[d, 2 * hp] = pair[:SSD_P]
                hout_ref[d, 2 * hp + 1] = pair[SSD_P:]

    def finish(c, carry):
        r0 = pl.multiple_of(c * t_len, t_len)
        y = yacc_ref[pl.ds(r0, t_len), :] * _silu(z_ref[pl.ds(r0, t_len), :])
        y_ref[pl.ds(r0, t_len), :] = (_rms(y) * ng_ref[...]).astype(y_ref.dtype)
        return carry

    lax.fori_loop(0, n_chunks, finish, 0)


def _ssd_call(proj, p, h0, **where):
    length = where['length']
    dtb = jnp.zeros((2, 1, LANE), F32).at[:, 0, :SSD_H].set(p['ssd_dt_bias'].astype(F32))
    nega = jnp.zeros((2, 1, LANE), F32).at[:, 0, :SSD_H].set(-jnp.exp(p['ssd_A_log'].astype(F32)))
    dvec = jnp.repeat(p['ssd_D'].astype(F32), SSD_P).reshape(1, SSD_D)
    cw = p['ssd_conv_w'].astype(F32)
    cb = p['ssd_conv_b'].astype(F32).reshape(1, SSD_CONV_CH)
    consts = [cw[:, :SSD_D], cb[:, :SSD_D], cw[:, SSD_D:], cb[:, SSD_D:], dtb, nega, dvec,
              p['ssd_norm_g'].astype(F32).reshape(1, SSD_D)]
    seq_cols = [(SSD_D, COL_Z // SSD_D), (SSD_D, COL_XS // SSD_D), (512, COL_BC // 512),
                (LANE, COL_SMALL // LANE)]
    scratch = [pltpu.VMEM((length, SSD_D), F32), pltpu.VMEM((length, 512), F32),
               pltpu.VMEM((length, SSD_D), F32), pltpu.VMEM((2, SSD_N, SSD_D), F32)]
    return _mixer_call(_ssd_kernel, "ssd_mixer", proj, seq_cols, consts, h0, (2, SSD_H, SSD_P, SSD_N),
                       SSD_D, scratch, **where)


def _log_sigmoid(x):
    return jnp.minimum(x, 0.0) - jnp.log1p(jnp.exp(-jnp.abs(x)))


def _gla_kernel(*refs, **layout):
    (q_ref, k_ref, v_ref, g_ref, small_ref), consts, s0_ref, y_ref, sout_ref, scratch = _split_refs(refs, **layout)
    gw_ref, gb_ref, ng_ref = consts
    oacc_ref, st_ref = scratch
    length = q_ref.shape[0]
    t_len = GLA_T
    n_chunks = length // t_len
    st_ref[...] = jnp.zeros_like(st_ref)
    if s0_ref is not None:
        for d in range(2):
            for h in range(GLA_H):
                st_ref[d, h * GLA_DV:(h + 1) * GLA_DV, h * GLA_DK:(h + 1) * GLA_DK] = s0_ref[d, h]
    oacc_ref[...] = jnp.zeros_like(oacc_ref)
    lower, upper = _tri_masks(t_len)
    tri = (lower.astype(BF16), upper.astype(BF16))
    masks = tuple(jnp.concatenate([m] * GLA_H, axis=0) for m in (lower, upper))
    q_head = (lax.broadcasted_iota(jnp.int32, (GLA_H * t_len, GLA_KW), 0) // t_len
              == lax.broadcasted_iota(jnp.int32, (GLA_H * t_len, GLA_KW), 1) // GLA_DK)
    st_diag = (lax.broadcasted_iota(jnp.int32, (GLA_VW, GLA_KW), 0) // GLA_DV
               == lax.broadcasted_iota(jnp.int32, (GLA_VW, GLA_KW), 1) // GLA_DK)
    scale = GLA_DK ** -0.5
    gate_w = []
    for d in range(2):
        w1, w2 = _split2(gw_ref[d])
        gate_w.append(jnp.concatenate([w1, w1, w2], axis=0))

    def chunk(c, d):
        r0 = pl.multiple_of(c * t_len, t_len)
        s1, s2 = _split2(small_ref[pl.ds(r0, t_len), :])
        logit = jnp.dot(jnp.concatenate([s1, s2, s1], axis=1), gate_w[d], preferred_element_type=F32) + gb_ref[d]
        la = _log_sigmoid(logit) / GLA_TAU
        bc3 = jnp.dot(tri[d], jnp.concatenate(_split3(la), axis=1), preferred_element_type=F32)
        bc = bc3[:, :GLA_KW] + bc3[:, GLA_KW:2 * GLA_KW] + bc3[:, 2 * GLA_KW:]
        end = t_len - 1 if d == 0 else 0
        mid = bc[t_len // 2:t_len // 2 + 1, :]
        tot = bc[end:end + 1, :]
        q = q_ref[pl.ds(r0, t_len), :] * scale
        k = k_ref[pl.ds(r0, t_len), :]
        v = v_ref[pl.ds(r0, t_len), :].astype(BF16)
        qt = q * jnp.exp(bc - mid)
        kt = k * jnp.exp(mid - bc)
        qs = q * jnp.exp(bc)
        kd = k * jnp.exp(tot - bc)
        q_rows = jnp.where(q_head, jnp.concatenate([qt] * GLA_H, axis=0), 0.0)
        att = jnp.where(masks[d], _bdot_nt(q_rows, kt), 0.0)
        o_all = jnp.dot(att.astype(BF16), v, preferred_element_type=F32)
        o = jnp.concatenate([o_all[h * t_len:(h + 1) * t_len, h * GLA_DV:(h + 1) * GLA_DV]
                             for h in range(GLA_H)], axis=1)
        st = st_ref[d]
        o = o + _bdot_nt(qs, st)
        upd = lax.dot_general(v, kd.astype(BF16), (((0,), (0,)), ((), ())), preferred_element_type=F32)
        st_ref[d] = st * jnp.exp(tot) + jnp.where(st_diag, upd, 0.0)
        oacc_ref[pl.ds(r0, t_len), :] += o

    def body(j, carry):
        chunk(j, 0)
        chunk(n_chunks - 1 - j, 1)
        return carry

    lax.fori_loop(0, n_chunks, body, 0)
    if sout_ref is not None:
        for d in range(2):
            for h in range(GLA_H):
                sout_ref[d, h] = st_ref[d, h * GLA_DV:(h + 1) * GLA_DV, h * GLA_DK:(h + 1) * GLA_DK]

    def finish(c, carry):
        r0 = pl.multiple_of(c * t_len, t_len)
        gate = _silu(g_ref[pl.ds(r0, t_len), :])
        for h in range(GLA_H):
            vs = slice(h * GLA_DV, (h + 1) * GLA_DV)
            o = _rms(oacc_ref[pl.ds(r0, t_len), vs]) * ng_ref[...]
            y_ref[pl.ds(r0, t_len), vs] = (o * gate[:, vs]).astype(y_ref.dtype)
        return carry

    lax.fori_loop(0, n_chunks, finish, 0)


def _gla_call(proj, p, s0t, **where):
    length = where['length']
    gw = jnp.zeros((2, LANE, GLA_KW), F32).at[:, GLA_RANK:2 * GLA_RANK, :].set(p['gla_gate_w'].astype(F32))
    consts = [gw, p['gla_gate_b'].astype(F32).reshape(2, 1, GLA_KW),
              p['gla_norm_g'].astype(F32).reshape(1, GLA_DV)]
    seq_cols = [(GLA_KW, COL_Q // GLA_KW), (GLA_KW, COL_K // GLA_KW), (GLA_VW, COL_V // GLA_VW),
                (GLA_VW, COL_G // GLA_VW), (LANE, COL_SMALL // LANE)]
    st_shape = (2, GLA_H, GLA_DV, GLA_DK)
    scratch = [pltpu.VMEM((length, GLA_VW), F32), pltpu.VMEM((2, GLA_VW, GLA_KW), F32)]
    return _mixer_call(_gla_kernel, "gla_mixer", proj, seq_cols, consts, s0t, st_shape, GLA_VW, scratch, **where)


def _lru_kernel(*refs, seg, **layout):
    (xb_ref, gb_ref), consts, h0_ref, y_ref, hout_ref, scratch = _split_refs(refs, **layout)
    cw_ref, cb_ref, w_ref, bias_ref, sp_ref = consts
    xr_ref, a_ref, u_ref = scratch
    length = xb_ref.shape[0]
    blk = 256
    _conv_cols(xb_ref, cw_ref, cb_ref, xr_ref, seg, False)

    def gates(c, carry):
        r0 = pl.multiple_of(c * blk, blk)
        xr = xr_ref[pl.ds(r0, blk), :]
        xbf = xr.astype(BF16)
        for d in range(2):
            pre = jnp.dot(xbf, w_ref[d], preferred_element_type=F32) + bias_ref[d]
            r = jax.nn.sigmoid(pre[:, :LRU_W])
            i = jax.nn.sigmoid(pre[:, LRU_W:])
            log_a = (-LRU_C) * r * sp_ref[d]
            a = jnp.exp(log_a)
            a_ref[d, pl.ds(r0, blk), :] = a
            u_ref[d, pl.ds(r0, blk), :] = jnp.sqrt(-jnp.tanh(log_a) * (a * a + 1.0)) * (i * xr)
        return carry

    lax.fori_loop(0, length // blk, gates, 0)

    def scan(t, carry):
        hf, hb = carry
        tb = length - 1 - t
        hf = a_ref[0, pl.ds(t, 1), :] * hf + u_ref[0, pl.ds(t, 1), :]
        hb = a_ref[1, pl.ds(tb, 1), :] * hb + u_ref[1, pl.ds(tb, 1), :]
        u_ref[0, pl.ds(t, 1), :] = hf
        u_ref[1, pl.ds(tb, 1), :] = hb
        return hf, hb

    if h0_ref is None:
        start = (jnp.zeros((1, LRU_W), F32), jnp.zeros((1, LRU_W), F32))
    else:
        start = (h0_ref[0], h0_ref[1])
    hf, hb = lax.fori_loop(0, length, scan, start)
    if hout_ref is not None:
        hout_ref[0] = hf
        hout_ref[1] = hb

    def finish(c, carry):
        r0 = pl.multiple_of(c * blk, blk)
        y = (u_ref[0, pl.ds(r0, blk), :] + u_ref[1, pl.ds(r0, blk), :]) * jax.nn.gelu(gb_ref[pl.ds(r0, blk), :])
        y_ref[pl.ds(r0, blk), :] = y.astype(y_ref.dtype)
        return carry

    lax.fori_loop(0, length // blk, finish, 0)


def _block_diag(w):
    nb, bw, _ = w.shape
    eye = jnp.eye(nb, dtype=w.dtype)
    return (eye[:, None, :, None] * w[:, :, None, :]).reshape(nb * bw, nb * bw)


def _lru_call(proj, p, h0, **where):
    length = where['length']
    w = jnp.stack([jnp.concatenate([_block_diag(p['lru_wa'][d].astype(F32)),
                                    _block_diag(p['lru_wx'][d].astype(F32))], axis=1) for d in range(2)])
    bias = jnp.stack([jnp.concatenate([p['lru_ba'][d], p['lru_bx'][d]]) for d in range(2)]).astype(F32)
    sp = jax.nn.softplus(-p['lru_lambda'].astype(F32)).reshape(2, 1, LRU_W)
    consts = [p['lru_conv_w'].astype(F32), p['lru_conv_b'].astype(F32).reshape(1, LRU_W), w.astype(BF16),
              bias.reshape(2, 1, 2 * LRU_W), sp]
    seq_cols = [(LRU_W, COL_XB // LRU_W), (LRU_W, COL_GB // LRU_W)]
    scratch = [pltpu.VMEM((length, LRU_W), F32), pltpu.VMEM((2, length, LRU_W), F32),
               pltpu.VMEM((2, length, LRU_W), F32)]
    return _mixer_call(_lru_kernel, "lru_mixer", proj, seq_cols, consts, h0, (2, 1, LRU_W), LRU_W, scratch,
                       **where)


def _permute_in_w(w):
    d = w.shape[0]
    o_dt = SSD_D + SSD_CONV_CH
    o_q = o_dt + SSD_H
    o_graw = o_q + 2 * GLA_KW + 2 * GLA_VW
    o_xb = o_graw + GLA_RANK
    cols = [w[:, :o_dt], w[:, o_q:o_graw], w[:, o_xb:], w[:, o_dt:o_q], w[:, o_graw:o_xb]]
    used = COL_SMALL + SSD_H + GLA_RANK
    cols.append(jnp.zeros((d, IN_PAD - used), w.dtype))
    return jnp.concatenate(cols, axis=1).astype(BF16)


def kernel(x_prompt, x_sample, state_ssd, state_gla, state_lru, c, c_ctx, mod_w, mod_b, norm1_g, norm2_g, in_w, ssd_conv_w, ssd_conv_b, ssd_A_log, ssd_dt_bias, ssd_D, ssd_norm_g, gla_gate_w, gla_gate_b, gla_norm_g, lru_conv_w, lru_conv_b, lru_wa, lru_ba, lru_wx, lru_bx, lru_lambda, out_w, ffn_w1, ffn_w3, ffn_w2, moe_router, moe_w1, moe_w3, moe_w2, final_norm_g):
    bp, lp, d = x_prompt.shape
    bs, ls, _ = x_sample.shape
    n_p = bp * lp
    n_s = bs * ls
    n_all = n_p + n_s
    depth = in_w.shape[0]
    assert n_p % ls == 0 and 1 + bs <= MOD_ROWS

    cvec = jnp.zeros((MOD_ROWS, d), F32).at[0].set(c_ctx.astype(F32)).at[1:1 + bs].set(c.astype(F32))
    mods = _adaln(cvec, mod_w, mod_b)
    mods3 = mods.reshape(depth * MOD_ROWS * 6, 1, d)
    x = jnp.concatenate([x_prompt.reshape(n_p, d), x_sample.reshape(n_s, d)], axis=0)

    st_ssd = st_gla = st_lru = None
    for i in range(depth):
        p = {'ssd_conv_w': ssd_conv_w[i], 'ssd_conv_b': ssd_conv_b[i], 'ssd_A_log': ssd_A_log[i],
             'ssd_dt_bias': ssd_dt_bias[i], 'ssd_D': ssd_D[i], 'ssd_norm_g': ssd_norm_g[i],
             'gla_gate_w': gla_gate_w[i], 'gla_gate_b': gla_gate_b[i], 'gla_norm_g': gla_norm_g[i],
             'lru_conv_w': lru_conv_w[i], 'lru_conv_b': lru_conv_b[i], 'lru_wa': lru_wa[i],
             'lru_ba': lru_ba[i], 'lru_wx': lru_wx[i], 'lru_bx': lru_bx[i], 'lru_lambda': lru_lambda[i]}
        h = _norm_mod(x, norm1_g[i], mods3, i, 0, n_p, ls)
        proj = _in_proj(h, _permute_in_w(in_w[i]))
        ctx = dict(n_seq=bp, length=lp, row_off=0, n_rows_total=n_all, y_prev=None, state_dims=(i, depth))
        lat = dict(n_seq=bs, length=ls, row_off=n_p // ls, n_rows_total=n_all, state_prev=None, state_dims=None)
        y_ssd, st_ssd = _ssd_call(proj, p, None, seg=lp, state_prev=st_ssd, **ctx)
        y_ssd, = _ssd_call(proj, p, state_ssd[:, i].astype(F32), seg=GRID_W, y_prev=y_ssd, **lat)
        y_gla, st_gla = _gla_call(proj, p, None, state_prev=st_gla, **ctx)
        y_gla, = _gla_call(proj, p, jnp.swapaxes(state_gla[:, i].astype(F32), -1, -2), y_prev=y_gla, **lat)
        y_lru, st_lru = _lru_call(proj, p, None, seg=lp, state_prev=st_lru, **ctx)
        y_lru, = _lru_call(proj, p, state_lru[:, i].astype(F32).reshape(bs, 2, 1, LRU_W), seg=GRID_W,
                           y_prev=y_lru, **lat)
        x = _out_proj(y_ssd, y_gla, y_lru, out_w[i], x, mods3, i, n_p, ls)
        j = i // 2
        if i % 2 == 1:
            h2, route = _norm_mod(x, norm2_g[i], mods3, i, 1, n_p, ls, router=moe_router[j])
            x = _moe_ffn(x, h2, route, moe_w1[j], moe_w3[j], moe_w2[j], mods3, i, n_p, ls)
        else:
            h2 = _norm_mod(x, norm2_g[i], mods3, i, 1, n_p, ls)
            x = _ffn_down(_ffn_up(h2, ffn_w1[j], ffn_w3[j]), ffn_w2[j], x, mods3, i, n_p, ls)
    y_p = _final_norm(x, final_norm_g, 0, n_p)
    y_s = _final_norm(x, final_norm_g, n_p, n_s)
    return (y_p.reshape(bp, lp, d), y_s.reshape(bs, ls, d), st_ssd, jnp.swapaxes(st_gla, -1, -2),
            st_lru.reshape(bp, depth, 2, LRU_W))
```

```python
import functools

import jax
import jax.numpy as jnp
from jax import lax
from jax.experimental import pallas as pl
from jax.experimental.pallas import tpu as pltpu

F32 = jnp.float32
BF16 = jnp.bfloat16

D_MODEL = 2048
GRID_W = 64
SSD_D = D_MODEL // 2
SSD_P = 64
SSD_H = SSD_D // SSD_P
SSD_G = 2
SSD_N = 128
GLA_H = 4
GLA_VW = D_MODEL // 4
GLA_DV = GLA_VW // GLA_H
GLA_DK = GLA_DV // 2
GLA_KW = GLA_H * GLA_DK
GLA_RANK = 16
GLA_TAU = 16.0
LRU_W = D_MODEL // 4
LRU_NB = 8
LRU_BW = LRU_W // LRU_NB
LRU_C = 8.0
SSD_CONV_CH = SSD_D + 2 * SSD_G * SSD_N
N_EXPERTS = 8
TOP_K = 2
EPS = 1e-6

LANE = 128
SSD_T = 128
GLA_T = 64
MIX_SPS = 2
MOD_ROWS = 8
VMEM_LIMIT = 56 * 1024 * 1024

COL_Z = 0
COL_XS = 1024
COL_BC = 2048
COL_Q = 2560
COL_K = 2816
COL_V = 3072
COL_G = 3584
COL_XB = 4096
COL_GB = 4608
COL_SMALL = 5120
IN_PAD = 5632


def _cparams(n_axes):
    return pltpu.CompilerParams(dimension_semantics=("arbitrary",) * n_axes,
                                vmem_limit_bytes=VMEM_LIMIT)


def _bdot(a, b):
    return jnp.dot(a.astype(BF16), b.astype(BF16), preferred_element_type=F32)


def _bdot_nt(a, b):
    return lax.dot_general(a.astype(BF16), b.astype(BF16), (((1,), (1,)), ((), ())),
                           preferred_element_type=F32)


def _bdot_tn(a, b):
    return lax.dot_general(a.astype(BF16), b.astype(BF16), (((0,), (0,)), ((), ())),
                           preferred_element_type=F32)


def _split2(a):
    a1 = a.astype(BF16)
    a2 = (a - a1.astype(F32)).astype(BF16)
    return a1, a2


def _split3(a):
    a1 = a.astype(BF16)
    r = a - a1.astype(F32)
    a2 = r.astype(BF16)
    a3 = (r - a2.astype(F32)).astype(BF16)
    return a1, a2, a3


def _exact_lhs_dot(m_bf16, a):
    a1, a2, a3 = _split3(a)
    f = lambda z: jnp.dot(m_bf16, z, preferred_element_type=F32)
    return f(a1) + f(a2) + f(a3)


def _dot3(a, b):
    a1, a2 = _split2(a)
    b1, b2 = _split2(b)
    f = lambda x, y: jnp.dot(x, y, preferred_element_type=F32)
    return f(a1, b1) + (f(a1, b2) + f(a2, b1))


def _softplus(x):
    return jnp.maximum(x, 0.0) + jnp.log1p(jnp.exp(-jnp.abs(x)))


def _silu(x):
    return x * jax.nn.sigmoid(x)


def _mod_row(m, tm, n_prompt_rows, dec_seq):
    r0 = m * tm
    return jnp.where(r0 < n_prompt_rows, 0, 1 + (r0 - n_prompt_rows) // dec_seq)


def _adaln_kernel(c_ref, w_ref, b_ref, o_ref):
    s = _silu(c_ref[...])
    o_ref[...] = _dot3(s, w_ref[...]) + b_ref[...]


def _adaln(cvec8, mod_w, mod_b):
    depth, d, n = mod_w.shape
    tn = 1024
    return pl.pallas_call(
        _adaln_kernel,
        grid=(depth, n // tn),
        in_specs=[pl.BlockSpec((MOD_ROWS, d), lambda i, j: (0, 0)),
                  pl.BlockSpec((None, d, tn), lambda i, j: (i, 0, j)),
                  pl.BlockSpec((None, 1, tn), lambda i, j: (i, 0, j))],
        out_specs=pl.BlockSpec((None, MOD_ROWS, tn), lambda i, j: (i, 0, j)),
        out_shape=jax.ShapeDtypeStruct((depth, MOD_ROWS, n), F32),
        compiler_params=_cparams(2),
        name="adaln",
    )(cvec8, mod_w, mod_b.reshape(depth, 1, n))


def _rms(x):
    return x * lax.rsqrt(jnp.mean(x * x, axis=-1, keepdims=True) + EPS)


def _norm_mod_kernel(x_ref, g_ref, sh_ref, sc_ref, o_ref):
    y = _rms(x_ref[...]) * g_ref[...]
    o_ref[...] = (y * (1.0 + sc_ref[...]) + sh_ref[...]).astype(o_ref.dtype)


def _norm_mod_route_kernel(x_ref, g_ref, sh_ref, sc_ref, r_ref, o_ref, route_ref):
    y = _rms(x_ref[...]) * g_ref[...]
    h = y * (1.0 + sc_ref[...]) + sh_ref[...]
    o_ref[...] = h.astype(o_ref.dtype)
    logits = _dot3(h, r_ref[...])
    lane = lax.broadcasted_iota(jnp.int32, logits.shape, 1)
    neg = jnp.float32(-jnp.inf)
    lg = jnp.where(lane < N_EXPERTS, logits, neg)
    m1 = jnp.max(lg, axis=-1, keepdims=True)
    i1 = jnp.min(jnp.where(lg == m1, lane, LANE), axis=-1, keepdims=True)
    lg2 = jnp.where(lane == i1, neg, lg)
    m2 = jnp.max(lg2, axis=-1, keepdims=True)
    i2 = jnp.min(jnp.where(lg2 == m2, lane, LANE), axis=-1, keepdims=True)
    e2 = jnp.exp(m2 - m1)
    den = 1.0 + e2
    g1 = 1.0 / den
    g2 = e2 / den
    out = jnp.where(lane == 0, i1.astype(F32),
                    jnp.where(lane == 1, i2.astype(F32),
                              jnp.where(lane == 2, g1, jnp.where(lane == 3, g2, 0.0))))
    route_ref[...] = out


def _norm_mod(x, g, mods3, layer, which, n_prompt_rows, dec_seq, router=None):
    m_rows, d = x.shape
    tm = 512
    base = layer * MOD_ROWS * 6

    def mod_spec(k):
        return pl.BlockSpec((None, 1, d),
                            lambda m: (base + _mod_row(m, tm, n_prompt_rows, dec_seq) * 6 + k, 0, 0))

    in_specs = [pl.BlockSpec((tm, d), lambda m: (m, 0)),
                pl.BlockSpec((1, d), lambda m: (0, 0)),
                mod_spec(3 * which), mod_spec(3 * which + 1)]
    args = [x, g.reshape(1, d), mods3, mods3]
    h_spec = pl.BlockSpec((tm, d), lambda m: (m, 0))
    if router is None:
        return pl.pallas_call(
            _norm_mod_kernel, grid=(m_rows // tm,), in_specs=in_specs, out_specs=h_spec,
            out_shape=jax.ShapeDtypeStruct((m_rows, d), BF16),
            compiler_params=_cparams(1), name="norm_mod")(*args)
    r_pad = jnp.zeros((d, LANE), F32).at[:, :N_EXPERTS].set(router.astype(F32))
    return pl.pallas_call(
        _norm_mod_route_kernel, grid=(m_rows // tm,),
        in_specs=in_specs + [pl.BlockSpec((d, LANE), lambda m: (0, 0))],
        out_specs=(h_spec, pl.BlockSpec((tm, LANE), lambda m: (m, 0))),
        out_shape=(jax.ShapeDtypeStruct((m_rows, d), F32), jax.ShapeDtypeStruct((m_rows, LANE), F32)),
        compiler_params=_cparams(1), name="norm_mod_route")(*args, r_pad)


def _final_norm_kernel(x_ref, g_ref, o_ref):
    o_ref[...] = _rms(x_ref[...]) * g_ref[...]


def _final_norm(x, g, row0, n_rows):
    d = x.shape[1]
    tm = 512
    off = row0 // tm
    return pl.pallas_call(
        _final_norm_kernel, grid=(n_rows // tm,),
        in_specs=[pl.BlockSpec((tm, d), lambda m: (m + off, 0)), pl.BlockSpec((1, d), lambda m: (0, 0))],
        out_specs=pl.BlockSpec((tm, d), lambda m: (m, 0)),
        out_shape=jax.ShapeDtypeStruct((n_rows, d), F32),
        compiler_params=_cparams(1), name="final_norm")(x, g.reshape(1, d))


IN_TN = 512
DT_SRC_BLOCK = (SSD_D + SSD_CONV_CH) // LANE
GRAW_SRC_BLOCK = COL_XB // LANE


def _in_proj_kernel(x_ref, wm_ref, wn_ref, wdt_ref, wgr_ref, o_ref, wbf_ref):
    j = pl.program_id(0)

    @pl.when(pl.program_id(1) == 0)
    def _():
        def shifted(s):
            w = jnp.concatenate([wm_ref[...], wn_ref[...]], axis=1)
            wbf_ref[...] = pltpu.roll(w, IN_TN + LANE - s, 1)[:, :IN_TN].astype(BF16)

        @pl.when(j < COL_Q // IN_TN)
        def _():
            wbf_ref[...] = wm_ref[...].astype(BF16)

        @pl.when((j >= COL_Q // IN_TN) & (j < COL_XB // IN_TN))
        def _():
            shifted(SSD_H)

        @pl.when((j >= COL_XB // IN_TN) & (j < COL_SMALL // IN_TN))
        def _():
            shifted(SSD_H + GLA_RANK)

        @pl.when(j == COL_SMALL // IN_TN)
        def _():
            lane = lax.broadcasted_iota(jnp.int32, wdt_ref.shape, 1)
            small = jnp.where(lane < SSD_H, wdt_ref[...], jnp.where(lane < SSD_H + GLA_RANK, wgr_ref[...], 0.0))
            wbf_ref[...] = jnp.zeros_like(wbf_ref)
            wbf_ref[:, :LANE] = small.astype(BF16)

    o_ref[...] = jnp.dot(x_ref[...], wbf_ref[...], preferred_element_type=F32)


def _in_proj(h, in_w, layer):
    m_rows, k = h.shape
    tm, tn = 2048, IN_TN
    n_main = COL_SMALL // tn
    last_lane_block = (in_w.shape[2] - 1) // LANE
    return pl.pallas_call(
        _in_proj_kernel, grid=(IN_PAD // tn, m_rows // tm),
        in_specs=[pl.BlockSpec((tm, k), lambda j, m: (m, 0)),
                  pl.BlockSpec((None, k, tn), lambda j, m: (layer, 0, jnp.minimum(j, n_main - 1))),
                  pl.BlockSpec((None, k, LANE),
                               lambda j, m: (layer, 0, jnp.minimum((j + 1) * (tn // LANE), last_lane_block))),
                  pl.BlockSpec((None, k, LANE), lambda j, m: (layer, 0, DT_SRC_BLOCK)),
                  pl.BlockSpec((None, k, LANE), lambda j, m: (layer, 0, GRAW_SRC_BLOCK))],
        out_specs=pl.BlockSpec((tm, tn), lambda j, m: (m, j)),
        out_shape=jax.ShapeDtypeStruct((m_rows, IN_PAD), F32),
        scratch_shapes=[pltpu.VMEM((k, tn), BF16)],
        compiler_params=_cparams(2), name="in_proj")(h, in_w, in_w, in_w, in_w)


def _out_proj_kernel(y1_ref, y2_ref, y3_ref, w_ref, res_ref, gate_ref, o_ref, wbf_ref):
    @pl.when(pl.program_id(1) == 0)
    def _():
        wbf_ref[...] = w_ref[...].astype(BF16)

    k1 = y1_ref.shape[1]
    k2 = k1 + y2_ref.shape[1]
    acc = jnp.dot(y1_ref[...], wbf_ref[0:k1, :], preferred_element_type=F32)
    acc += jnp.dot(y2_ref[...], wbf_ref[k1:k2, :], preferred_element_type=F32)
    acc += jnp.dot(y3_ref[...], wbf_ref[k2:, :], preferred_element_type=F32)
    o_ref[...] = res_ref[...] + gate_ref[...] * acc


def _out_proj(y1, y2, y3, w, res, mods3, layer, n_prompt_rows, dec_seq):
    m_rows, d = res.shape
    k = w.shape[0]
    tm, tn = 1024, 1024
    base = layer * MOD_ROWS * 6
    return pl.pallas_call(
        _out_proj_kernel, grid=(d // tn, m_rows // tm),
        in_specs=[pl.BlockSpec((tm, y1.shape[1]), lambda j, m: (m, 0)),
                  pl.BlockSpec((tm, y2.shape[1]), lambda j, m: (m, 0)),
                  pl.BlockSpec((tm, y3.shape[1]), lambda j, m: (m, 0)),
                  pl.BlockSpec((k, tn), lambda j, m: (0, j)),
                  pl.BlockSpec((tm, tn), lambda j, m: (m, j)),
                  pl.BlockSpec((None, 1, tn),
                               lambda j, m: (base + _mod_row(m, tm, n_prompt_rows, dec_seq) * 6 + 2, 0, j))],
        out_specs=pl.BlockSpec((tm, tn), lambda j, m: (m, j)),
        out_shape=jax.ShapeDtypeStruct((m_rows, d), F32),
        scratch_shapes=[pltpu.VMEM((k, tn), BF16)],
        compiler_params=_cparams(2), name="out_proj")(y1, y2, y3, w, res, mods3)


def _ffn_up_kernel(x_ref, w1_ref, w3_ref, o_ref, w1bf_ref, w3bf_ref):
    @pl.when(pl.program_id(1) == 0)
    def _():
        w1bf_ref[...] = w1_ref[...].astype(BF16)
        w3bf_ref[...] = w3_ref[...].astype(BF16)

    a = jnp.dot(x_ref[...], w1bf_ref[...], preferred_element_type=F32)
    b = jnp.dot(x_ref[...], w3bf_ref[...], preferred_element_type=F32)
    o_ref[...] = (_silu(a) * b).astype(o_ref.dtype)


def _ffn_up(h, w1, w3):
    m_rows, d = h.shape
    f = w1.shape[1]
    tm, tf = 1024, 512
    return pl.pallas_call(
        _ffn_up_kernel, grid=(pl.cdiv(f, tf), m_rows // tm),
        in_specs=[pl.BlockSpec((tm, d), lambda j, m: (m, 0)),
                  pl.BlockSpec((d, tf), lambda j, m: (0, j)),
                  pl.BlockSpec((d, tf), lambda j, m: (0, j))],
        out_specs=pl.BlockSpec((tm, tf), lambda j, m: (m, j)),
        out_shape=jax.ShapeDtypeStruct((m_rows, f), BF16),
        scratch_shapes=[pltpu.VMEM((d, tf), BF16), pltpu.VMEM((d, tf), BF16)],
        compiler_params=_cparams(2), name="ffn_up")(h, w1, w3)


def _ffn_down_kernel(g_ref, w_ref, res_ref, gate_ref, o_ref, wbf_ref):
    @pl.when(pl.program_id(1) == 0)
    def _():
        wbf_ref[...] = w_ref[...].astype(BF16)

    acc = jnp.dot(g_ref[...], wbf_ref[...], preferred_element_type=F32)
    o_ref[...] = res_ref[...] + gate_ref[...] * acc


def _ffn_down(g, w2, res, mods3, layer, n_prompt_rows, dec_seq):
    m_rows, d = res.shape
    f = w2.shape[0]
    tm, tn = 512, 512
    base = layer * MOD_ROWS * 6
    return pl.pallas_call(
        _ffn_down_kernel, grid=(d // tn, m_rows // tm),
        in_specs=[pl.BlockSpec((tm, f), lambda j, m: (m, 0)),
                  pl.BlockSpec((f, tn), lambda j, m: (0, j)),
                  pl.BlockSpec((tm, tn), lambda j, m: (m, j)),
                  pl.BlockSpec((None, 1, tn),
                               lambda j, m: (base + _mod_row(m, tm, n_prompt_rows, dec_seq) * 6 + 5, 0, j))],
        out_specs=pl.BlockSpec((tm, tn), lambda j, m: (m, j)),
        out_shape=jax.ShapeDtypeStruct((m_rows, d), F32),
        scratch_shapes=[pltpu.VMEM((f, tn), BF16)],
        compiler_params=_cparams(2), name="ffn_down")(g, w2, res, mods3)


MOE_TM = 256


def _grouped_weights(plan_refs, w_hbm, stages, casts, sems):
    te_ref, first_ref, next_e_ref, last_ref = plan_refs
    j = pl.program_id(0)
    m = pl.program_id(1)
    n_col_tiles = pl.num_programs(0)

    def fetch(e, jj, k):
        width = stages[k].shape[1]
        src = w_hbm[k].at[e, :, pl.ds(pl.multiple_of(jj * width, width), width)]
        return pltpu.make_async_copy(src, stages[k], sems.at[k])

    @pl.when((j == 0) & (m == 0))
    def _():
        for k in range(len(stages)):
            fetch(te_ref[0], 0, k).start()

    @pl.when(first_ref[m] == 1)
    def _():
        for k in range(len(stages)):
            fetch(te_ref[m], j, k).wait()
            casts[k][...] = stages[k][...].astype(BF16)
        next_j = j + last_ref[m]

        @pl.when(next_j < n_col_tiles)
        def _():
            for k in range(len(stages)):
                fetch(next_e_ref[m], next_j, k).start()


def _moe_up_kernel(te_ref, first_ref, next_e_ref, last_ref, nu_ref, x_ref, w1_hbm, w3_hbm, o_ref,
                   w1st_ref, w3st_ref, w1bf_ref, w3bf_ref, sems):
    m = pl.program_id(1)
    _grouped_weights((te_ref, first_ref, next_e_ref, last_ref), (w1_hbm, w3_hbm), (w1st_ref, w3st_ref),
                     (w1bf_ref, w3bf_ref), sems)

    @pl.when(m < nu_ref[0])
    def _():
        x = x_ref[...].astype(BF16)
        a = jnp.dot(x, w1bf_ref[...], preferred_element_type=F32)
        b = jnp.dot(x, w3bf_ref[...], preferred_element_type=F32)
        o_ref[...] = (_silu(a) * b).astype(o_ref.dtype)

    @pl.when(m >= nu_ref[0])
    def _():
        o_ref[...] = jnp.zeros_like(o_ref)


def _moe_up(xs, w1, w3, plan):
    n_rows, d = xs.shape
    f = w1.shape[2]
    tm, tf = MOE_TM, 1024
    n_plan = len(plan)
    grid_spec = pltpu.PrefetchScalarGridSpec(
        num_scalar_prefetch=n_plan, grid=(f // tf, n_rows // tm),
        in_specs=[pl.BlockSpec((tm, d), lambda j, m, *_: (m, 0)),
                  pl.BlockSpec(memory_space=pl.ANY), pl.BlockSpec(memory_space=pl.ANY)],
        out_specs=pl.BlockSpec((tm, tf), lambda j, m, *_: (m, j)),
        scratch_shapes=[pltpu.VMEM((d, tf), F32), pltpu.VMEM((d, tf), F32),
                        pltpu.VMEM((d, tf), BF16), pltpu.VMEM((d, tf), BF16),
                        pltpu.SemaphoreType.DMA((2,))])
    return pl.pallas_call(
        _moe_up_kernel, grid_spec=grid_spec,
        out_shape=jax.ShapeDtypeStruct((n_rows, f), BF16),
        compiler_params=_cparams(2), name="moe_up")(*plan, xs, w1, w3)


def _moe_down_kernel(te_ref, first_ref, next_e_ref, last_ref, nu_ref, g_ref, w_hbm, o_ref,
                     wst_ref, wbf_ref, sems):
    m = pl.program_id(1)
    _grouped_weights((te_ref, first_ref, next_e_ref, last_ref), (w_hbm,), (wst_ref,), (wbf_ref,), sems)

    @pl.when(m < nu_ref[0])
    def _():
        o_ref[...] = jnp.dot(g_ref[...], wbf_ref[...], preferred_element_type=F32)

    @pl.when(m >= nu_ref[0])
    def _():
        o_ref[...] = jnp.zeros_like(o_ref)


def _moe_down(g, w2, plan):
    n_rows, f = g.shape
    d = w2.shape[2]
    tm, tn = MOE_TM, 512
    grid_spec = pltpu.PrefetchScalarGridSpec(
        num_scalar_prefetch=len(plan), grid=(d // tn, n_rows // tm),
        in_specs=[pl.BlockSpec((tm, f), lambda j, m, *_: (m, 0)),
                  pl.BlockSpec(memory_space=pl.ANY)],
        out_specs=pl.BlockSpec((tm, tn), lambda j, m, *_: (m, j)),
        scratch_shapes=[pltpu.VMEM((f, tn), F32), pltpu.VMEM((f, tn), BF16),
                        pltpu.SemaphoreType.DMA((1,))])
    return pl.pallas_call(
        _moe_down_kernel, grid_spec=grid_spec,
        out_shape=jax.ShapeDtypeStruct((n_rows, d), F32),
        compiler_params=_cparams(2), name="moe_down")(*plan, g, w2)


def _combine_kernel(x_ref, ya_ref, yb_ref, route_ref, gate_ref, o_ref):
    ga = route_ref[:, TOP_K:TOP_K + 1]
    gb = route_ref[:, TOP_K + 1:TOP_K + 2]
    o_ref[...] = x_ref[...] + gate_ref[...] * (ga * ya_ref[...] + gb * yb_ref[...])


def _moe_combine(x, ya, yb, route, mods3, layer, n_prompt_rows, dec_seq):
    m_rows, d = x.shape
    tm = 512
    base = layer * MOD_ROWS * 6
    row = pl.BlockSpec((tm, d), lambda m: (m, 0))
    return pl.pallas_call(
        _combine_kernel, grid=(m_rows // tm,),
        in_specs=[row, row, row, pl.BlockSpec((tm, LANE), lambda m: (m, 0)),
                  pl.BlockSpec((None, 1, d),
                               lambda m: (base + _mod_row(m, tm, n_prompt_rows, dec_seq) * 6 + 5, 0, 0))],
        out_specs=row, out_shape=jax.ShapeDtypeStruct((m_rows, d), F32),
        compiler_params=_cparams(1), name="moe_combine")(x, ya, yb, route, mods3)


def _take_rows(a, idx):
    return a.at[idx].get(mode="promise_in_bounds")


def _moe_ffn(x, h, route, w1, w3, w2, mods3, layer, n_prompt_rows, dec_seq):
    m_rows, d = x.shape
    tm = MOE_TM
    n_slots = m_rows * TOP_K
    n_rows = n_slots + N_EXPERTS * tm
    n_tiles = n_rows // tm
    top_i = route[:, 0:TOP_K].astype(jnp.int32)
    flat_e = top_i.reshape(n_slots)
    onehot = (flat_e[:, None] == jnp.arange(N_EXPERTS, dtype=jnp.int32)[None, :]).astype(jnp.int32)
    counts = jnp.sum(onehot, axis=0)
    rank = jnp.sum((jnp.cumsum(onehot, axis=0) - onehot) * onehot, axis=1)
    padded = ((counts + tm - 1) // tm) * tm
    group_end = jnp.cumsum(padded)
    group_start = group_end - padded
    pos = group_start[flat_e] + rank
    src_token = jnp.zeros((n_rows,), jnp.int32).at[pos].set(
        jnp.arange(n_slots, dtype=jnp.int32) // TOP_K, unique_indices=True, mode="promise_in_bounds")
    n_used = (group_end[-1] // tm).astype(jnp.int32)
    tiles = jnp.arange(n_tiles, dtype=jnp.int32)
    tile_start = jnp.minimum(tiles, n_used - 1) * tm
    tile_expert = jnp.sum((tile_start[:, None] >= group_end[None, :]).astype(jnp.int32), axis=1)
    tile_expert = jnp.minimum(tile_expert, N_EXPERTS - 1).astype(jnp.int32)
    prev_expert = jnp.concatenate([jnp.full((1,), -1, jnp.int32), tile_expert[:-1]])
    first = ((tile_expert != prev_expert) & (tiles < n_used)).astype(jnp.int32)
    ids = jnp.arange(N_EXPERTS, dtype=jnp.int32)
    later = jnp.where((ids[None, :] > ids[:, None]) & (counts[None, :] > 0), ids[None, :], N_EXPERTS)
    next_expert = jnp.min(later, axis=1)
    is_last = next_expert == N_EXPERTS
    next_expert = jnp.where(is_last, tile_expert[0], next_expert)
    plan = (tile_expert, first, next_expert[tile_expert], is_last.astype(jnp.int32)[tile_expert],
            n_used.reshape(1))
    xs = _take_rows(h, src_token)
    g = _moe_up(xs, w1, w3, plan)
    ys = _moe_down(g, w2, plan)
    pos2 = pos.reshape(m_rows, TOP_K)
    ya = _take_rows(ys, pos2[:, 0])
    yb = _take_rows(ys, pos2[:, 1])
    return _moe_combine(x, ya, yb, route, mods3, layer, n_prompt_rows, dec_seq)


def _conv_cols(x_ref, w_ref, b_ref, o_ref, seg, act):
    length, ch = x_ref.shape
    t = lax.broadcasted_iota(jnp.int32, (length, LANE), 0) % seg

    def body(cb, carry):
        c0 = pl.multiple_of(cb * LANE, LANE)
        x = x_ref[:, pl.ds(c0, LANE)]
        w = w_ref[:, pl.ds(c0, LANE)]
        xm1 = jnp.where(t >= 1, pltpu.roll(x, 1, 0), 0.0)
        xp1 = jnp.where(t < seg - 1, pltpu.roll(x, length - 1, 0), 0.0)
        xp2 = jnp.where(t < seg - 2, pltpu.roll(x, length - 2, 0), 0.0)
        y = b_ref[:, pl.ds(c0, LANE)] + (w[0:1] * xm1 + w[1:2] * x + w[2:3] * xp1 + w[3:4] * xp2)
        if act:
            y = _silu(y)
        o_ref[:, pl.ds(c0, LANE)] = y
        return carry

    lax.fori_loop(0, ch // LANE, body, 0)


def _tri_masks(t_len):
    r = lax.broadcasted_iota(jnp.int32, (t_len, t_len), 0)
    c = lax.broadcasted_iota(jnp.int32, (t_len, t_len), 1)
    return r >= c, r <= c


def _mixer_call(body, name, proj, seq_cols, consts, init, st_shape, width, scratch, *, n_seq, length,
                row_off, n_rows_total, y_prev, state_prev, state_dims, **static):
    st_nd = len(st_shape)
    sps = static.get('sps')
    seq_dim = None if sps is None else sps
    sps = 1 if sps is None else sps
    assert n_seq % sps == 0 and row_off % sps == 0
    rows = sps * length
    row_off = row_off // sps
    in_specs = [pl.BlockSpec((rows, w), lambda b, cb=cb: (b + row_off, cb)) for w, cb in seq_cols]
    args = [proj] * len(seq_cols)
    for c in consts:
        in_specs.append(pl.BlockSpec(c.shape, lambda b, nd=c.ndim: (0,) * nd))
        args.append(c)
    if init is not None:
        in_specs.append(pl.BlockSpec((seq_dim,) + st_shape, lambda b: (b,) + (0,) * st_nd))
        args.append(init)
    aliases = {}
    n_alias = 0
    if y_prev is not None:
        aliases[len(args)] = 0
        in_specs.append(pl.BlockSpec(memory_space=pl.ANY))
        args.append(y_prev)
        n_alias += 1
    out_specs = [pl.BlockSpec((rows, width), lambda b: (b + row_off, 0))]
    out_shape = [jax.ShapeDtypeStruct((n_rows_total, width), BF16)]
    emit_state = state_dims is not None
    if emit_state:
        layer, depth = state_dims
        if state_prev is not None:
            aliases[len(args)] = 1
            in_specs.append(pl.BlockSpec(memory_space=pl.ANY))
            args.append(state_prev)
            n_alias += 1
        out_specs.append(pl.BlockSpec((seq_dim, None) + st_shape, lambda b: (b, layer) + (0,) * st_nd))
        out_shape.append(jax.ShapeDtypeStruct((n_seq, depth) + st_shape, F32))
    kern = functools.partial(body, n_seq_in=len(seq_cols), n_const=len(consts), has_init=init is not None,
                             n_alias=n_alias, emit_state=emit_state, **static)
    return pl.pallas_call(
        kern, grid=(n_seq // sps,), in_specs=in_specs, out_specs=tuple(out_specs), out_shape=tuple(out_shape),
        scratch_shapes=scratch, input_output_aliases=aliases, compiler_params=_cparams(1), name=name)(*args)


def _split_refs(refs, n_seq_in, n_const, has_init, n_alias, emit_state):
    seq = refs[:n_seq_in]
    consts = refs[n_seq_in:n_seq_in + n_const]
    pos = n_seq_in + n_const
    init = refs[pos] if has_init else None
    pos += int(has_init) + n_alias
    y_ref = refs[pos]
    st_out = refs[pos + 1] if emit_state else None
    pos += 1 + int(emit_state)
    return seq, consts, init, y_ref, st_out, refs[pos:]


def _ssd_kernel(*refs, seg, **layout):
    (z_ref, xs_ref, bc_ref, small_ref), consts, h0_ref, y_ref, hout_ref, scratch = _split_refs(refs, **layout)
    wx_ref, bx_ref, wbc_ref, bbc_ref, dtb_ref, nega_ref, dvec_ref, ng_ref = consts
    xc_ref, bcc_ref, yacc_ref, stt_ref = scratch
    length = z_ref.shape[0]
    t_len = SSD_T
    n_chunks = length // t_len
    n_pairs = SSD_H // 2
    _conv_cols(xs_ref, wx_ref, bx_ref, xc_ref, seg, True)
    _conv_cols(bc_ref, wbc_ref, bbc_ref, bcc_ref, seg, True)
    if h0_ref is None:
        stt_ref[...] = jnp.zeros_like(stt_ref)
    else:
        for d in range(2):
            for hp in range(n_pairs):
                pair = jnp.concatenate([h0_ref[d, 2 * hp], h0_ref[d, 2 * hp + 1]], axis=0)
                stt_ref[d, :, hp * LANE:(hp + 1) * LANE] = pair.T
    yacc_ref[...] = xc_ref[...] * dvec_ref[...]
    lower, upper = _tri_masks(t_len)
    tri = (lower.astype(BF16), upper.astype(BF16))
    masks = (lower, upper)
    lane_lo = lax.broadcasted_iota(jnp.int32, (t_len, LANE), 1) < SSD_P
    gn = SSD_N

    def chunk(c, d):
        r0 = pl.multiple_of(c * t_len, t_len)
        dtv = _softplus(small_ref[pl.ds(r0, t_len), :] + dtb_ref[d])
        cs = _exact_lhs_dot(tri[d], nega_ref[d] * dtv)
        cst = cs.T
        dtt = dtv.T
        end = t_len - 1 if d == 0 else 0
        bcv = bcc_ref[pl.ds(r0, t_len), :]
        groups = []
        for g in range(SSD_G):
            bg = bcv[:, g * gn:(g + 1) * gn]
            cg = bcv[:, (SSD_G + g) * gn:(SSD_G + g + 1) * gn]
            groups.append((_bdot_nt(cg, bg), bg.T, cg))
        for hp in range(n_pairs):
            gm, bgt, cg = groups[(2 * hp) // (SSD_H // SSD_G)]
            cols = slice(hp * LANE, (hp + 1) * LANE)
            x = xc_ref[pl.ds(r0, t_len), cols]
            st = stt_ref[d, :, cols]
            x_lo = jnp.where(lane_lo, x, 0.0).astype(BF16)
            x_hi = jnp.where(lane_lo, 0.0, x).astype(BF16)
            s_lo = jnp.where(lane_lo, st, 0.0).astype(BF16)
            s_hi = jnp.where(lane_lo, 0.0, st).astype(BF16)
            intra, carry_in, upd, edec = [], [], [], []
            for h in (2 * hp, 2 * hp + 1):
                colx = jnp.broadcast_to(cs[:, h:h + 1], (t_len, t_len))
                row = cst[h:h + 1, :]
                dtr = dtt[h:h + 1, :]
                cend = cst[h:h + 1, end:end + 1]
                intra.append((jnp.where(masks[d], jnp.exp(colx - row), 0.0) * (gm * dtr)).astype(BF16))
                carry_in.append((cg * jnp.exp(colx)).astype(BF16))
                upd.append((bgt * (jnp.exp(cend - row) * dtr)).astype(BF16))
                edec.append(jnp.exp(cend))
            xblk = jnp.concatenate([x_lo, x_hi], axis=0)
            y = jnp.dot(jnp.concatenate(intra + carry_in, axis=1),
                        jnp.concatenate([xblk, s_lo, s_hi], axis=0), preferred_element_type=F32)
            snew = jnp.dot(jnp.concatenate(upd, axis=1), xblk, preferred_element_type=F32)
            stt_ref[d, :, cols] = st * jnp.where(lane_lo[0:1], edec[0], edec[1]) + snew
            yacc_ref[pl.ds(r0, t_len), cols] += y

    def body(j, carry):
        chunk(j, 0)
        chunk(n_chunks - 1 - j, 1)
        return carry

    lax.fori_loop(0, n_chunks, body, 0)
    if hout_ref is not None:
        for d in range(2):
            for hp in range(n_pairs):
                pair = stt_ref[d, :, hp * LANE:(hp + 1) * LANE].T
                hout_ref[d, 2 * hp] = pair[:SSD_P]
                hout_ref[d, 2 * hp + 1] = pair[SSD_P:]

    def finish(c, carry):
        r0 = pl.multiple_of(c * t_len, t_len)
        y = yacc_ref[pl.ds(r0, t_len), :] * _silu(z_ref[pl.ds(r0, t_len), :])
        y_ref[pl.ds(r0, t_len), :] = (_rms(y) * ng_ref[...]).astype(y_ref.dtype)
        return carry

    lax.fori_loop(0, n_chunks, finish, 0)


def _ssd_call(proj, p, h0, **where):
    length = where['length']
    dtb = jnp.zeros((2, 1, LANE), F32).at[:, 0, :SSD_H].set(p['ssd_dt_bias'].astype(F32))
    nega = jnp.zeros((2, 1, LANE), F32).at[:, 0, :SSD_H].set(-jnp.exp(p['ssd_A_log'].astype(F32)))
    dvec = jnp.repeat(p['ssd_D'].astype(F32), SSD_P).reshape(1, SSD_D)
    cw = p['ssd_conv_w'].astype(F32)
    cb = p['ssd_conv_b'].astype(F32).reshape(1, SSD_CONV_CH)
    consts = [cw[:, :SSD_D], cb[:, :SSD_D], cw[:, SSD_D:], cb[:, SSD_D:], dtb, nega, dvec,
              p['ssd_norm_g'].astype(F32).reshape(1, SSD_D)]
    seq_cols = [(SSD_D, COL_Z // SSD_D), (SSD_D, COL_XS // SSD_D), (512, COL_BC // 512),
                (LANE, COL_SMALL // LANE)]
    scratch = [pltpu.VMEM((length, SSD_D), F32), pltpu.VMEM((length, 512), F32),
               pltpu.VMEM((length, SSD_D), F32), pltpu.VMEM((2, SSD_N, SSD_D), F32)]
    return _mixer_call(_ssd_kernel, "ssd_mixer", proj, seq_cols, consts, h0, (2, SSD_H, SSD_P, SSD_N),
                       SSD_D, scratch, **where)


def _log_sigmoid(x):
    return jnp.minimum(x, 0.0) - jnp.log1p(jnp.exp(-jnp.abs(x)))


def _gla_kernel(*refs, sps, **layout):
    (q_ref, k_ref, v_ref, g_ref, small_ref), consts, s0_ref, y_ref, sout_ref, scratch = _split_refs(refs, **layout)
    gw_ref, gb_ref, ng_ref = consts
    oacc_ref, st_ref = scratch
    length = q_ref.shape[0] // sps
    t_len = GLA_T
    n_chunks = length // t_len
    st_ref[...] = jnp.zeros_like(st_ref)
    if s0_ref is not None:
        for s in range(sps):
            for d in range(2):
                for h in range(GLA_H):
                    st_ref[s, d, h * GLA_DV:(h + 1) * GLA_DV, h * GLA_DK:(h + 1) * GLA_DK] = s0_ref[s, d, h]
    oacc_ref[...] = jnp.zeros_like(oacc_ref)
    lower, upper = _tri_masks(t_len)
    tri = (lower.astype(BF16), upper.astype(BF16))
    masks = tuple(jnp.concatenate([m] * GLA_H, axis=0) for m in (lower, upper))
    q_head = (lax.broadcasted_iota(jnp.int32, (GLA_H * t_len, GLA_KW), 0) // t_len
              == lax.broadcasted_iota(jnp.int32, (GLA_H * t_len, GLA_KW), 1) // GLA_DK)
    st_diag = (lax.broadcasted_iota(jnp.int32, (GLA_VW, GLA_KW), 0) // GLA_DV
               == lax.broadcasted_iota(jnp.int32, (GLA_VW, GLA_KW), 1) // GLA_DK)
    scale = GLA_DK ** -0.5
    gate_w = []
    for d in range(2):
        w1, w2 = _split2(gw_ref[d])
        gate_w.append(jnp.concatenate([w1, w1, w2], axis=0))

    def chunk(c, d, s):
        r0 = pl.multiple_of(s * length + c * t_len, t_len)
        s1, s2 = _split2(small_ref[pl.ds(r0, t_len), :])
        logit = jnp.dot(jnp.concatenate([s1, s2, s1], axis=1), gate_w[d], preferred_element_type=F32) + gb_ref[d]
        la = _log_sigmoid(logit) / GLA_TAU
        bc3 = jnp.dot(tri[d], jnp.concatenate(_split3(la), axis=1), preferred_element_type=F32)
        bc = bc3[:, :GLA_KW] + bc3[:, GLA_KW:2 * GLA_KW] + bc3[:, 2 * GLA_KW:]
        end = t_len - 1 if d == 0 else 0
        mid = bc[t_len // 2:t_len // 2 + 1, :]
        tot = bc[end:end + 1, :]
        q = q_ref[pl.ds(r0, t_len), :] * scale
        k = k_ref[pl.ds(r0, t_len), :]
        v = v_ref[pl.ds(r0, t_len), :].astype(BF16)
        qt = q * jnp.exp(bc - mid)
        kt = k * jnp.exp(mid - bc)
        qs = q * jnp.exp(bc)
        kd = k * jnp.exp(tot - bc)
        q_rows = jnp.where(q_head, jnp.concatenate([qt] * GLA_H, axis=0), 0.0)
        att = jnp.where(masks[d], _bdot_nt(q_rows, kt), 0.0)
        o_all = jnp.dot(att.astype(BF16), v, preferred_element_type=F32)
        o = jnp.concatenate([o_all[h * t_len:(h + 1) * t_len, h * GLA_DV:(h + 1) * GLA_DV]
                             for h in range(GLA_H)], axis=1)
        st = st_ref[s, d]
        o = o + _bdot_nt(qs, st)
        upd = lax.dot_general(v, kd.astype(BF16), (((0,), (0,)), ((), ())), preferred_element_type=F32)
        st_ref[s, d] = st * jnp.exp(tot) + jnp.where(st_diag, upd, 0.0)
        oacc_ref[pl.ds(r0, t_len), :] += o

    def body(j, carry):
        for s in range(sps):
            chunk(j, 0, s)
            chunk(n_chunks - 1 - j, 1, s)
        return carry

    lax.fori_loop(0, n_chunks, body, 0)
    if sout_ref is not None:
        for s in range(sps):
            for d in range(2):
                for h in range(GLA_H):
                    sout_ref[s, d, h] = st_ref[s, d, h * GLA_DV:(h + 1) * GLA_DV, h * GLA_DK:(h + 1) * GLA_DK]

    def finish(c, carry):
        r0 = pl.multiple_of(c * t_len, t_len)
        gate = _silu(g_ref[pl.ds(r0, t_len), :])
        for h in range(GLA_H):
            vs = slice(h * GLA_DV, (h + 1) * GLA_DV)
            o = _rms(oacc_ref[pl.ds(r0, t_len), vs]) * ng_ref[...]
            y_ref[pl.ds(r0, t_len), vs] = (o * gate[:, vs]).astype(y_ref.dtype)
        return carry

    lax.fori_loop(0, sps * n_chunks, finish, 0)


def _gla_call(proj, p, s0t, **where):
    length = where['length']
    sps = where['sps']
    gw = jnp.zeros((2, LANE, GLA_KW), F32).at[:, GLA_RANK:2 * GLA_RANK, :].set(p['gla_gate_w'].astype(F32))
    consts = [gw, p['gla_gate_b'].astype(F32).reshape(2, 1, GLA_KW),
              p['gla_norm_g'].astype(F32).reshape(1, GLA_DV)]
    seq_cols = [(GLA_KW, COL_Q // GLA_KW), (GLA_KW, COL_K // GLA_KW), (GLA_VW, COL_V // GLA_VW),
                (GLA_VW, COL_G // GLA_VW), (LANE, COL_SMALL // LANE)]
    st_shape = (2, GLA_H, GLA_DV, GLA_DK)
    scratch = [pltpu.VMEM((sps * length, GLA_VW), F32), pltpu.VMEM((sps, 2, GLA_VW, GLA_KW), F32)]
    return _mixer_call(_gla_kernel, "gla_mixer", proj, seq_cols, consts, s0t, st_shape, GLA_VW, scratch, **where)


def _lru_kernel(*refs, seg, sps, **layout):
    (xb_ref, gb_ref), consts, h0_ref, y_ref, hout_ref, scratch = _split_refs(refs, **layout)
    cw_ref, cb_ref, w_ref, bias_ref, sp_ref = consts
    xr_ref, a_ref, u_ref = scratch
    n_rows = xb_ref.shape[0]
    length = n_rows // sps
    blk = 256
    _conv_cols(xb_ref, cw_ref, cb_ref, xr_ref, seg, False)

    def gates(c, carry):
        r0 = pl.multiple_of(c * blk, blk)
        xr = xr_ref[pl.ds(r0, blk), :]
        xbf = xr.astype(BF16)
        for d in range(2):
            pre = jnp.dot(xbf, w_ref[d], preferred_element_type=F32) + bias_ref[d]
            r = jax.nn.sigmoid(pre[:, :LRU_W])
            i = jax.nn.sigmoid(pre[:, LRU_W:])
            log_a = (-LRU_C) * r * sp_ref[d]
            a = jnp.exp(log_a)
            a_ref[d, pl.ds(r0, blk), :] = a
            u_ref[d, pl.ds(r0, blk), :] = jnp.sqrt(-jnp.tanh(log_a) * (a * a + 1.0)) * (i * xr)
        return carry

    lax.fori_loop(0, n_rows // blk, gates, 0)

    def scan(t, carry):
        out = []
        for s in range(sps):
            for d in range(2):
                row = s * length + (t if d == 0 else length - 1 - t)
                h = a_ref[d, pl.ds(row, 1), :] * carry[2 * s + d] + u_ref[d, pl.ds(row, 1), :]
                u_ref[d, pl.ds(row, 1), :] = h
                out.append(h)
        return tuple(out)

    if h0_ref is None:
        start = tuple(jnp.zeros((1, LRU_W), F32) for _ in range(2 * sps))
    else:
        start = tuple(h0_ref[s, d] for s in range(sps) for d in range(2))
    final = lax.fori_loop(0, length, scan, start)
    if hout_ref is not None:
        for s in range(sps):
            for d in range(2):
                hout_ref[s, d] = final[2 * s + d]

    def finish(c, carry):
        r0 = pl.multiple_of(c * blk, blk)
        y = (u_ref[0, pl.ds(r0, blk), :] + u_ref[1, pl.ds(r0, blk), :]) * jax.nn.gelu(gb_ref[pl.ds(r0, blk), :])
        y_ref[pl.ds(r0, blk), :] = y.astype(y_ref.dtype)
        return carry

    lax.fori_loop(0, n_rows // blk, finish, 0)


def _block_diag(w):
    nb, bw, _ = w.shape
    eye = jnp.eye(nb, dtype=w.dtype)
    return (eye[:, None, :, None] * w[:, :, None, :]).reshape(nb * bw, nb * bw)


def _lru_call(proj, p, h0, **where):
    length = where['length']
    w = jnp.stack([jnp.concatenate([_block_diag(p['lru_wa'][d].astype(F32)),
                                    _block_diag(p['lru_wx'][d].astype(F32))], axis=1) for d in range(2)])
    bias = jnp.stack([jnp.concatenate([p['lru_ba'][d], p['lru_bx'][d]]) for d in range(2)]).astype(F32)
    sp = jax.nn.softplus(-p['lru_lambda'].astype(F32)).reshape(2, 1, LRU_W)
    consts = [p['lru_conv_w'].astype(F32), p['lru_conv_b'].astype(F32).reshape(1, LRU_W), w.astype(BF16),
              bias.reshape(2, 1, 2 * LRU_W), sp]
    seq_cols = [(LRU_W, COL_XB // LRU_W), (LRU_W, COL_GB // LRU_W)]
    rows = where['sps'] * length
    scratch = [pltpu.VMEM((rows, LRU_W), F32), pltpu.VMEM((2, rows, LRU_W), F32),
               pltpu.VMEM((2, rows, LRU_W), F32)]
    return _mixer_call(_lru_kernel, "lru_mixer", proj, seq_cols, consts, h0, (2, 1, LRU_W), LRU_W, scratch,
                       **where)


def kernel(x_prompt, x_sample, state_ssd, state_gla, state_lru, c, c_ctx, mod_w, mod_b, norm1_g, norm2_g, in_w, ssd_conv_w, ssd_conv_b, ssd_A_log, ssd_dt_bias, ssd_D, ssd_norm_g, gla_gate_w, gla_gate_b, gla_norm_g, lru_conv_w, lru_conv_b, lru_wa, lru_ba, lru_wx, lru_bx, lru_lambda, out_w, ffn_w1, ffn_w3, ffn_w2, moe_router, moe_w1, moe_w3, moe_w2, final_norm_g):
    bp, lp, d = x_prompt.shape
    bs, ls, _ = x_sample.shape
    n_p = bp * lp
    n_s = bs * ls
    n_all = n_p + n_s
    depth = in_w.shape[0]
    assert n_p % ls == 0 and 1 + bs <= MOD_ROWS

    cvec = jnp.zeros((MOD_ROWS, d), F32).at[0].set(c_ctx.astype(F32)).at[1:1 + bs].set(c.astype(F32))
    mods = _adaln(cvec, mod_w, mod_b)
    mods3 = mods.reshape(depth * MOD_ROWS * 6, 1, d)
    x = jnp.concatenate([x_prompt.reshape(n_p, d), x_sample.reshape(n_s, d)], axis=0)

    st_ssd = st_gla = st_lru = None
    for i in range(depth):
        p = {'ssd_conv_w': ssd_conv_w[i], 'ssd_conv_b': ssd_conv_b[i], 'ssd_A_log': ssd_A_log[i],
             'ssd_dt_bias': ssd_dt_bias[i], 'ssd_D': ssd_D[i], 'ssd_norm_g': ssd_norm_g[i],
             'gla_gate_w': gla_gate_w[i], 'gla_gate_b': gla_gate_b[i], 'gla_norm_g': gla_norm_g[i],
             'lru_conv_w': lru_conv_w[i], 'lru_conv_b': lru_conv_b[i], 'lru_wa': lru_wa[i],
             'lru_ba': lru_ba[i], 'lru_wx': lru_wx[i], 'lru_bx': lru_bx[i], 'lru_lambda': lru_lambda[i]}
        h = _norm_mod(x, norm1_g[i], mods3, i, 0, n_p, ls)
        proj = _in_proj(h, in_w, i)
        ctx = dict(n_seq=bp, length=lp, row_off=0, n_rows_total=n_all, y_prev=None, state_dims=(i, depth))
        lat = dict(n_seq=bs, length=ls, row_off=n_p // ls, n_rows_total=n_all, state_prev=None, state_dims=None)
        y_ssd, st_ssd = _ssd_call(proj, p, None, seg=lp, state_prev=st_ssd, **ctx)
        y_ssd, = _ssd_call(proj, p, state_ssd[:, i].astype(F32), seg=GRID_W, y_prev=y_ssd, **lat)
        y_gla, st_gla = _gla_call(proj, p, None, sps=MIX_SPS, state_prev=st_gla, **ctx)
        y_gla, = _gla_call(proj, p, jnp.swapaxes(state_gla[:, i].astype(F32), -1, -2), sps=MIX_SPS,
                           y_prev=y_gla, **lat)
        y_lru, st_lru = _lru_call(proj, p, None, seg=lp, sps=MIX_SPS, state_prev=st_lru, **ctx)
        y_lru, = _lru_call(proj, p, state_lru[:, i].astype(F32).reshape(bs, 2, 1, LRU_W), seg=GRID_W,
                           sps=MIX_SPS, y_prev=y_lru, **lat)
        x = _out_proj(y_ssd, y_gla, y_lru, out_w[i], x, mods3, i, n_p, ls)
        j = i // 2
        if i % 2 == 1:
            h2, route = _norm_mod(x, norm2_g[i], mods3, i, 1, n_p, ls, router=moe_router[j])
            x = _moe_ffn(x, h2, route, moe_w1[j], moe_w3[j], moe_w2[j], mods3, i, n_p, ls)
        else:
            h2 = _norm_mod(x, norm2_g[i], mods3, i, 1, n_p, ls)
            x = _ffn_down(_ffn_up(h2, ffn_w1[j], ffn_w3[j]), ffn_w2[j], x, mods3, i, n_p, ls)
    y_p = _final_norm(x, final_norm_g, 0, n_p)
    y_s = _final_norm(x, final_norm_g, n_p, n_s)
    return (y_p.reshape(bp, lp, d), y_s.reshape(bs, ls, d), st_ssd, jnp.swapaxes(st_gla, -1, -2),
            st_lru.reshape(bp, depth, 2, LRU_W))
```

```python
import functools

import jax
import jax.numpy as jnp
from jax import lax
from jax.experimental import pallas as pl
from jax.experimental.pallas import tpu as pltpu

F32 = jnp.float32
BF16 = jnp.bfloat16

D_MODEL = 2048
GRID_W = 64
SSD_D = D_MODEL // 2
SSD_P = 64
SSD_H = SSD_D // SSD_P
SSD_G = 2
SSD_N = 128
GLA_H = 4
GLA_VW = D_MODEL // 4
GLA_DV = GLA_VW // GLA_H
GLA_DK = GLA_DV // 2
GLA_KW = GLA_H * GLA_DK
GLA_RANK = 16
GLA_TAU = 16.0
LRU_W = D_MODEL // 4
LRU_NB = 8
LRU_BW = LRU_W // LRU_NB
LRU_C = 8.0
SSD_CONV_CH = SSD_D + 2 * SSD_G * SSD_N
N_EXPERTS = 8
TOP_K = 2
EPS = 1e-6

LANE = 128
SSD_T = 128
GLA_T = 64
MIX_SPS = 2
MOD_ROWS = 8
VMEM_LIMIT = 56 * 1024 * 1024

COL_Z = 0
COL_XS = 1024
COL_BC = 2048
COL_Q = 2560
COL_K = 2816
COL_V = 3072
COL_G = 3584
COL_XB = 4096
COL_GB = 4608
COL_SMALL = 5120
IN_PAD = 5632


def _cparams(n_axes):
    return pltpu.CompilerParams(dimension_semantics=("arbitrary",) * n_axes,
                                vmem_limit_bytes=VMEM_LIMIT)


def _bdot(a, b):
    return jnp.dot(a.astype(BF16), b.astype(BF16), preferred_element_type=F32)


def _bdot_nt(a, b):
    return lax.dot_general(a.astype(BF16), b.astype(BF16), (((1,), (1,)), ((), ())),
                           preferred_element_type=F32)


def _bdot_tn(a, b):
    return lax.dot_general(a.astype(BF16), b.astype(BF16), (((0,), (0,)), ((), ())),
                           preferred_element_type=F32)


def _split2(a):
    a1 = a.astype(BF16)
    a2 = (a - a1.astype(F32)).astype(BF16)
    return a1, a2


def _split3(a):
    a1 = a.astype(BF16)
    r = a - a1.astype(F32)
    a2 = r.astype(BF16)
    a3 = (r - a2.astype(F32)).astype(BF16)
    return a1, a2, a3


def _exact_lhs_dot(m_bf16, a):
    a1, a2, a3 = _split3(a)
    f = lambda z: jnp.dot(m_bf16, z, preferred_element_type=F32)
    return f(a1) + f(a2) + f(a3)


def _dot3(a, b):
    a1, a2 = _split2(a)
    b1, b2 = _split2(b)
    f = lambda x, y: jnp.dot(x, y, preferred_element_type=F32)
    return f(a1, b1) + (f(a1, b2) + f(a2, b1))


def _softplus(x):
    return jnp.maximum(x, 0.0) + jnp.log1p(jnp.exp(-jnp.abs(x)))


def _silu(x):
    return x * jax.nn.sigmoid(x)


def _mod_row(m, tm, n_prompt_rows, dec_seq):
    r0 = m * tm
    return jnp.where(r0 < n_prompt_rows, 0, 1 + (r0 - n_prompt_rows) // dec_seq)


def _adaln_kernel(c_ref, w_ref, b_ref, o_ref):
    s = _silu(c_ref[...])
    o_ref[...] = _dot3(s, w_ref[...]) + b_ref[...]


def _adaln(cvec8, mod_w, mod_b):
    depth, d, n = mod_w.shape
    tn = 1024
    return pl.pallas_call(
        _adaln_kernel,
        grid=(depth, n // tn),
        in_specs=[pl.BlockSpec((MOD_ROWS, d), lambda i, j: (0, 0)),
                  pl.BlockSpec((None, d, tn), lambda i, j: (i, 0, j)),
                  pl.BlockSpec((None, 1, tn), lambda i, j: (i, 0, j))],
        out_specs=pl.BlockSpec((None, MOD_ROWS, tn), lambda i, j: (i, 0, j)),
        out_shape=jax.ShapeDtypeStruct((depth, MOD_ROWS, n), F32),
        compiler_params=_cparams(2),
        name="adaln",
    )(cvec8, mod_w, mod_b.reshape(depth, 1, n))


def _rms(x):
    return x * lax.rsqrt(jnp.mean(x * x, axis=-1, keepdims=True) + EPS)


def _norm_mod_kernel(x_ref, g_ref, sh_ref, sc_ref, o_ref):
    y = _rms(x_ref[...]) * g_ref[...]
    o_ref[...] = (y * (1.0 + sc_ref[...]) + sh_ref[...]).astype(o_ref.dtype)


def _norm_mod_route_kernel(x_ref, g_ref, sh_ref, sc_ref, r_ref, o_ref, route_ref):
    y = _rms(x_ref[...]) * g_ref[...]
    h = y * (1.0 + sc_ref[...]) + sh_ref[...]
    o_ref[...] = h.astype(o_ref.dtype)
    logits = _dot3(h, r_ref[...])
    lane = lax.broadcasted_iota(jnp.int32, logits.shape, 1)
    neg = jnp.float32(-jnp.inf)
    lg = jnp.where(lane < N_EXPERTS, logits, neg)
    m1 = jnp.max(lg, axis=-1, keepdims=True)
    i1 = jnp.min(jnp.where(lg == m1, lane, LANE), axis=-1, keepdims=True)
    lg2 = jnp.where(lane == i1, neg, lg)
    m2 = jnp.max(lg2, axis=-1, keepdims=True)
    i2 = jnp.min(jnp.where(lg2 == m2, lane, LANE), axis=-1, keepdims=True)
    e2 = jnp.exp(m2 - m1)
    den = 1.0 + e2
    g1 = 1.0 / den
    g2 = e2 / den
    out = jnp.where(lane == 0, i1.astype(F32),
                    jnp.where(lane == 1, i2.astype(F32),
                              jnp.where(lane == 2, g1, jnp.where(lane == 3, g2, 0.0))))
    route_ref[...] = out


def _norm_mod(x, g, mods3, layer, which, n_prompt_rows, dec_seq, router=None):
    m_rows, d = x.shape
    tm = 512
    base = layer * MOD_ROWS * 6

    def mod_spec(k):
        return pl.BlockSpec((None, 1, d),
                            lambda m: (base + _mod_row(m, tm, n_prompt_rows, dec_seq) * 6 + k, 0, 0))

    in_specs = [pl.BlockSpec((tm, d), lambda m: (m, 0)),
                pl.BlockSpec((1, d), lambda m: (0, 0)),
                mod_spec(3 * which), mod_spec(3 * which + 1)]
    args = [x, g.reshape(1, d), mods3, mods3]
    h_spec = pl.BlockSpec((tm, d), lambda m: (m, 0))
    if router is None:
        return pl.pallas_call(
            _norm_mod_kernel, grid=(m_rows // tm,), in_specs=in_specs, out_specs=h_spec,
            out_shape=jax.ShapeDtypeStruct((m_rows, d), BF16),
            compiler_params=_cparams(1), name="norm_mod")(*args)
    r_pad = jnp.zeros((d, LANE), F32).at[:, :N_EXPERTS].set(router.astype(F32))
    return pl.pallas_call(
        _norm_mod_route_kernel, grid=(m_rows // tm,),
        in_specs=in_specs + [pl.BlockSpec((d, LANE), lambda m: (0, 0))],
        out_specs=(h_spec, pl.BlockSpec((tm, LANE), lambda m: (m, 0))),
        out_shape=(jax.ShapeDtypeStruct((m_rows, d), F32), jax.ShapeDtypeStruct((m_rows, LANE), F32)),
        compiler_params=_cparams(1), name="norm_mod_route")(*args, r_pad)


def _final_norm_kernel(x_ref, g_ref, o_ref):
    o_ref[...] = _rms(x_ref[...]) * g_ref[...]


def _final_norm(x, g, row0, n_rows):
    d = x.shape[1]
    tm = 512
    off = row0 // tm
    return pl.pallas_call(
        _final_norm_kernel, grid=(n_rows // tm,),
        in_specs=[pl.BlockSpec((tm, d), lambda m: (m + off, 0)), pl.BlockSpec((1, d), lambda m: (0, 0))],
        out_specs=pl.BlockSpec((tm, d), lambda m: (m, 0)),
        out_shape=jax.ShapeDtypeStruct((n_rows, d), F32),
        compiler_params=_cparams(1), name="final_norm")(x, g.reshape(1, d))


IN_TN = 512
IN_DT_COL = SSD_D + SSD_CONV_CH
IN_GRAW_COL = IN_DT_COL + SSD_H + 2 * GLA_KW + 2 * GLA_VW


def _in_proj_kernel(x_ref, wm_ref, wn_ref, wdt_ref, wgr_ref, o_ref, wbf_ref):
    j = pl.program_id(0)

    @pl.when(pl.program_id(1) == 0)
    def _():
        def shifted(s):
            w = jnp.concatenate([wm_ref[...], wn_ref[...]], axis=0)
            wbf_ref[...] = w[s:s + IN_TN].astype(BF16)

        @pl.when(j < COL_Q // IN_TN)
        def _():
            wbf_ref[...] = wm_ref[...].astype(BF16)

        @pl.when((j >= COL_Q // IN_TN) & (j < COL_XB // IN_TN))
        def _():
            shifted(SSD_H)

        @pl.when((j >= COL_XB // IN_TN) & (j < COL_SMALL // IN_TN))
        def _():
            shifted(SSD_H + GLA_RANK)

        @pl.when(j == COL_SMALL // IN_TN)
        def _():
            wbf_ref[...] = jnp.zeros_like(wbf_ref)
            wbf_ref[0:SSD_H] = wdt_ref[...].astype(BF16)
            wbf_ref[SSD_H:SSD_H + GLA_RANK] = wgr_ref[...].astype(BF16)

    o_ref[...] = lax.dot_general(x_ref[...], wbf_ref[...], (((1,), (1,)), ((), ())),
                                 preferred_element_type=F32)


def _in_proj(h, in_w_t, layer):
    m_rows, k = h.shape
    tm, tn = 2048, IN_TN
    n_main = COL_SMALL // tn
    shift_max = SSD_H + GLA_RANK
    last_next_block = in_w_t.shape[1] // shift_max - 1
    return pl.pallas_call(
        _in_proj_kernel, grid=(IN_PAD // tn, m_rows // tm),
        in_specs=[pl.BlockSpec((tm, k), lambda j, m: (m, 0)),
                  pl.BlockSpec((None, tn, k), lambda j, m: (layer, jnp.minimum(j, n_main - 1), 0)),
                  pl.BlockSpec((None, shift_max, k),
                               lambda j, m: (layer, jnp.minimum((j + 1) * (tn // shift_max), last_next_block), 0)),
                  pl.BlockSpec((None, SSD_H, k), lambda j, m: (layer, IN_DT_COL // SSD_H, 0)),
                  pl.BlockSpec((None, GLA_RANK, k), lambda j, m: (layer, IN_GRAW_COL // GLA_RANK, 0))],
        out_specs=pl.BlockSpec((tm, tn), lambda j, m: (m, j)),
        out_shape=jax.ShapeDtypeStruct((m_rows, IN_PAD), F32),
        scratch_shapes=[pltpu.VMEM((tn, k), BF16)],
        compiler_params=_cparams(2), name="in_proj")(h, in_w_t, in_w_t, in_w_t, in_w_t)


def _out_proj_kernel(y1_ref, y2_ref, y3_ref, w_ref, res_ref, gate_ref, o_ref, wbf_ref):
    @pl.when(pl.program_id(1) == 0)
    def _():
        wbf_ref[...] = w_ref[...].astype(BF16)

    k1 = y1_ref.shape[1]
    k2 = k1 + y2_ref.shape[1]
    acc = jnp.dot(y1_ref[...], wbf_ref[0:k1, :], preferred_element_type=F32)
    acc += jnp.dot(y2_ref[...], wbf_ref[k1:k2, :], preferred_element_type=F32)
    acc += jnp.dot(y3_ref[...], wbf_ref[k2:, :], preferred_element_type=F32)
    o_ref[...] = res_ref[...] + gate_ref[...] * acc


def _out_proj(y1, y2, y3, w, res, mods3, layer, n_prompt_rows, dec_seq):
    m_rows, d = res.shape
    k = w.shape[1]
    tm, tn = 1024, 1024
    base = layer * MOD_ROWS * 6
    return pl.pallas_call(
        _out_proj_kernel, grid=(d // tn, m_rows // tm),
        in_specs=[pl.BlockSpec((tm, y1.shape[1]), lambda j, m: (m, 0)),
                  pl.BlockSpec((tm, y2.shape[1]), lambda j, m: (m, 0)),
                  pl.BlockSpec((tm, y3.shape[1]), lambda j, m: (m, 0)),
                  pl.BlockSpec((None, k, tn), lambda j, m: (layer, 0, j)),
                  pl.BlockSpec((tm, tn), lambda j, m: (m, j)),
                  pl.BlockSpec((None, 1, tn),
                               lambda j, m: (base + _mod_row(m, tm, n_prompt_rows, dec_seq) * 6 + 2, 0, j))],
        out_specs=pl.BlockSpec((tm, tn), lambda j, m: (m, j)),
        out_shape=jax.ShapeDtypeStruct((m_rows, d), F32),
        scratch_shapes=[pltpu.VMEM((k, tn), BF16)],
        compiler_params=_cparams(2), name="out_proj")(y1, y2, y3, w, res, mods3)


def _ffn_up_kernel(x_ref, w1_ref, w3_ref, o_ref, w1bf_ref, w3bf_ref):
    @pl.when(pl.program_id(1) == 0)
    def _():
        w1bf_ref[...] = w1_ref[...].astype(BF16)
        w3bf_ref[...] = w3_ref[...].astype(BF16)

    a = jnp.dot(x_ref[...], w1bf_ref[...], preferred_element_type=F32)
    b = jnp.dot(x_ref[...], w3bf_ref[...], preferred_element_type=F32)
    o_ref[...] = (_silu(a) * b).astype(o_ref.dtype)


def _ffn_up(h, w1, w3):
    m_rows, d = h.shape
    f = w1.shape[1]
    tm, tf = 1024, 512
    return pl.pallas_call(
        _ffn_up_kernel, grid=(pl.cdiv(f, tf), m_rows // tm),
        in_specs=[pl.BlockSpec((tm, d), lambda j, m: (m, 0)),
                  pl.BlockSpec((d, tf), lambda j, m: (0, j)),
                  pl.BlockSpec((d, tf), lambda j, m: (0, j))],
        out_specs=pl.BlockSpec((tm, tf), lambda j, m: (m, j)),
        out_shape=jax.ShapeDtypeStruct((m_rows, f), BF16),
        scratch_shapes=[pltpu.VMEM((d, tf), BF16), pltpu.VMEM((d, tf), BF16)],
        compiler_params=_cparams(2), name="ffn_up")(h, w1, w3)


def _ffn_down_kernel(g_ref, w_ref, res_ref, gate_ref, o_ref, wbf_ref):
    @pl.when(pl.program_id(1) == 0)
    def _():
        wbf_ref[...] = w_ref[...].astype(BF16)

    acc = jnp.dot(g_ref[...], wbf_ref[...], preferred_element_type=F32)
    o_ref[...] = res_ref[...] + gate_ref[...] * acc


def _ffn_down(g, w2, res, mods3, layer, n_prompt_rows, dec_seq):
    m_rows, d = res.shape
    f = w2.shape[0]
    tm, tn = 512, 512
    base = layer * MOD_ROWS * 6
    return pl.pallas_call(
        _ffn_down_kernel, grid=(d // tn, m_rows // tm),
        in_specs=[pl.BlockSpec((tm, f), lambda j, m: (m, 0)),
                  pl.BlockSpec((f, tn), lambda j, m: (0, j)),
                  pl.BlockSpec((tm, tn), lambda j, m: (m, j)),
                  pl.BlockSpec((None, 1, tn),
                               lambda j, m: (base + _mod_row(m, tm, n_prompt_rows, dec_seq) * 6 + 5, 0, j))],
        out_specs=pl.BlockSpec((tm, tn), lambda j, m: (m, j)),
        out_shape=jax.ShapeDtypeStruct((m_rows, d), F32),
        scratch_shapes=[pltpu.VMEM((f, tn), BF16)],
        compiler_params=_cparams(2), name="ffn_down")(g, w2, res, mods3)


MOE_TM = 256


def _grouped_weights(plan_refs, w_hbm, stages, casts, sems):
    te_ref, first_ref, next_e_ref, last_ref = plan_refs
    j = pl.program_id(0)
    m = pl.program_id(1)
    n_col_tiles = pl.num_programs(0)

    def fetch(e, jj, k):
        width = stages[k].shape[1]
        src = w_hbm[k].at[e, :, pl.ds(pl.multiple_of(jj * width, width), width)]
        return pltpu.make_async_copy(src, stages[k], sems.at[k])

    @pl.when((j == 0) & (m == 0))
    def _():
        for k in range(len(stages)):
            fetch(te_ref[0], 0, k).start()

    @pl.when(first_ref[m] == 1)
    def _():
        for k in range(len(stages)):
            fetch(te_ref[m], j, k).wait()
            casts[k][...] = stages[k][...].astype(BF16)
        next_j = j + last_ref[m]

        @pl.when(next_j < n_col_tiles)
        def _():
            for k in range(len(stages)):
                fetch(next_e_ref[m], next_j, k).start()


def _moe_up_kernel(te_ref, first_ref, next_e_ref, last_ref, nu_ref, x_ref, w1_hbm, w3_hbm, o_ref,
                   w1st_ref, w3st_ref, w1bf_ref, w3bf_ref, sems):
    m = pl.program_id(1)
    _grouped_weights((te_ref, first_ref, next_e_ref, last_ref), (w1_hbm, w3_hbm), (w1st_ref, w3st_ref),
                     (w1bf_ref, w3bf_ref), sems)

    @pl.when(m < nu_ref[0])
    def _():
        x = x_ref[...].astype(BF16)
        a = jnp.dot(x, w1bf_ref[...], preferred_element_type=F32)
        b = jnp.dot(x, w3bf_ref[...], preferred_element_type=F32)
        o_ref[...] = (_silu(a) * b).astype(o_ref.dtype)

    @pl.when(m >= nu_ref[0])
    def _():
        o_ref[...] = jnp.zeros_like(o_ref)


def _moe_up(xs, w1, w3, plan):
    n_rows, d = xs.shape
    f = w1.shape[2]
    tm, tf = MOE_TM, 1792
    n_plan = len(plan)
    grid_spec = pltpu.PrefetchScalarGridSpec(
        num_scalar_prefetch=n_plan, grid=(f // tf, n_rows // tm),
        in_specs=[pl.BlockSpec((tm, d), lambda j, m, *_: (m, 0)),
                  pl.BlockSpec(memory_space=pl.ANY), pl.BlockSpec(memory_space=pl.ANY)],
        out_specs=pl.BlockSpec((tm, tf), lambda j, m, *_: (m, j)),
        scratch_shapes=[pltpu.VMEM((d, tf), F32), pltpu.VMEM((d, tf), F32),
                        pltpu.VMEM((d, tf), BF16), pltpu.VMEM((d, tf), BF16),
                        pltpu.SemaphoreType.DMA((2,))])
    return pl.pallas_call(
        _moe_up_kernel, grid_spec=grid_spec,
        out_shape=jax.ShapeDtypeStruct((n_rows, f), BF16),
        compiler_params=_cparams(2), name="moe_up")(*plan, xs, w1, w3)


def _moe_down_kernel(te_ref, first_ref, next_e_ref, last_ref, nu_ref, g_ref, w_hbm, o_ref,
                     wst_ref, wbf_ref, sems):
    m = pl.program_id(1)
    _grouped_weights((te_ref, first_ref, next_e_ref, last_ref), (w_hbm,), (wst_ref,), (wbf_ref,), sems)

    @pl.when(m < nu_ref[0])
    def _():
        o_ref[...] = jnp.dot(g_ref[...], wbf_ref[...], preferred_element_type=F32)

    @pl.when(m >= nu_ref[0])
    def _():
        o_ref[...] = jnp.zeros_like(o_ref)


def _moe_down(g, w2, plan):
    n_rows, f = g.shape
    d = w2.shape[2]
    tm, tn = MOE_TM, 1024
    grid_spec = pltpu.PrefetchScalarGridSpec(
        num_scalar_prefetch=len(plan), grid=(d // tn, n_rows // tm),
        in_specs=[pl.BlockSpec((tm, f), lambda j, m, *_: (m, 0)),
                  pl.BlockSpec(memory_space=pl.ANY)],
        out_specs=pl.BlockSpec((tm, tn), lambda j, m, *_: (m, j)),
        scratch_shapes=[pltpu.VMEM((f, tn), F32), pltpu.VMEM((f, tn), BF16),
                        pltpu.SemaphoreType.DMA((1,))])
    return pl.pallas_call(
        _moe_down_kernel, grid_spec=grid_spec,
        out_shape=jax.ShapeDtypeStruct((n_rows, d), F32),
        compiler_params=_cparams(2), name="moe_down")(*plan, g, w2)


def _combine_kernel(x_ref, ya_ref, yb_ref, route_ref, gate_ref, *rest):
    ga = route_ref[:, TOP_K:TOP_K + 1]
    gb = route_ref[:, TOP_K + 1:TOP_K + 2]
    v = x_ref[...] + gate_ref[...] * (ga * ya_ref[...] + gb * yb_ref[...])
    if len(rest) == 2:
        g_ref, o_ref = rest
        o_ref[...] = _rms(v) * g_ref[...]
    else:
        rest[0][...] = v


def _moe_combine(x, ya, yb, route, mods3, layer, n_prompt_rows, dec_seq, row0, n_rows, final_g):
    d = x.shape[1]
    tm = 512
    off = row0 // tm
    base = layer * MOD_ROWS * 6
    row = pl.BlockSpec((tm, d), lambda m: (m + off, 0))
    in_specs = [row, row, row, pl.BlockSpec((tm, LANE), lambda m: (m + off, 0)),
                pl.BlockSpec((None, 1, d),
                             lambda m: (base + _mod_row(m + off, tm, n_prompt_rows, dec_seq) * 6 + 5, 0, 0))]
    args = [x, ya, yb, route, mods3]
    if final_g is not None:
        in_specs.append(pl.BlockSpec((1, d), lambda m: (0, 0)))
        args.append(final_g.reshape(1, d))
    return pl.pallas_call(
        _combine_kernel, grid=(n_rows // tm,), in_specs=in_specs,
        out_specs=pl.BlockSpec((tm, d), lambda m: (m, 0)),
        out_shape=jax.ShapeDtypeStruct((n_rows, d), F32),
        compiler_params=_cparams(1), name="moe_combine")(*args)


def _take_rows(a, idx):
    return a.at[idx].get(mode="promise_in_bounds")


def _moe_ffn(x, h, route, w1, w3, w2, mods3, layer, n_prompt_rows, dec_seq, final_g=None):
    m_rows, d = x.shape
    tm = MOE_TM
    n_slots = m_rows * TOP_K
    n_rows = n_slots + N_EXPERTS * tm
    n_tiles = n_rows // tm
    top_i = route[:, 0:TOP_K].astype(jnp.int32)
    flat_e = top_i.reshape(n_slots)
    onehot = (flat_e[:, None] == jnp.arange(N_EXPERTS, dtype=jnp.int32)[None, :]).astype(jnp.int32)
    counts = jnp.sum(onehot, axis=0)
    rank = jnp.sum((jnp.cumsum(onehot, axis=0) - onehot) * onehot, axis=1)
    padded = ((counts + tm - 1) // tm) * tm
    group_end = jnp.cumsum(padded)
    group_start = group_end - padded
    pos = group_start[flat_e] + rank
    src_token = jnp.zeros((n_rows,), jnp.int32).at[pos].set(
        jnp.arange(n_slots, dtype=jnp.int32) // TOP_K, unique_indices=True, mode="promise_in_bounds")
    n_used = (group_end[-1] // tm).astype(jnp.int32)
    tiles = jnp.arange(n_tiles, dtype=jnp.int32)
    tile_start = jnp.minimum(tiles, n_used - 1) * tm
    tile_expert = jnp.sum((tile_start[:, None] >= group_end[None, :]).astype(jnp.int32), axis=1)
    tile_expert = jnp.minimum(tile_expert, N_EXPERTS - 1).astype(jnp.int32)
    prev_expert = jnp.concatenate([jnp.full((1,), -1, jnp.int32), tile_expert[:-1]])
    first = ((tile_expert != prev_expert) & (tiles < n_used)).astype(jnp.int32)
    ids = jnp.arange(N_EXPERTS, dtype=jnp.int32)
    later = jnp.where((ids[None, :] > ids[:, None]) & (counts[None, :] > 0), ids[None, :], N_EXPERTS)
    next_expert = jnp.min(later, axis=1)
    is_last = next_expert == N_EXPERTS
    next_expert = jnp.where(is_last, tile_expert[0], next_expert)
    plan = (tile_expert, first, next_expert[tile_expert], is_last.astype(jnp.int32)[tile_expert],
            n_used.reshape(1))
    xs = _take_rows(h, src_token)
    g = _moe_up(xs, w1, w3, plan)
    ys = _moe_down(g, w2, plan)
    pos2 = pos.reshape(m_rows, TOP_K)
    ya = _take_rows(ys, pos2[:, 0])
    yb = _take_rows(ys, pos2[:, 1])
    common = (x, ya, yb, route, mods3, layer, n_prompt_rows, dec_seq)
    if final_g is None:
        return _moe_combine(*common, 0, m_rows, None)
    return (_moe_combine(*common, 0, n_prompt_rows, final_g),
            _moe_combine(*common, n_prompt_rows, m_rows - n_prompt_rows, final_g))


def _conv_cols(x_ref, w_ref, b_ref, o_ref, seg, act):
    length, ch = x_ref.shape
    t = lax.broadcasted_iota(jnp.int32, (length, LANE), 0) % seg

    def body(cb, carry):
        c0 = pl.multiple_of(cb * LANE, LANE)
        x = x_ref[:, pl.ds(c0, LANE)]
        w = w_ref[:, pl.ds(c0, LANE)]
        xm1 = jnp.where(t >= 1, pltpu.roll(x, 1, 0), 0.0)
        xp1 = jnp.where(t < seg - 1, pltpu.roll(x, length - 1, 0), 0.0)
        xp2 = jnp.where(t < seg - 2, pltpu.roll(x, length - 2, 0), 0.0)
        y = b_ref[:, pl.ds(c0, LANE)] + (w[0:1] * xm1 + w[1:2] * x + w[2:3] * xp1 + w[3:4] * xp2)
        if act:
            y = _silu(y)
        o_ref[:, pl.ds(c0, LANE)] = y
        return carry

    lax.fori_loop(0, ch // LANE, body, 0)


def _tri_masks(t_len):
    r = lax.broadcasted_iota(jnp.int32, (t_len, t_len), 0)
    c = lax.broadcasted_iota(jnp.int32, (t_len, t_len), 1)
    return r >= c, r <= c


def _mixer_call(body, name, proj, seq_cols, consts, init, st_shape, width, scratch, *, n_seq, length,
                row_off, n_rows_total, y_prev, emit_state, **static):
    st_nd = len(st_shape)
    sps = static.get('sps')
    seq_dim = None if sps is None else sps
    sps = 1 if sps is None else sps
    assert n_seq % sps == 0 and row_off % sps == 0
    rows = sps * length
    row_off = row_off // sps
    n_steps = n_seq // sps
    n_fill = 0
    if y_prev is None:
        assert row_off == 0 and n_rows_total % rows == 0
        n_fill = n_rows_total // rows - n_steps
    own = lambda b: jnp.minimum(b, n_steps - 1)
    in_specs = [pl.BlockSpec((rows, w), lambda b, cb=cb: (own(b) + row_off, cb)) for w, cb in seq_cols]
    args = [proj] * len(seq_cols)
    for c in consts:
        in_specs.append(pl.BlockSpec(c.shape, lambda b, nd=c.ndim: (0,) * nd))
        args.append(c)
    if init is not None:
        in_specs.append(pl.BlockSpec((seq_dim,) + st_shape, lambda b: (own(b),) + (0,) * st_nd))
        args.append(init)
    aliases = {}
    n_alias = 0
    if y_prev is not None:
        aliases[len(args)] = 0
        in_specs.append(pl.BlockSpec(memory_space=pl.ANY))
        args.append(y_prev)
        n_alias += 1
    out_specs = [pl.BlockSpec((rows, width), lambda b: (b + row_off, 0))]
    out_shape = [jax.ShapeDtypeStruct((n_rows_total, width), BF16)]
    if emit_state:
        out_specs.append(pl.BlockSpec((seq_dim,) + st_shape, lambda b: (own(b),) + (0,) * st_nd))
        out_shape.append(jax.ShapeDtypeStruct((n_seq,) + st_shape, F32))
    layout = dict(n_seq_in=len(seq_cols), n_const=len(consts), has_init=init is not None, n_alias=n_alias,
                  emit_state=emit_state)

    def kern(*refs):
        step = pl.program_id(0)

        @pl.when(step < n_steps)
        def _():
            body(*refs, **layout, **static)

        if n_fill:
            @pl.when(step >= n_steps)
            def _():
                y_ref = _split_refs(refs, **layout)[3]
                y_ref[...] = jnp.zeros_like(y_ref)

    return pl.pallas_call(
        kern, grid=(n_steps + n_fill,), in_specs=in_specs, out_specs=tuple(out_specs),
        out_shape=tuple(out_shape), scratch_shapes=scratch, input_output_aliases=aliases,
        compiler_params=_cparams(1), name=name)(*args)


def _split_refs(refs, n_seq_in, n_const, has_init, n_alias, emit_state):
    seq = refs[:n_seq_in]
    consts = refs[n_seq_in:n_seq_in + n_const]
    pos = n_seq_in + n_const
    init = refs[pos] if has_init else None
    pos += int(has_init) + n_alias
    y_ref = refs[pos]
    st_out = refs[pos + 1] if emit_state else None
    pos += 1 + int(emit_state)
    return seq, consts, init, y_ref, st_out, refs[pos:]


def _ssd_kernel(*refs, seg, **layout):
    (z_ref, xs_ref, bc_ref, small_ref), consts, h0_ref, y_ref, hout_ref, scratch = _split_refs(refs, **layout)
    wx_ref, bx_ref, wbc_ref, bbc_ref, dtb_ref, nega_ref, dvec_ref, ng_ref = consts
    xc_ref, bcc_ref, yacc_ref, stt_ref = scratch
    length = z_ref.shape[0]
    t_len = SSD_T
    n_chunks = length // t_len
    n_pairs = SSD_H // 2
    _conv_cols(xs_ref, wx_ref, bx_ref, xc_ref, seg, True)
    _conv_cols(bc_ref, wbc_ref, bbc_ref, bcc_ref, seg, True)
    if h0_ref is None:
        stt_ref[...] = jnp.zeros_like(stt_ref)
    else:
        for d in range(2):
            for hp in range(n_pairs):
                pair = jnp.concatenate([h0_ref[d, 2 * hp], h0_ref[d, 2 * hp + 1]], axis=0)
                stt_ref[d, :, hp * LANE:(hp + 1) * LANE] = pair.T
    yacc_ref[...] = xc_ref[...] * dvec_ref[...]
    lower, upper = _tri_masks(t_len)
    tri = (lower.astype(BF16), upper.astype(BF16))
    masks = (lower, upper)
    lane_lo = lax.broadcasted_iota(jnp.int32, (t_len, LANE), 1) < SSD_P
    gn = SSD_N

    def chunk(c, d):
        r0 = pl.multiple_of(c * t_len, t_len)
        dtv = _softplus(small_ref[pl.ds(r0, t_len), :] + dtb_ref[d])
        cs = _exact_lhs_dot(tri[d], nega_ref[d] * dtv)
        cst = cs.T
        dtt = dtv.T
        end = t_len - 1 if d == 0 else 0
        bcv = bcc_ref[pl.ds(r0, t_len), :]
        groups = []
        for g in range(SSD_G):
            bg = bcv[:, g * gn:(g + 1) * gn]
            cg = bcv[:, (SSD_G + g) * gn:(SSD_G + g + 1) * gn]
            groups.append((_bdot_nt(cg, bg), bg.T, cg))
        for hp in range(n_pairs):
            gm, bgt, cg = groups[(2 * hp) // (SSD_H // SSD_G)]
            cols = slice(hp * LANE, (hp + 1) * LANE)
            x = xc_ref[pl.ds(r0, t_len), cols]
            st = stt_ref[d, :, cols]
            x_lo = jnp.where(lane_lo, x, 0.0).astype(BF16)
            x_hi = jnp.where(lane_lo, 0.0, x).astype(BF16)
            s_lo = jnp.where(lane_lo, st, 0.0).astype(BF16)
            s_hi = jnp.where(lane_lo, 0.0, st).astype(BF16)
            intra, carry_in, upd, edec = [], [], [], []
            for h in (2 * hp, 2 * hp + 1):
                colx = jnp.broadcast_to(cs[:, h:h + 1], (t_len, t_len))
                row = cst[h:h + 1, :]
                dtr = dtt[h:h + 1, :]
                cend = cst[h:h + 1, end:end + 1]
                intra.append((jnp.where(masks[d], jnp.exp(colx - row), 0.0) * (gm * dtr)).astype(BF16))
                carry_in.append((cg * jnp.exp(colx)).astype(BF16))
                upd.append((bgt * (jnp.exp(cend - row) * dtr)).astype(BF16))
                edec.append(jnp.exp(cend))
            xblk = jnp.concatenate([x_lo, x_hi], axis=0)
            y = jnp.dot(jnp.concatenate(intra + carry_in, axis=1),
                        jnp.concatenate([xblk, s_lo, s_hi], axis=0), preferred_element_type=F32)
            snew = jnp.dot(jnp.concatenate(upd, axis=1), xblk, preferred_element_type=F32)
            stt_ref[d, :, cols] = st * jnp.where(lane_lo[0:1], edec[0], edec[1]) + snew
            yacc_ref[pl.ds(r0, t_len), cols] += y

    def body(j, carry):
        chunk(j, 0)
        chunk(n_chunks - 1 - j, 1)
        return carry

    lax.fori_loop(0, n_chunks, body, 0)
    if hout_ref is not None:
        for d in range(2):
            for hp in range(n_pairs):
                pair = stt_ref[d, :, hp * LANE:(hp + 1) * LANE].T
                hout_ref[d, 2 * hp] = pair[:SSD_P]
                hout_ref[d, 2 * hp + 1] = pair[SSD_P:]

    def finish(c, carry):
        r0 = pl.multiple_of(c * t_len, t_len)
        y = yacc_ref[pl.ds(r0, t_len), :] * _silu(z_ref[pl.ds(r0, t_len), :])
        y_ref[pl.ds(r0, t_len), :] = (_rms(y) * ng_ref[...]).astype(y_ref.dtype)
        return carry

    lax.fori_loop(0, n_chunks, finish, 0)


def _ssd_call(proj, p, h0, **where):
    length = where['length']
    dtb = jnp.zeros((2, 1, LANE), F32).at[:, 0, :SSD_H].set(p['ssd_dt_bias'].astype(F32))
    nega = jnp.zeros((2, 1, LANE), F32).at[:, 0, :SSD_H].set(-jnp.exp(p['ssd_A_log'].astype(F32)))
    dvec = jnp.repeat(p['ssd_D'].astype(F32), SSD_P).reshape(1, SSD_D)
    cw = p['ssd_conv_w'].astype(F32)
    cb = p['ssd_conv_b'].astype(F32).reshape(1, SSD_CONV_CH)
    consts = [cw[:, :SSD_D], cb[:, :SSD_D], cw[:, SSD_D:], cb[:, SSD_D:], dtb, nega, dvec,
              p['ssd_norm_g'].astype(F32).reshape(1, SSD_D)]
    seq_cols = [(SSD_D, COL_Z // SSD_D), (SSD_D, COL_XS // SSD_D), (512, COL_BC // 512),
                (LANE, COL_SMALL // LANE)]
    scratch = [pltpu.VMEM((length, SSD_D), F32), pltpu.VMEM((length, 512), F32),
               pltpu.VMEM((length, SSD_D), F32), pltpu.VMEM((2, SSD_N, SSD_D), F32)]
    return _mixer_call(_ssd_kernel, "ssd_mixer", proj, seq_cols, consts, h0, (2, SSD_H, SSD_P, SSD_N),
                       SSD_D, scratch, **where)


def _log_sigmoid(x):
    return jnp.minimum(x, 0.0) - jnp.log1p(jnp.exp(-jnp.abs(x)))


def _gla_kernel(*refs, sps, **layout):
    (q_ref, k_ref, v_ref, g_ref, small_ref), consts, s0_ref, y_ref, sout_ref, scratch = _split_refs(refs, **layout)
    gw_ref, gb_ref, ng_ref = consts
    oacc_ref, st_ref, bc_ref = scratch
    n_rows = q_ref.shape[0]
    length = n_rows // sps
    t_len = GLA_T
    n_chunks = length // t_len
    st_ref[...] = jnp.zeros_like(st_ref)
    if s0_ref is not None:
        for s in range(sps):
            for d in range(2):
                for h in range(GLA_H):
                    st_ref[s, d, h * GLA_DV:(h + 1) * GLA_DV, h * GLA_DK:(h + 1) * GLA_DK] = s0_ref[s, d, h]
    oacc_ref[...] = jnp.zeros_like(oacc_ref)
    lower, upper = _tri_masks(t_len)
    masks = tuple(jnp.concatenate([m] * GLA_H, axis=0) for m in (lower, upper))
    q_head = (lax.broadcasted_iota(jnp.int32, (GLA_H * t_len, GLA_KW), 0) // t_len
              == lax.broadcasted_iota(jnp.int32, (GLA_H * t_len, GLA_KW), 1) // GLA_DK)
    st_diag = (lax.broadcasted_iota(jnp.int32, (GLA_VW, GLA_KW), 0) // GLA_DV
               == lax.broadcasted_iota(jnp.int32, (GLA_VW, GLA_KW), 1) // GLA_DK)
    scale = GLA_DK ** -0.5

    blk = 256
    rr = lax.broadcasted_iota(jnp.int32, (blk, blk), 0)
    cc = lax.broadcasted_iota(jnp.int32, (blk, blk), 1)
    same_chunk = rr // t_len == cc // t_len
    tri_blk = ((same_chunk & (rr >= cc)).astype(BF16), (same_chunk & (rr <= cc)).astype(BF16))
    gate_w = []
    for d in range(2):
        w1, w2 = _split2(gw_ref[d])
        gate_w.append(jnp.concatenate([w1, w1, w2], axis=0))

    def decay_sums(b, carry):
        r0 = pl.multiple_of(b * blk, blk)
        s1, s2 = _split2(small_ref[pl.ds(r0, blk), :])
        lhs = jnp.concatenate([s1, s2, s1], axis=1)
        for d in range(2):
            logit = jnp.dot(lhs, gate_w[d], preferred_element_type=F32) + gb_ref[d]
            la = _log_sigmoid(logit) / GLA_TAU
            bc3 = jnp.dot(tri_blk[d], jnp.concatenate(_split3(la), axis=1), preferred_element_type=F32)
            bc_ref[d, pl.ds(r0, blk), :] = bc3[:, :GLA_KW] + bc3[:, GLA_KW:2 * GLA_KW] + bc3[:, 2 * GLA_KW:]
        return carry

    lax.fori_loop(0, n_rows // blk, decay_sums, 0)

    def chunk(c, d, s):
        r0 = pl.multiple_of(s * length + c * t_len, t_len)
        bc = bc_ref[d, pl.ds(r0, t_len), :]
        end = t_len - 1 if d == 0 else 0
        mid = bc[t_len // 2:t_len // 2 + 1, :]
        tot = bc[end:end + 1, :]
        q = q_ref[pl.ds(r0, t_len), :] * scale
        k = k_ref[pl.ds(r0, t_len), :]
        v = v_ref[pl.ds(r0, t_len), :].astype(BF16)
        qt = q * jnp.exp(bc - mid)
        kt = k * jnp.exp(mid - bc)
        qs = q * jnp.exp(bc)
        kd = k * jnp.exp(tot - bc)
        q_rows = jnp.where(q_head, jnp.concatenate([qt] * GLA_H, axis=0), 0.0)
        att = jnp.where(masks[d], _bdot_nt(q_rows, kt), 0.0)
        o_all = jnp.dot(att.astype(BF16), v, preferred_element_type=F32)
        o = jnp.concatenate([o_all[h * t_len:(h + 1) * t_len, h * GLA_DV:(h + 1) * GLA_DV]
                             for h in range(GLA_H)], axis=1)
        st = st_ref[s, d]
        o = o + _bdot_nt(qs, st)
        upd = lax.dot_general(v, kd.astype(BF16), (((0,), (0,)), ((), ())), preferred_element_type=F32)
        st_ref[s, d] = st * jnp.exp(tot) + jnp.where(st_diag, upd, 0.0)
        oacc_ref[pl.ds(r0, t_len), :] += o

    def body(j, carry):
        for s in range(sps):
            chunk(j, 0, s)
            chunk(n_chunks - 1 - j, 1, s)
        return carry

    lax.fori_loop(0, n_chunks, body, 0)
    if sout_ref is not None:
        for s in range(sps):
            for d in range(2):
                for h in range(GLA_H):
                    sout_ref[s, d, h] = st_ref[s, d, h * GLA_DV:(h + 1) * GLA_DV, h * GLA_DK:(h + 1) * GLA_DK]

    def finish(c, carry):
        r0 = pl.multiple_of(c * t_len, t_len)
        gate = _silu(g_ref[pl.ds(r0, t_len), :])
        for h in range(GLA_H):
            vs = slice(h * GLA_DV, (h + 1) * GLA_DV)
            o = _rms(oacc_ref[pl.ds(r0, t_len), vs]) * ng_ref[...]
            y_ref[pl.ds(r0, t_len), vs] = (o * gate[:, vs]).astype(y_ref.dtype)
        return carry

    lax.fori_loop(0, sps * n_chunks, finish, 0)


def _gla_call(proj, p, s0t, **where):
    length = where['length']
    sps = where['sps']
    gw = jnp.zeros((2, LANE, GLA_KW), F32).at[:, GLA_RANK:2 * GLA_RANK, :].set(p['gla_gate_w'].astype(F32))
    consts = [gw, p['gla_gate_b'].astype(F32).reshape(2, 1, GLA_KW),
              p['gla_norm_g'].astype(F32).reshape(1, GLA_DV)]
    seq_cols = [(GLA_KW, COL_Q // GLA_KW), (GLA_KW, COL_K // GLA_KW), (GLA_VW, COL_V // GLA_VW),
                (GLA_VW, COL_G // GLA_VW), (LANE, COL_SMALL // LANE)]
    st_shape = (2, GLA_H, GLA_DV, GLA_DK)
    scratch = [pltpu.VMEM((sps * length, GLA_VW), F32), pltpu.VMEM((sps, 2, GLA_VW, GLA_KW), F32),
               pltpu.VMEM((2, sps * length, GLA_KW), F32)]
    return _mixer_call(_gla_kernel, "gla_mixer", proj, seq_cols, consts, s0t, st_shape, GLA_VW, scratch, **where)


def _lru_kernel(*refs, seg, sps, **layout):
    (xb_ref, gb_ref), consts, h0_ref, y_ref, hout_ref, scratch = _split_refs(refs, **layout)
    cw_ref, cb_ref, w_ref, bias_ref, sp_ref = consts
    xr_ref, a_ref, u_ref = scratch
    n_rows = xb_ref.shape[0]
    length = n_rows // sps
    blk = 256
    _conv_cols(xb_ref, cw_ref, cb_ref, xr_ref, seg, False)

    def gates(c, carry):
        r0 = pl.multiple_of(c * blk, blk)
        xr = xr_ref[pl.ds(r0, blk), :]
        xbf = xr.astype(BF16)
        for d in range(2):
            pre = jnp.dot(xbf, w_ref[d], preferred_element_type=F32) + bias_ref[d]
            r = jax.nn.sigmoid(pre[:, :LRU_W])
            i = jax.nn.sigmoid(pre[:, LRU_W:])
            log_a = (-LRU_C) * r * sp_ref[d]
            a = jnp.exp(log_a)
            a_ref[d, pl.ds(r0, blk), :] = a
            u_ref[d, pl.ds(r0, blk), :] = jnp.sqrt(-jnp.tanh(log_a) * (a * a + 1.0)) * (i * xr)
        return carry

    lax.fori_loop(0, n_rows // blk, gates, 0)

    def scan(t, carry):
        out = []
        for s in range(sps):
            for d in range(2):
                row = s * length + (t if d == 0 else length - 1 - t)
                h = a_ref[d, pl.ds(row, 1), :] * carry[2 * s + d] + u_ref[d, pl.ds(row, 1), :]
                u_ref[d, pl.ds(row, 1), :] = h
                out.append(h)
        return tuple(out)

    if h0_ref is None:
        start = tuple(jnp.zeros((1, LRU_W), F32) for _ in range(2 * sps))
    else:
        start = tuple(h0_ref[s, d] for s in range(sps) for d in range(2))
    final = lax.fori_loop(0, length, scan, start)
    if hout_ref is not None:
        for s in range(sps):
            for d in range(2):
                hout_ref[s, d] = final[2 * s + d]

    def finish(c, carry):
        r0 = pl.multiple_of(c * blk, blk)
        y = (u_ref[0, pl.ds(r0, blk), :] + u_ref[1, pl.ds(r0, blk), :]) * jax.nn.gelu(gb_ref[pl.ds(r0, blk), :])
        y_ref[pl.ds(r0, blk), :] = y.astype(y_ref.dtype)
        return carry

    lax.fori_loop(0, n_rows // blk, finish, 0)


def _block_diag(w):
    nb, bw, _ = w.shape
    eye = jnp.eye(nb, dtype=w.dtype)
    return (eye[:, None, :, None] * w[:, :, None, :]).reshape(nb * bw, nb * bw)


def _lru_call(proj, p, h0, **where):
    length = where['length']
    w = jnp.stack([jnp.concatenate([_block_diag(p['lru_wa'][d].astype(F32)),
                                    _block_diag(p['lru_wx'][d].astype(F32))], axis=1) for d in range(2)])
    bias = jnp.stack([jnp.concatenate([p['lru_ba'][d], p['lru_bx'][d]]) for d in range(2)]).astype(F32)
    sp = jax.nn.softplus(-p['lru_lambda'].astype(F32)).reshape(2, 1, LRU_W)
    consts = [p['lru_conv_w'].astype(F32), p['lru_conv_b'].astype(F32).reshape(1, LRU_W), w.astype(BF16),
              bias.reshape(2, 1, 2 * LRU_W), sp]
    seq_cols = [(LRU_W, COL_XB // LRU_W), (LRU_W, COL_GB // LRU_W)]
    rows = where['sps'] * length
    scratch = [pltpu.VMEM((rows, LRU_W), F32), pltpu.VMEM((2, rows, LRU_W), F32),
               pltpu.VMEM((2, rows, LRU_W), F32)]
    return _mixer_call(_lru_kernel, "lru_mixer", proj, seq_cols, consts, h0, (2, 1, LRU_W), LRU_W, scratch,
                       **where)


def kernel(x_prompt, x_sample, state_ssd, state_gla, state_lru, c, c_ctx, mod_w, mod_b, norm1_g, norm2_g, in_w, ssd_conv_w, ssd_conv_b, ssd_A_log, ssd_dt_bias, ssd_D, ssd_norm_g, gla_gate_w, gla_gate_b, gla_norm_g, lru_conv_w, lru_conv_b, lru_wa, lru_ba, lru_wx, lru_bx, lru_lambda, out_w, ffn_w1, ffn_w3, ffn_w2, moe_router, moe_w1, moe_w3, moe_w2, final_norm_g):
    bp, lp, d = x_prompt.shape
    bs, ls, _ = x_sample.shape
    n_p = bp * lp
    n_s = bs * ls
    n_all = n_p + n_s
    depth = in_w.shape[0]
    assert n_p % ls == 0 and 1 + bs <= MOD_ROWS

    cvec = jnp.zeros((MOD_ROWS, d), F32).at[0].set(c_ctx.astype(F32)).at[1:1 + bs].set(c.astype(F32))
    mods = _adaln(cvec, mod_w, mod_b)
    mods3 = mods.reshape(depth * MOD_ROWS * 6, 1, d)
    x = jnp.concatenate([x_prompt.reshape(n_p, d), x_sample.reshape(n_s, d)], axis=0)

    in_w_t = jnp.swapaxes(in_w, 1, 2)
    st_ssd, st_gla, st_lru = [], [], []
    for i in range(depth):
        p = {'ssd_conv_w': ssd_conv_w[i], 'ssd_conv_b': ssd_conv_b[i], 'ssd_A_log': ssd_A_log[i],
             'ssd_dt_bias': ssd_dt_bias[i], 'ssd_D': ssd_D[i], 'ssd_norm_g': ssd_norm_g[i],
             'gla_gate_w': gla_gate_w[i], 'gla_gate_b': gla_gate_b[i], 'gla_norm_g': gla_norm_g[i],
             'lru_conv_w': lru_conv_w[i], 'lru_conv_b': lru_conv_b[i], 'lru_wa': lru_wa[i],
             'lru_ba': lru_ba[i], 'lru_wx': lru_wx[i], 'lru_bx': lru_bx[i], 'lru_lambda': lru_lambda[i]}
        h = _norm_mod(x, norm1_g[i], mods3, i, 0, n_p, ls)
        proj = _in_proj(h, in_w_t, i)
        ctx = dict(n_seq=bp, length=lp, row_off=0, n_rows_total=n_all, y_prev=None, emit_state=True)
        lat = dict(n_seq=bs, length=ls, row_off=n_p // ls, n_rows_total=n_all, emit_state=False)
        y_ssd, s_ssd = _ssd_call(proj, p, None, seg=lp, **ctx)
        y_ssd, = _ssd_call(proj, p, state_ssd[:, i].astype(F32), seg=GRID_W, y_prev=y_ssd, **lat)
        y_gla, s_gla = _gla_call(proj, p, None, sps=MIX_SPS, **ctx)
        y_gla, = _gla_call(proj, p, jnp.swapaxes(state_gla[:, i].astype(F32), -1, -2), sps=MIX_SPS,
                           y_prev=y_gla, **lat)
        y_lru, s_lru = _lru_call(proj, p, None, seg=lp, sps=MIX_SPS, **ctx)
        y_lru, = _lru_call(proj, p, state_lru[:, i].astype(F32).reshape(bs, 2, 1, LRU_W), seg=GRID_W,
                           sps=MIX_SPS, y_prev=y_lru, **lat)
        st_ssd.append(s_ssd)
        st_gla.append(jnp.swapaxes(s_gla, -1, -2))
        st_lru.append(s_lru.reshape(bp, 2, LRU_W))
        x = _out_proj(y_ssd, y_gla, y_lru, out_w, x, mods3, i, n_p, ls)
        j = i // 2
        if i % 2 == 1:
            h2, route = _norm_mod(x, norm2_g[i], mods3, i, 1, n_p, ls, router=moe_router[j])
            x = _moe_ffn(x, h2, route, moe_w1[j], moe_w3[j], moe_w2[j], mods3, i, n_p, ls,
                         final_g=final_norm_g if i == depth - 1 else None)
        else:
            h2 = _norm_mod(x, norm2_g[i], mods3, i, 1, n_p, ls)
            x = _ffn_down(_ffn_up(h2, ffn_w1[j], ffn_w3[j]), ffn_w2[j], x, mods3, i, n_p, ls)
    if isinstance(x, tuple):
        y_p, y_s = x
    else:
        y_p = _final_norm(x, final_norm_g, 0, n_p)
        y_s = _final_norm(x, final_norm_g, n_p, n_s)
    return (y_p.reshape(bp, lp, d), y_s.reshape(bs, ls, d), jnp.stack(st_ssd, axis=1),
            jnp.stack(st_gla, axis=1), jnp.stack(st_lru, axis=1))
```

```python
import functools

import jax
import jax.numpy as jnp
from jax import lax
from jax.experimental import pallas as pl
from jax.experimental.pallas import tpu as pltpu

F32 = jnp.float32
BF16 = jnp.bfloat16

D_MODEL = 2048
GRID_W = 64
SSD_D = D_MODEL // 2
SSD_P = 64
SSD_H = SSD_D // SSD_P
SSD_G = 2
SSD_N = 128
GLA_H = 4
GLA_VW = D_MODEL // 4
GLA_DV = GLA_VW // GLA_H
GLA_DK = GLA_DV // 2
GLA_KW = GLA_H * GLA_DK
GLA_RANK = 16
GLA_TAU = 16.0
LRU_W = D_MODEL // 4
LRU_NB = 8
LRU_BW = LRU_W // LRU_NB
LRU_C = 8.0
SSD_CONV_CH = SSD_D + 2 * SSD_G * SSD_N
N_EXPERTS = 8
TOP_K = 2
EPS = 1e-6

LANE = 128
SSD_T = 128
GLA_T = 64
MIX_SPS = 2
MOD_ROWS = 8
VMEM_LIMIT = 56 * 1024 * 1024

COL_Z = 0
COL_XS = 1024
COL_BC = 2048
COL_Q = 2560
COL_K = 2816
COL_V = 3072
COL_G = 3584
COL_XB = 4096
COL_GB = 4608
COL_SMALL = 5120
IN_PAD = 5632


def _cparams(n_axes):
    return pltpu.CompilerParams(dimension_semantics=("arbitrary",) * n_axes,
                                vmem_limit_bytes=VMEM_LIMIT)


def _bdot(a, b):
    return jnp.dot(a.astype(BF16), b.astype(BF16), preferred_element_type=F32)


def _bdot_nt(a, b):
    return lax.dot_general(a.astype(BF16), b.astype(BF16), (((1,), (1,)), ((), ())),
                           preferred_element_type=F32)


def _bdot_tn(a, b):
    return lax.dot_general(a.astype(BF16), b.astype(BF16), (((0,), (0,)), ((), ())),
                           preferred_element_type=F32)


def _split2(a):
    a1 = a.astype(BF16)
    a2 = (a - a1.astype(F32)).astype(BF16)
    return a1, a2


def _split3(a):
    a1 = a.astype(BF16)
    r = a - a1.astype(F32)
    a2 = r.astype(BF16)
    a3 = (r - a2.astype(F32)).astype(BF16)
    return a1, a2, a3


def _exact_lhs_dot(m_bf16, a):
    a1, a2, a3 = _split3(a)
    f = lambda z: jnp.dot(m_bf16, z, preferred_element_type=F32)
    return f(a1) + f(a2) + f(a3)


def _dot3(a, b):
    a1, a2 = _split2(a)
    b1, b2 = _split2(b)
    f = lambda x, y: jnp.dot(x, y, preferred_element_type=F32)
    return f(a1, b1) + (f(a1, b2) + f(a2, b1))


def _softplus(x):
    return jnp.maximum(x, 0.0) + jnp.log1p(jnp.exp(-jnp.abs(x)))


def _silu(x):
    return x * jax.nn.sigmoid(x)


def _mod_row(m, tm, n_prompt_rows, dec_seq):
    r0 = m * tm
    return jnp.where(r0 < n_prompt_rows, 0, 1 + (r0 - n_prompt_rows) // dec_seq)


def _adaln_kernel(c_ref, w_ref, b_ref, o_ref):
    s = _silu(c_ref[...])
    o_ref[...] = _dot3(s, w_ref[...]) + b_ref[...]


def _adaln(cvec8, mod_w, mod_b):
    depth, d, n = mod_w.shape
    tn = 1024
    return pl.pallas_call(
        _adaln_kernel,
        grid=(depth, n // tn),
        in_specs=[pl.BlockSpec((MOD_ROWS, d), lambda i, j: (0, 0)),
                  pl.BlockSpec((None, d, tn), lambda i, j: (i, 0, j)),
                  pl.BlockSpec((None, 1, tn), lambda i, j: (i, 0, j))],
        out_specs=pl.BlockSpec((None, MOD_ROWS, tn), lambda i, j: (i, 0, j)),
        out_shape=jax.ShapeDtypeStruct((depth, MOD_ROWS, n), F32),
        compiler_params=_cparams(2),
        name="adaln",
    )(cvec8, mod_w, mod_b.reshape(depth, 1, n))


def _rms(x):
    return x * lax.rsqrt(jnp.mean(x * x, axis=-1, keepdims=True) + EPS)


def _stacked_specs(x, block, row_tile, col_block):
    if not isinstance(x, tuple):
        return [pl.BlockSpec(block, lambda *g: (row_tile(*g), col_block(*g)))], [x], 0
    split = x[0].shape[0] // block[0]
    specs = [pl.BlockSpec(block, lambda *g: (jnp.minimum(row_tile(*g), split - 1), col_block(*g))),
             pl.BlockSpec(block, lambda *g: (jnp.maximum(row_tile(*g) - split, 0), col_block(*g)))]
    return specs, list(x), split


def _stacked_tile(x_refs, m, split):
    if len(x_refs) == 1:
        return x_refs[0][...]
    return jnp.where(m < split, x_refs[0][...], x_refs[1][...])


def _norm_mod_kernel(*refs, split):
    g_ref, sh_ref, sc_ref, o_ref = refs[-4:]
    y = _rms(_stacked_tile(refs[:-4], pl.program_id(0), split)) * g_ref[...]
    o_ref[...] = (y * (1.0 + sc_ref[...]) + sh_ref[...]).astype(o_ref.dtype)


def _norm_mod_route_kernel(x_ref, g_ref, sh_ref, sc_ref, r_ref, o_ref, route_ref):
    y = _rms(x_ref[...]) * g_ref[...]
    h = y * (1.0 + sc_ref[...]) + sh_ref[...]
    o_ref[...] = h.astype(o_ref.dtype)
    logits = _dot3(h, r_ref[...])
    lane = lax.broadcasted_iota(jnp.int32, logits.shape, 1)
    neg = jnp.float32(-jnp.inf)
    lg = jnp.where(lane < N_EXPERTS, logits, neg)
    m1 = jnp.max(lg, axis=-1, keepdims=True)
    i1 = jnp.min(jnp.where(lg == m1, lane, LANE), axis=-1, keepdims=True)
    lg2 = jnp.where(lane == i1, neg, lg)
    m2 = jnp.max(lg2, axis=-1, keepdims=True)
    i2 = jnp.min(jnp.where(lg2 == m2, lane, LANE), axis=-1, keepdims=True)
    e2 = jnp.exp(m2 - m1)
    den = 1.0 + e2
    g1 = 1.0 / den
    g2 = e2 / den
    out = jnp.where(lane == 0, i1.astype(F32),
                    jnp.where(lane == 1, i2.astype(F32),
                              jnp.where(lane == 2, g1, jnp.where(lane == 3, g2, 0.0))))
    route_ref[...] = out


def _norm_mod(x, g, mods3, layer, which, n_prompt_rows, dec_seq, router=None):
    d = g.shape[0]
    m_rows = sum(part.shape[0] for part in x) if isinstance(x, tuple) else x.shape[0]
    tm = 512
    base = layer * MOD_ROWS * 6

    def mod_spec(k):
        return pl.BlockSpec((None, 1, d),
                            lambda m: (base + _mod_row(m, tm, n_prompt_rows, dec_seq) * 6 + k, 0, 0))

    x_specs, x_args, split = _stacked_specs(x, (tm, d), lambda m: m, lambda m: 0)
    in_specs = x_specs + [pl.BlockSpec((1, d), lambda m: (0, 0)), mod_spec(3 * which), mod_spec(3 * which + 1)]
    args = x_args + [g.reshape(1, d), mods3, mods3]
    h_spec = pl.BlockSpec((tm, d), lambda m: (m, 0))
    if router is None:
        return pl.pallas_call(
            functools.partial(_norm_mod_kernel, split=split), grid=(m_rows // tm,), in_specs=in_specs,
            out_specs=h_spec, out_shape=jax.ShapeDtypeStruct((m_rows, d), BF16),
            compiler_params=_cparams(1), name="norm_mod")(*args)
    assert not isinstance(x, tuple)
    r_pad = jnp.zeros((d, LANE), F32).at[:, :N_EXPERTS].set(router.astype(F32))
    return pl.pallas_call(
        _norm_mod_route_kernel, grid=(m_rows // tm,),
        in_specs=in_specs + [pl.BlockSpec((d, LANE), lambda m: (0, 0))],
        out_specs=(h_spec, pl.BlockSpec((tm, LANE), lambda m: (m, 0))),
        out_shape=(jax.ShapeDtypeStruct((m_rows, d), F32), jax.ShapeDtypeStruct((m_rows, LANE), F32)),
        compiler_params=_cparams(1), name="norm_mod_route")(*args, r_pad)


def _final_norm_kernel(x_ref, g_ref, o_ref):
    o_ref[...] = _rms(x_ref[...]) * g_ref[...]


def _final_norm(x, g, row0, n_rows):
    d = x.shape[1]
    tm = 512
    off = row0 // tm
    return pl.pallas_call(
        _final_norm_kernel, grid=(n_rows // tm,),
        in_specs=[pl.BlockSpec((tm, d), lambda m: (m + off, 0)), pl.BlockSpec((1, d), lambda m: (0, 0))],
        out_specs=pl.BlockSpec((tm, d), lambda m: (m, 0)),
        out_shape=jax.ShapeDtypeStruct((n_rows, d), F32),
        compiler_params=_cparams(1), name="final_norm")(x, g.reshape(1, d))


IN_TN = 512
IN_DT_COL = SSD_D + SSD_CONV_CH
IN_GRAW_COL = IN_DT_COL + SSD_H + 2 * GLA_KW + 2 * GLA_VW


def _in_proj_kernel(x_ref, wm_ref, wn_ref, wdt_ref, wgr_ref, o_ref, wbf_ref):
    j = pl.program_id(0)

    @pl.when(pl.program_id(1) == 0)
    def _():
        def shifted(s):
            w = jnp.concatenate([wm_ref[...], wn_ref[...]], axis=0)
            wbf_ref[...] = w[s:s + IN_TN].astype(BF16)

        @pl.when(j < COL_Q // IN_TN)
        def _():
            wbf_ref[...] = wm_ref[...].astype(BF16)

        @pl.when((j >= COL_Q // IN_TN) & (j < COL_XB // IN_TN))
        def _():
            shifted(SSD_H)

        @pl.when((j >= COL_XB // IN_TN) & (j < COL_SMALL // IN_TN))
        def _():
            shifted(SSD_H + GLA_RANK)

        @pl.when(j == COL_SMALL // IN_TN)
        def _():
            wbf_ref[...] = jnp.zeros_like(wbf_ref)
            wbf_ref[0:SSD_H] = wdt_ref[...].astype(BF16)
            wbf_ref[SSD_H:SSD_H + GLA_RANK] = wgr_ref[...].astype(BF16)

    o_ref[...] = lax.dot_general(x_ref[...], wbf_ref[...], (((1,), (1,)), ((), ())),
                                 preferred_element_type=F32)


def _in_proj(h, in_w_t, layer):
    m_rows, k = h.shape
    tm, tn = 2048, IN_TN
    n_main = COL_SMALL // tn
    shift_max = SSD_H + GLA_RANK
    last_next_block = in_w_t.shape[1] // shift_max - 1
    return pl.pallas_call(
        _in_proj_kernel, grid=(IN_PAD // tn, m_rows // tm),
        in_specs=[pl.BlockSpec((tm, k), lambda j, m: (m, 0)),
                  pl.BlockSpec((None, tn, k), lambda j, m: (layer, jnp.minimum(j, n_main - 1), 0)),
                  pl.BlockSpec((None, shift_max, k),
                               lambda j, m: (layer, jnp.minimum((j + 1) * (tn // shift_max), last_next_block), 0)),
                  pl.BlockSpec((None, SSD_H, k), lambda j, m: (layer, IN_DT_COL // SSD_H, 0)),
                  pl.BlockSpec((None, GLA_RANK, k), lambda j, m: (layer, IN_GRAW_COL // GLA_RANK, 0))],
        out_specs=pl.BlockSpec((tm, tn), lambda j, m: (m, j)),
        out_shape=jax.ShapeDtypeStruct((m_rows, IN_PAD), F32),
        scratch_shapes=[pltpu.VMEM((tn, k), BF16)],
        compiler_params=_cparams(2), name="in_proj")(h, in_w_t, in_w_t, in_w_t, in_w_t)


def _out_proj_kernel(y1_ref, y2_ref, y3_ref, w_ref, *rest, split):
    gate_ref, o_ref, wbf_ref = rest[-3:]

    @pl.when(pl.program_id(1) == 0)
    def _():
        wbf_ref[...] = w_ref[...].astype(BF16)

    k1 = y1_ref.shape[1]
    k2 = k1 + y2_ref.shape[1]
    acc = jnp.dot(y1_ref[...], wbf_ref[0:k1, :], preferred_element_type=F32)
    acc += jnp.dot(y2_ref[...], wbf_ref[k1:k2, :], preferred_element_type=F32)
    acc += jnp.dot(y3_ref[...], wbf_ref[k2:, :], preferred_element_type=F32)
    o_ref[...] = _stacked_tile(rest[:-3], pl.program_id(1), split) + gate_ref[...] * acc


def _out_proj(y1, y2, y3, w, res, mods3, layer, n_prompt_rows, dec_seq):
    m_rows = y1.shape[0]
    k, d = w.shape[1:]
    tm = 1024
    tn = 512 if isinstance(res, tuple) else 1024
    base = layer * MOD_ROWS * 6
    res_specs, res_args, split = _stacked_specs(res, (tm, tn), lambda j, m: m, lambda j, m: j)
    return pl.pallas_call(
        functools.partial(_out_proj_kernel, split=split), grid=(d // tn, m_rows // tm),
        in_specs=[pl.BlockSpec((tm, y1.shape[1]), lambda j, m: (m, 0)),
                  pl.BlockSpec((tm, y2.shape[1]), lambda j, m: (m, 0)),
                  pl.BlockSpec((tm, y3.shape[1]), lambda j, m: (m, 0)),
                  pl.BlockSpec((None, k, tn), lambda j, m: (layer, 0, j))] + res_specs + [
                  pl.BlockSpec((None, 1, tn),
                               lambda j, m: (base + _mod_row(m, tm, n_prompt_rows, dec_seq) * 6 + 2, 0, j))],
        out_specs=pl.BlockSpec((tm, tn), lambda j, m: (m, j)),
        out_shape=jax.ShapeDtypeStruct((m_rows, d), F32),
        scratch_shapes=[pltpu.VMEM((k, tn), BF16)],
        compiler_params=_cparams(2), name="out_proj")(y1, y2, y3, w, *res_args, mods3)


def _ffn_up_kernel(x_ref, w1_ref, w3_ref, o_ref, w1bf_ref, w3bf_ref):
    @pl.when(pl.program_id(1) == 0)
    def _():
        w1bf_ref[...] = w1_ref[...].astype(BF16)
        w3bf_ref[...] = w3_ref[...].astype(BF16)

    a = jnp.dot(x_ref[...], w1bf_ref[...], preferred_element_type=F32)
    b = jnp.dot(x_ref[...], w3bf_ref[...], preferred_element_type=F32)
    o_ref[...] = (_silu(a) * b).astype(o_ref.dtype)


def _ffn_up(h, w1, w3):
    m_rows, d = h.shape
    f = w1.shape[1]
    tm, tf = 1024, 512
    return pl.pallas_call(
        _ffn_up_kernel, grid=(pl.cdiv(f, tf), m_rows // tm),
        in_specs=[pl.BlockSpec((tm, d), lambda j, m: (m, 0)),
                  pl.BlockSpec((d, tf), lambda j, m: (0, j)),
                  pl.BlockSpec((d, tf), lambda j, m: (0, j))],
        out_specs=pl.BlockSpec((tm, tf), lambda j, m: (m, j)),
        out_shape=jax.ShapeDtypeStruct((m_rows, f), BF16),
        scratch_shapes=[pltpu.VMEM((d, tf), BF16), pltpu.VMEM((d, tf), BF16)],
        compiler_params=_cparams(2), name="ffn_up")(h, w1, w3)


def _ffn_down_kernel(g_ref, w_ref, res_ref, gate_ref, o_ref, wbf_ref):
    @pl.when(pl.program_id(1) == 0)
    def _():
        wbf_ref[...] = w_ref[...].astype(BF16)

    acc = jnp.dot(g_ref[...], wbf_ref[...], preferred_element_type=F32)
    o_ref[...] = res_ref[...] + gate_ref[...] * acc


def _ffn_down(g, w2, res, mods3, layer, n_prompt_rows, dec_seq):
    m_rows, d = res.shape
    f = w2.shape[0]
    tm, tn = 512, 512
    base = layer * MOD_ROWS * 6
    return pl.pallas_call(
        _ffn_down_kernel, grid=(d // tn, m_rows // tm),
        in_specs=[pl.BlockSpec((tm, f), lambda j, m: (m, 0)),
                  pl.BlockSpec((f, tn), lambda j, m: (0, j)),
                  pl.BlockSpec((tm, tn), lambda j, m: (m, j)),
                  pl.BlockSpec((None, 1, tn),
                               lambda j, m: (base + _mod_row(m, tm, n_prompt_rows, dec_seq) * 6 + 5, 0, j))],
        out_specs=pl.BlockSpec((tm, tn), lambda j, m: (m, j)),
        out_shape=jax.ShapeDtypeStruct((m_rows, d), F32),
        scratch_shapes=[pltpu.VMEM((f, tn), BF16)],
        compiler_params=_cparams(2), name="ffn_down")(g, w2, res, mods3)


MOE_TM = 256


def _grouped_weights(plan_refs, w_hbm, stages, casts, sems):
    te_ref, first_ref, next_e_ref, last_ref = plan_refs
    j = pl.program_id(0)
    m = pl.program_id(1)
    n_col_tiles = pl.num_programs(0)

    def fetch(e, jj, k):
        width = stages[k].shape[1]
        src = w_hbm[k].at[e, :, pl.ds(pl.multiple_of(jj * width, width), width)]
        return pltpu.make_async_copy(src, stages[k], sems.at[k])

    @pl.when((j == 0) & (m == 0))
    def _():
        for k in range(len(stages)):
            fetch(te_ref[0], 0, k).start()

    @pl.when(first_ref[m] == 1)
    def _():
        for k in range(len(stages)):
            fetch(te_ref[m], j, k).wait()
            casts[k][...] = stages[k][...].astype(BF16)
        next_j = j + last_ref[m]

        @pl.when(next_j < n_col_tiles)
        def _():
            for k in range(len(stages)):
                fetch(next_e_ref[m], next_j, k).start()


def _moe_up_kernel(te_ref, first_ref, next_e_ref, last_ref, nu_ref, x_ref, w1_hbm, w3_hbm, o_ref,
                   w1st_ref, w3st_ref, w1bf_ref, w3bf_ref, sems):
    m = pl.program_id(1)
    _grouped_weights((te_ref, first_ref, next_e_ref, last_ref), (w1_hbm, w3_hbm), (w1st_ref, w3st_ref),
                     (w1bf_ref, w3bf_ref), sems)

    @pl.when(m < nu_ref[0])
    def _():
        x = x_ref[...].astype(BF16)
        a = jnp.dot(x, w1bf_ref[...], preferred_element_type=F32)
        b = jnp.dot(x, w3bf_ref[...], preferred_element_type=F32)
        o_ref[...] = (_silu(a) * b).astype(o_ref.dtype)

    @pl.when(m >= nu_ref[0])
    def _():
        o_ref[...] = jnp.zeros_like(o_ref)


def _moe_up(xs, w1, w3, plan):
    n_rows, d = xs.shape
    f = w1.shape[2]
    tm, tf = MOE_TM, 1792
    n_plan = len(plan)
    grid_spec = pltpu.PrefetchScalarGridSpec(
        num_scalar_prefetch=n_plan, grid=(f // tf, n_rows // tm),
        in_specs=[pl.BlockSpec((tm, d), lambda j, m, *_: (m, 0)),
                  pl.BlockSpec(memory_space=pl.ANY), pl.BlockSpec(memory_space=pl.ANY)],
        out_specs=pl.BlockSpec((tm, tf), lambda j, m, *_: (m, j)),
        scratch_shapes=[pltpu.VMEM((d, tf), F32), pltpu.VMEM((d, tf), F32),
                        pltpu.VMEM((d, tf), BF16), pltpu.VMEM((d, tf), BF16),
                        pltpu.SemaphoreType.DMA((2,))])
    return pl.pallas_call(
        _moe_up_kernel, grid_spec=grid_spec,
        out_shape=jax.ShapeDtypeStruct((n_rows, f), BF16),
        compiler_params=_cparams(2), name="moe_up")(*plan, xs, w1, w3)


def _moe_down_kernel(te_ref, first_ref, next_e_ref, last_ref, nu_ref, g_ref, w_hbm, o_ref,
                     wst_ref, wbf_ref, sems):
    m = pl.program_id(1)
    _grouped_weights((te_ref, first_ref, next_e_ref, last_ref), (w_hbm,), (wst_ref,), (wbf_ref,), sems)

    @pl.when(m < nu_ref[0])
    def _():
        o_ref[...] = jnp.dot(g_ref[...], wbf_ref[...], preferred_element_type=F32)

    @pl.when(m >= nu_ref[0])
    def _():
        o_ref[...] = jnp.zeros_like(o_ref)


def _moe_down(g, w2, plan):
    n_rows, f = g.shape
    d = w2.shape[2]
    tm, tn = MOE_TM, 1024
    grid_spec = pltpu.PrefetchScalarGridSpec(
        num_scalar_prefetch=len(plan), grid=(d // tn, n_rows // tm),
        in_specs=[pl.BlockSpec((tm, f), lambda j, m, *_: (m, 0)),
                  pl.BlockSpec(memory_space=pl.ANY)],
        out_specs=pl.BlockSpec((tm, tn), lambda j, m, *_: (m, j)),
        scratch_shapes=[pltpu.VMEM((f, tn), F32), pltpu.VMEM((f, tn), BF16),
                        pltpu.SemaphoreType.DMA((1,))])
    return pl.pallas_call(
        _moe_down_kernel, grid_spec=grid_spec,
        out_shape=jax.ShapeDtypeStruct((n_rows, d), F32),
        compiler_params=_cparams(2), name="moe_down")(*plan, g, w2)


def _combine_kernel(x_ref, ya_ref, yb_ref, route_ref, gate_ref, *rest):
    ga = route_ref[:, TOP_K:TOP_K + 1]
    gb = route_ref[:, TOP_K + 1:TOP_K + 2]
    v = x_ref[...] + gate_ref[...] * (ga * ya_ref[...] + gb * yb_ref[...])
    if len(rest) == 2:
        g_ref, o_ref = rest
        o_ref[...] = _rms(v) * g_ref[...]
    else:
        rest[0][...] = v


def _moe_combine(x, ya, yb, route, mods3, layer, n_prompt_rows, dec_seq, row0, n_rows, final_g):
    d = x.shape[1]
    tm = 512
    off = row0 // tm
    base = layer * MOD_ROWS * 6
    row = pl.BlockSpec((tm, d), lambda m: (m + off, 0))
    in_specs = [row, row, row, pl.BlockSpec((tm, LANE), lambda m: (m + off, 0)),
                pl.BlockSpec((None, 1, d),
                             lambda m: (base + _mod_row(m + off, tm, n_prompt_rows, dec_seq) * 6 + 5, 0, 0))]
    args = [x, ya, yb, route, mods3]
    if final_g is not None:
        in_specs.append(pl.BlockSpec((1, d), lambda m: (0, 0)))
        args.append(final_g.reshape(1, d))
    return pl.pallas_call(
        _combine_kernel, grid=(n_rows // tm,), in_specs=in_specs,
        out_specs=pl.BlockSpec((tm, d), lambda m: (m, 0)),
        out_shape=jax.ShapeDtypeStruct((n_rows, d), F32),
        compiler_params=_cparams(1), name="moe_combine")(*args)


def _take_rows(a, idx):
    return a.at[idx].get(mode="promise_in_bounds")


def _moe_ffn(x, h, route, w1, w3, w2, mods3, layer, n_prompt_rows, dec_seq, final_g=None):
    m_rows, d = x.shape
    tm = MOE_TM
    n_slots = m_rows * TOP_K
    n_rows = n_slots + N_EXPERTS * tm
    n_tiles = n_rows // tm
    top_i = route[:, 0:TOP_K].astype(jnp.int32)
    flat_e = top_i.reshape(n_slots)
    onehot = (flat_e[:, None] == jnp.arange(N_EXPERTS, dtype=jnp.int32)[None, :]).astype(jnp.int32)
    counts = jnp.sum(onehot, axis=0)
    rank = jnp.sum((jnp.cumsum(onehot, axis=0) - onehot) * onehot, axis=1)
    padded = ((counts + tm - 1) // tm) * tm
    group_end = jnp.cumsum(padded)
    group_start = group_end - padded
    pos = group_start[flat_e] + rank
    src_token = (jnp.arange(n_rows, dtype=jnp.int32) % m_rows).at[pos].set(
        jnp.arange(n_slots, dtype=jnp.int32) // TOP_K, unique_indices=True, mode="promise_in_bounds")
    n_used = (group_end[-1] // tm).astype(jnp.int32)
    tiles = jnp.arange(n_tiles, dtype=jnp.int32)
    tile_start = jnp.minimum(tiles, n_used - 1) * tm
    tile_expert = jnp.sum((tile_start[:, None] >= group_end[None, :]).astype(jnp.int32), axis=1)
    tile_expert = jnp.minimum(tile_expert, N_EXPERTS - 1).astype(jnp.int32)
    prev_expert = jnp.concatenate([jnp.full((1,), -1, jnp.int32), tile_expert[:-1]])
    first = ((tile_expert != prev_expert) & (tiles < n_used)).astype(jnp.int32)
    ids = jnp.arange(N_EXPERTS, dtype=jnp.int32)
    later = jnp.where((ids[None, :] > ids[:, None]) & (counts[None, :] > 0), ids[None, :], N_EXPERTS)
    next_expert = jnp.min(later, axis=1)
    is_last = next_expert == N_EXPERTS
    next_expert = jnp.where(is_last, tile_expert[0], next_expert)
    plan = (tile_expert, first, next_expert[tile_expert], is_last.astype(jnp.int32)[tile_expert],
            n_used.reshape(1))
    xs = _take_rows(h, src_token)
    g = _moe_up(xs, w1, w3, plan)
    ys = _moe_down(g, w2, plan)
    pos2 = pos.reshape(m_rows, TOP_K)
    ya = _take_rows(ys, pos2[:, 0])
    yb = _take_rows(ys, pos2[:, 1])
    common = (x, ya, yb, route, mods3, layer, n_prompt_rows, dec_seq)
    if final_g is None:
        return _moe_combine(*common, 0, m_rows, None)
    return (_moe_combine(*common, 0, n_prompt_rows, final_g),
            _moe_combine(*common, n_prompt_rows, m_rows - n_prompt_rows, final_g))


def _conv_cols(x_ref, w_ref, b_ref, o_ref, seg, act):
    length, ch = x_ref.shape
    t = lax.broadcasted_iota(jnp.int32, (length, LANE), 0) % seg

    def body(cb, carry):
        c0 = pl.multiple_of(cb * LANE, LANE)
        x = x_ref[:, pl.ds(c0, LANE)]
        w = w_ref[:, pl.ds(c0, LANE)]
        xm1 = jnp.where(t >= 1, pltpu.roll(x, 1, 0), 0.0)
        xp1 = jnp.where(t < seg - 1, pltpu.roll(x, length - 1, 0), 0.0)
        xp2 = jnp.where(t < seg - 2, pltpu.roll(x, length - 2, 0), 0.0)
        y = b_ref[:, pl.ds(c0, LANE)] + (w[0:1] * xm1 + w[1:2] * x + w[2:3] * xp1 + w[3:4] * xp2)
        if act:
            y = _silu(y)
        o_ref[:, pl.ds(c0, LANE)] = y
        return carry

    lax.fori_loop(0, ch // LANE, body, 0)


def _tri_masks(t_len):
    r = lax.broadcasted_iota(jnp.int32, (t_len, t_len), 0)
    c = lax.broadcasted_iota(jnp.int32, (t_len, t_len), 1)
    return r >= c, r <= c


def _mixer_call(body, name, proj, seq_cols, consts, init, st_shape, width, scratch, *, n_seq, length,
                row_off, n_rows_total, y_prev, emit_state, prev_state=None, **static):
    st_nd = len(st_shape)
    sps = static.get('sps')
    seq_dim = None if sps is None else sps
    sps = 1 if sps is None else sps
    assert n_seq % sps == 0 and row_off % sps == 0
    rows = sps * length
    row_off = row_off // sps
    n_steps = n_seq // sps
    n_fill = 0
    if y_prev is None:
        assert row_off == 0 and n_rows_total % rows == 0
        n_fill = n_rows_total // rows - n_steps
    own = lambda b: jnp.minimum(b, n_steps - 1)
    in_specs = [pl.BlockSpec((rows, w), lambda b, cb=cb: (own(b) + row_off, cb)) for w, cb in seq_cols]
    args = [proj] * len(seq_cols)
    for c in consts:
        in_specs.append(pl.BlockSpec(c.shape, lambda b, nd=c.ndim: (0,) * nd))
        args.append(c)
    if init is not None:
        in_specs.append(pl.BlockSpec((seq_dim,) + st_shape, lambda b: (own(b),) + (0,) * st_nd))
        args.append(init)
    n_prev = 0 if prev_state is None else prev_state.shape[1]
    if n_prev:
        in_specs.append(pl.BlockSpec((seq_dim, n_prev) + st_shape, lambda b: (own(b),) + (0,) * (st_nd + 1)))
        args.append(prev_state)
    aliases = {}
    n_alias = 0
    if y_prev is not None:
        aliases[len(args)] = 0
        in_specs.append(pl.BlockSpec(memory_space=pl.ANY))
        args.append(y_prev)
        n_alias += 1
    out_specs = [pl.BlockSpec((rows, width), lambda b: (b + row_off, 0))]
    out_shape = [jax.ShapeDtypeStruct((n_rows_total, width), BF16)]
    if emit_state:
        out_specs.append(pl.BlockSpec((seq_dim, n_prev + 1) + st_shape, lambda b: (own(b),) + (0,) * (st_nd + 1)))
        out_shape.append(jax.ShapeDtypeStruct((n_seq, n_prev + 1) + st_shape, F32))
        static = dict(static, st_layer=n_prev)
    layout = dict(n_seq_in=len(seq_cols), n_const=len(consts), has_init=init is not None, n_alias=n_alias,
                  emit_state=emit_state)
    prev_pos = len(seq_cols) + len(consts) + int(init is not None)

    def kern(*refs):
        step = pl.program_id(0)
        body_refs = refs[:prev_pos] + refs[prev_pos + 1:] if n_prev else refs

        @pl.when(step < n_steps)
        def _():
            if n_prev:
                st_out = _split_refs(body_refs, **layout)[4]
                if seq_dim is None:
                    st_out[0:n_prev] = refs[prev_pos][...]
                else:
                    st_out[:, 0:n_prev] = refs[prev_pos][...]
            body(*body_refs, **layout, **static)

        if n_fill:
            @pl.when(step >= n_steps)
            def _():
                y_ref = _split_refs(body_refs, **layout)[3]
                y_ref[...] = jnp.zeros_like(y_ref)

    return pl.pallas_call(
        kern, grid=(n_steps + n_fill,), in_specs=in_specs, out_specs=tuple(out_specs),
        out_shape=tuple(out_shape), scratch_shapes=scratch, input_output_aliases=aliases,
        compiler_params=_cparams(1), name=name)(*args)


def _split_refs(refs, n_seq_in, n_const, has_init, n_alias, emit_state):
    seq = refs[:n_seq_in]
    consts = refs[n_seq_in:n_seq_in + n_const]
    pos = n_seq_in + n_const
    init = refs[pos] if has_init else None
    pos += int(has_init) + n_alias
    y_ref = refs[pos]
    st_out = refs[pos + 1] if emit_state else None
    pos += 1 + int(emit_state)
    return seq, consts, init, y_ref, st_out, refs[pos:]


def _ssd_kernel(*refs, seg, st_layer=0, **layout):
    (z_ref, xs_ref, bc_ref, small_ref), consts, h0_ref, y_ref, hout_ref, scratch = _split_refs(refs, **layout)
    wx_ref, bx_ref, wbc_ref, bbc_ref, dtb_ref, nega_ref, dvec_ref, ng_ref = consts
    xc_ref, bcc_ref, yacc_ref, stt_ref = scratch
    length = z_ref.shape[0]
    t_len = SSD_T
    n_chunks = length // t_len
    n_pairs = SSD_H // 2
    _conv_cols(xs_ref, wx_ref, bx_ref, xc_ref, seg, True)
    _conv_cols(bc_ref, wbc_ref, bbc_ref, bcc_ref, seg, True)
    if h0_ref is None:
        stt_ref[...] = jnp.zeros_like(stt_ref)
    else:
        for d in range(2):
            for hp in range(n_pairs):
                pair = jnp.concatenate([h0_ref[d, 2 * hp], h0_ref[d, 2 * hp + 1]], axis=0)
                stt_ref[d, :, hp * LANE:(hp + 1) * LANE] = pair.T
    yacc_ref[...] = xc_ref[...] * dvec_ref[...]
    lower, upper = _tri_masks(t_len)
    tri = (lower.astype(BF16), upper.astype(BF16))
    masks = (lower, upper)
    lane_lo = lax.broadcasted_iota(jnp.int32, (t_len, LANE), 1) < SSD_P
    gn = SSD_N

    def chunk(c, d):
        r0 = pl.multiple_of(c * t_len, t_len)
        dtv = _softplus(small_ref[pl.ds(r0, t_len), :] + dtb_ref[d])
        cs = _exact_lhs_dot(tri[d], nega_ref[d] * dtv)
        cst = cs.T
        dtt = dtv.T
        end = t_len - 1 if d == 0 else 0
        bcv = bcc_ref[pl.ds(r0, t_len), :]
        groups = []
        for g in range(SSD_G):
            bg = bcv[:, g * gn:(g + 1) * gn]
            cg = bcv[:, (SSD_G + g) * gn:(SSD_G + g + 1) * gn]
            groups.append((_bdot_nt(cg, bg), bg.T, cg))
        for hp in range(n_pairs):
            gm, bgt, cg = groups[(2 * hp) // (SSD_H // SSD_G)]
            cols = slice(hp * LANE, (hp + 1) * LANE)
            x = xc_ref[pl.ds(r0, t_len), cols]
            st = stt_ref[d, :, cols]
            x_lo = jnp.where(lane_lo, x, 0.0).astype(BF16)
            x_hi = jnp.where(lane_lo, 0.0, x).astype(BF16)
            s_lo = jnp.where(lane_lo, st, 0.0).astype(BF16)
            s_hi = jnp.where(lane_lo, 0.0, st).astype(BF16)
            intra, carry_in, upd, edec = [], [], [], []
            for h in (2 * hp, 2 * hp + 1):
                colx = jnp.broadcast_to(cs[:, h:h + 1], (t_len, t_len))
                row = cst[h:h + 1, :]
                dtr = dtt[h:h + 1, :]
                cend = cst[h:h + 1, end:end + 1]
                intra.append((jnp.where(masks[d], jnp.exp(colx - row), 0.0) * (gm * dtr)).astype(BF16))
                carry_in.append((cg * jnp.exp(colx)).astype(BF16))
                upd.append((bgt * (jnp.exp(cend - row) * dtr)).astype(BF16))
                edec.append(jnp.exp(cend))
            xblk = jnp.concatenate([x_lo, x_hi], axis=0)
            y = jnp.dot(jnp.concatenate(intra + carry_in, axis=1),
                        jnp.concatenate([xblk, s_lo, s_hi], axis=0), preferred_element_type=F32)
            snew = jnp.dot(jnp.concatenate(upd, axis=1), xblk, preferred_element_type=F32)
            stt_ref[d, :, cols] = st * jnp.where(lane_lo[0:1], edec[0], edec[1]) + snew
            yacc_ref[pl.ds(r0, t_len), cols] += y

    def body(j, carry):
        chunk(j, 0)
        chunk(n_chunks - 1 - j, 1)
        return carry

    lax.fori_loop(0, n_chunks, body, 0)
    if hout_ref is not None:
        for d in range(2):
            for hp in range(n_pairs):
                pair = stt_ref[d, :, hp * LANE:(hp + 1) * LANE].T
                hout_ref[st_layer, d, 2 * hp] = pair[:SSD_P]
                hout_ref[st_layer, d, 2 * hp + 1] = pair[SSD_P:]

    def finish(c, carry):
        r0 = pl.multiple_of(c * t_len, t_len)
        y = yacc_ref[pl.ds(r0, t_len), :] * _silu(z_ref[pl.ds(r0, t_len), :])
        y_ref[pl.ds(r0, t_len), :] = (_rms(y) * ng_ref[...]).astype(y_ref.dtype)
        return carry

    lax.fori_loop(0, n_chunks, finish, 0)


def _ssd_call(proj, p, h0, **where):
    length = where['length']
    dtb = jnp.zeros((2, 1, LANE), F32).at[:, 0, :SSD_H].set(p['ssd_dt_bias'].astype(F32))
    nega = jnp.zeros((2, 1, LANE), F32).at[:, 0, :SSD_H].set(-jnp.exp(p['ssd_A_log'].astype(F32)))
    dvec = jnp.repeat(p['ssd_D'].astype(F32), SSD_P).reshape(1, SSD_D)
    cw = p['ssd_conv_w'].astype(F32)
    cb = p['ssd_conv_b'].astype(F32).reshape(1, SSD_CONV_CH)
    consts = [cw[:, :SSD_D], cb[:, :SSD_D], cw[:, SSD_D:], cb[:, SSD_D:], dtb, nega, dvec,
              p['ssd_norm_g'].astype(F32).reshape(1, SSD_D)]
    seq_cols = [(SSD_D, COL_Z // SSD_D), (SSD_D, COL_XS // SSD_D), (512, COL_BC // 512),
                (LANE, COL_SMALL // LANE)]
    scratch = [pltpu.VMEM((length, SSD_D), F32), pltpu.VMEM((length, 512), F32),
               pltpu.VMEM((length, SSD_D), F32), pltpu.VMEM((2, SSD_N, SSD_D), F32)]
    return _mixer_call(_ssd_kernel, "ssd_mixer", proj, seq_cols, consts, h0, (2, SSD_H, SSD_P, SSD_N),
                       SSD_D, scratch, **where)


def _log_sigmoid(x):
    return jnp.minimum(x, 0.0) - jnp.log1p(jnp.exp(-jnp.abs(x)))


def _gla_kernel(*refs, sps, st_layer=0, **layout):
    (q_ref, k_ref, v_ref, g_ref, small_ref), consts, s0_ref, y_ref, sout_ref, scratch = _split_refs(refs, **layout)
    gw_ref, gb_ref, ng_ref = consts
    oacc_ref, st_ref, bc_ref = scratch
    n_rows = q_ref.shape[0]
    length = n_rows // sps
    t_len = GLA_T
    n_chunks = length // t_len
    st_ref[...] = jnp.zeros_like(st_ref)
    if s0_ref is not None:
        for s in range(sps):
            for d in range(2):
                for h in range(GLA_H):
                    st_ref[s, d, h * GLA_DV:(h + 1) * GLA_DV, h * GLA_DK:(h + 1) * GLA_DK] = s0_ref[s, d, h]
    oacc_ref[...] = jnp.zeros_like(oacc_ref)
    lower, upper = _tri_masks(t_len)
    masks = tuple(jnp.concatenate([m] * GLA_H, axis=0) for m in (lower, upper))
    q_head = (lax.broadcasted_iota(jnp.int32, (GLA_H * t_len, GLA_KW), 0) // t_len
              == lax.broadcasted_iota(jnp.int32, (GLA_H * t_len, GLA_KW), 1) // GLA_DK)
    st_diag = (lax.broadcasted_iota(jnp.int32, (GLA_VW, GLA_KW), 0) // GLA_DV
               == lax.broadcasted_iota(jnp.int32, (GLA_VW, GLA_KW), 1) // GLA_DK)
    scale = GLA_DK ** -0.5

    blk = 256
    rr = lax.broadcasted_iota(jnp.int32, (blk, blk), 0)
    cc = lax.broadcasted_iota(jnp.int32, (blk, blk), 1)
    same_chunk = rr // t_len == cc // t_len
    tri_blk = ((same_chunk & (rr >= cc)).astype(BF16), (same_chunk & (rr <= cc)).astype(BF16))
    gate_w = []
    for d in range(2):
        w1, w2 = _split2(gw_ref[d])
        gate_w.append(jnp.concatenate([w1, w1, w2], axis=0))

    def decay_sums(b, carry):
        r0 = pl.multiple_of(b * blk, blk)
        s1, s2 = _split2(small_ref[pl.ds(r0, blk), :])
        lhs = jnp.concatenate([s1, s2, s1], axis=1)
        for d in range(2):
            logit = jnp.dot(lhs, gate_w[d], preferred_element_type=F32) + gb_ref[d]
            la = _log_sigmoid(logit) / GLA_TAU
            bc3 = jnp.dot(tri_blk[d], jnp.concatenate(_split3(la), axis=1), preferred_element_type=F32)
            bc_ref[d, pl.ds(r0, blk), :] = bc3[:, :GLA_KW] + bc3[:, GLA_KW:2 * GLA_KW] + bc3[:, 2 * GLA_KW:]
        return carry

    lax.fori_loop(0, n_rows // blk, decay_sums, 0)

    def chunk(c, d, s):
        r0 = pl.multiple_of(s * length + c * t_len, t_len)
        bc = bc_ref[d, pl.ds(r0, t_len), :]
        end = t_len - 1 if d == 0 else 0
        mid = bc[t_len // 2:t_len // 2 + 1, :]
        tot = bc[end:end + 1, :]
        q = q_ref[pl.ds(r0, t_len), :] * scale
        k = k_ref[pl.ds(r0, t_len), :]
        v = v_ref[pl.ds(r0, t_len), :].astype(BF16)
        qt = q * jnp.exp(bc - mid)
        kt = k * jnp.exp(mid - bc)
        qs = q * jnp.exp(bc)
        kd = k * jnp.exp(tot - bc)
        q_rows = jnp.where(q_head, jnp.concatenate([qt] * GLA_H, axis=0), 0.0)
        att = jnp.where(masks[d], _bdot_nt(q_rows, kt), 0.0)
        o_all = jnp.dot(att.astype(BF16), v, preferred_element_type=F32)
        o = jnp.concatenate([o_all[h * t_len:(h + 1) * t_len, h * GLA_DV:(h + 1) * GLA_DV]
                             for h in range(GLA_H)], axis=1)
        st = st_ref[s, d]
        o = o + _bdot_nt(qs, st)
        upd = lax.dot_general(v, kd.astype(BF16), (((0,), (0,)), ((), ())), preferred_element_type=F32)
        st_ref[s, d] = st * jnp.exp(tot) + jnp.where(st_diag, upd, 0.0)
        oacc_ref[pl.ds(r0, t_len), :] += o

    def body(j, carry):
        for s in range(sps):
            chunk(j, 0, s)
            chunk(n_chunks - 1 - j, 1, s)
        return carry

    lax.fori_loop(0, n_chunks, body, 0)
    if sout_ref is not None:
        for s in range(sps):
            for d in range(2):
                for h in range(GLA_H):
                    sout_ref[s, st_layer, d, h] = st_ref[s, d, h * GLA_DV:(h + 1) * GLA_DV,
                                                         h * GLA_DK:(h + 1) * GLA_DK]

    def finish(c, carry):
        r0 = pl.multiple_of(c * t_len, t_len)
        gate = _silu(g_ref[pl.ds(r0, t_len), :])
        for h in range(GLA_H):
            vs = slice(h * GLA_DV, (h + 1) * GLA_DV)
            o = _rms(oacc_ref[pl.ds(r0, t_len), vs]) * ng_ref[...]
            y_ref[pl.ds(r0, t_len), vs] = (o * gate[:, vs]).astype(y_ref.dtype)
        return carry

    lax.fori_loop(0, sps * n_chunks, finish, 0)


def _gla_call(proj, p, s0t, **where):
    length = where['length']
    sps = where['sps']
    gw = jnp.zeros((2, LANE, GLA_KW), F32).at[:, GLA_RANK:2 * GLA_RANK, :].set(p['gla_gate_w'].astype(F32))
    consts = [gw, p['gla_gate_b'].astype(F32).reshape(2, 1, GLA_KW),
              p['gla_norm_g'].astype(F32).reshape(1, GLA_DV)]
    seq_cols = [(GLA_KW, COL_Q // GLA_KW), (GLA_KW, COL_K // GLA_KW), (GLA_VW, COL_V // GLA_VW),
                (GLA_VW, COL_G // GLA_VW), (LANE, COL_SMALL // LANE)]
    st_shape = (2, GLA_H, GLA_DV, GLA_DK)
    scratch = [pltpu.VMEM((sps * length, GLA_VW), F32), pltpu.VMEM((sps, 2, GLA_VW, GLA_KW), F32),
               pltpu.VMEM((2, sps * length, GLA_KW), F32)]
    return _mixer_call(_gla_kernel, "gla_mixer", proj, seq_cols, consts, s0t, st_shape, GLA_VW, scratch, **where)


def _lru_kernel(*refs, seg, sps, st_layer=0, **layout):
    (xb_ref, gb_ref), consts, h0_ref, y_ref, hout_ref, scratch = _split_refs(refs, **layout)
    cw_ref, cb_ref, w_ref, bias_ref, sp_ref = consts
    xr_ref, a_ref, u_ref = scratch
    n_rows = xb_ref.shape[0]
    length = n_rows // sps
    blk = 256
    _conv_cols(xb_ref, cw_ref, cb_ref, xr_ref, seg, False)

    def gates(c, carry):
        r0 = pl.multiple_of(c * blk, blk)
        xr = xr_ref[pl.ds(r0, blk), :]
        xbf = xr.astype(BF16)
        for d in range(2):
            pre = jnp.dot(xbf, w_ref[d], preferred_element_type=F32) + bias_ref[d]
            r = jax.nn.sigmoid(pre[:, :LRU_W])
            i = jax.nn.sigmoid(pre[:, LRU_W:])
            log_a = (-LRU_C) * r * sp_ref[d]
            a = jnp.exp(log_a)
            a_ref[d, pl.ds(r0, blk), :] = a
            u_ref[d, pl.ds(r0, blk), :] = jnp.sqrt(-jnp.tanh(log_a) * (a * a + 1.0)) * (i * xr)
        return carry

    lax.fori_loop(0, n_rows // blk, gates, 0)

    def scan(t, carry):
        out = []
        for s in range(sps):
            for d in range(2):
                row = s * length + (t if d == 0 else length - 1 - t)
                h = a_ref[d, pl.ds(row, 1), :] * carry[2 * s + d] + u_ref[d, pl.ds(row, 1), :]
                u_ref[d, pl.ds(row, 1), :] = h
                out.append(h)
        return tuple(out)

    if h0_ref is None:
        start = tuple(jnp.zeros((1, LRU_W), F32) for _ in range(2 * sps))
    else:
        start = tuple(h0_ref[s, d] for s in range(sps) for d in range(2))
    final = lax.fori_loop(0, length, scan, start)
    if hout_ref is not None:
        for s in range(sps):
            for d in range(2):
                hout_ref[s, st_layer, d] = final[2 * s + d]

    def finish(c, carry):
        r0 = pl.multiple_of(c * blk, blk)
        y = (u_ref[0, pl.ds(r0, blk), :] + u_ref[1, pl.ds(r0, blk), :]) * jax.nn.gelu(gb_ref[pl.ds(r0, blk), :])
        y_ref[pl.ds(r0, blk), :] = y.astype(y_ref.dtype)
        return carry

    lax.fori_loop(0, n_rows // blk, finish, 0)


def _block_diag(w):
    nb, bw, _ = w.shape
    eye = jnp.eye(nb, dtype=w.dtype)
    return (eye[:, None, :, None] * w[:, :, None, :]).reshape(nb * bw, nb * bw)


def _lru_call(proj, p, h0, **where):
    length = where['length']
    w = jnp.stack([jnp.concatenate([_block_diag(p['lru_wa'][d].astype(F32)),
                                    _block_diag(p['lru_wx'][d].astype(F32))], axis=1) for d in range(2)])
    bias = jnp.stack([jnp.concatenate([p['lru_ba'][d], p['lru_bx'][d]]) for d in range(2)]).astype(F32)
    sp = jax.nn.softplus(-p['lru_lambda'].astype(F32)).reshape(2, 1, LRU_W)
    consts = [p['lru_conv_w'].astype(F32), p['lru_conv_b'].astype(F32).reshape(1, LRU_W), w.astype(BF16),
              bias.reshape(2, 1, 2 * LRU_W), sp]
    seq_cols = [(LRU_W, COL_XB // LRU_W), (LRU_W, COL_GB // LRU_W)]
    rows = where['sps'] * length
    scratch = [pltpu.VMEM((rows, LRU_W), F32), pltpu.VMEM((2, rows, LRU_W), F32),
               pltpu.VMEM((2, rows, LRU_W), F32)]
    return _mixer_call(_lru_kernel, "lru_mixer", proj, seq_cols, consts, h0, (2, 1, LRU_W), LRU_W, scratch,
                       **where)


def kernel(x_prompt, x_sample, state_ssd, state_gla, state_lru, c, c_ctx, mod_w, mod_b, norm1_g, norm2_g, in_w, ssd_conv_w, ssd_conv_b, ssd_A_log, ssd_dt_bias, ssd_D, ssd_norm_g, gla_gate_w, gla_gate_b, gla_norm_g, lru_conv_w, lru_conv_b, lru_wa, lru_ba, lru_wx, lru_bx, lru_lambda, out_w, ffn_w1, ffn_w3, ffn_w2, moe_router, moe_w1, moe_w3, moe_w2, final_norm_g):
    bp, lp, d = x_prompt.shape
    bs, ls, _ = x_sample.shape
    n_p = bp * lp
    n_s = bs * ls
    n_all = n_p + n_s
    depth = in_w.shape[0]
    assert n_p % ls == 0 and 1 + bs <= MOD_ROWS

    cvec = jnp.zeros((MOD_ROWS, d), F32).at[0].set(c_ctx.astype(F32)).at[1:1 + bs].set(c.astype(F32))
    mods = _adaln(cvec, mod_w, mod_b)
    mods3 = mods.reshape(depth * MOD_ROWS * 6, 1, d)
    x = (x_prompt.reshape(n_p, d).astype(F32), x_sample.reshape(n_s, d).astype(F32))

    in_w_t = jnp.swapaxes(in_w, 1, 2)
    st_ssd = st_gla = st_lru = None
    for i in range(depth):
        p = {'ssd_conv_w': ssd_conv_w[i], 'ssd_conv_b': ssd_conv_b[i], 'ssd_A_log': ssd_A_log[i],
             'ssd_dt_bias': ssd_dt_bias[i], 'ssd_D': ssd_D[i], 'ssd_norm_g': ssd_norm_g[i],
             'gla_gate_w': gla_gate_w[i], 'gla_gate_b': gla_gate_b[i], 'gla_norm_g': gla_norm_g[i],
             'lru_conv_w': lru_conv_w[i], 'lru_conv_b': lru_conv_b[i], 'lru_wa': lru_wa[i],
             'lru_ba': lru_ba[i], 'lru_wx': lru_wx[i], 'lru_bx': lru_bx[i], 'lru_lambda': lru_lambda[i]}
        h = _norm_mod(x, norm1_g[i], mods3, i, 0, n_p, ls)
        proj = _in_proj(h, in_w_t, i)
        ctx = dict(n_seq=bp, length=lp, row_off=0, n_rows_total=n_all, y_prev=None, emit_state=True)
        lat = dict(n_seq=bs, length=ls, row_off=n_p // ls, n_rows_total=n_all, emit_state=False)
        y_ssd, st_ssd = _ssd_call(proj, p, None, seg=lp, prev_state=st_ssd, **ctx)
        y_ssd, = _ssd_call(proj, p, state_ssd[:, i].astype(F32), seg=GRID_W, y_prev=y_ssd, **lat)
        y_gla, st_gla = _gla_call(proj, p, None, sps=MIX_SPS, prev_state=st_gla, **ctx)
        y_gla, = _gla_call(proj, p, jnp.swapaxes(state_gla[:, i].astype(F32), -1, -2), sps=MIX_SPS,
                           y_prev=y_gla, **lat)
        y_lru, st_lru = _lru_call(proj, p, None, seg=lp, sps=MIX_SPS, prev_state=st_lru, **ctx)
        y_lru, = _lru_call(proj, p, state_lru[:, i].astype(F32).reshape(bs, 2, 1, LRU_W), seg=GRID_W,
                           sps=MIX_SPS, y_prev=y_lru, **lat)
        x = _out_proj(y_ssd, y_gla, y_lru, out_w, x, mods3, i, n_p, ls)
        j = i // 2
        if i % 2 == 1:
            h2, route = _norm_mod(x, norm2_g[i], mods3, i, 1, n_p, ls, router=moe_router[j])
            x = _moe_ffn(x, h2, route, moe_w1[j], moe_w3[j], moe_w2[j], mods3, i, n_p, ls,
                         final_g=final_norm_g if i == depth - 1 else None)
        else:
            h2 = _norm_mod(x, norm2_g[i], mods3, i, 1, n_p, ls)
            x = _ffn_down(_ffn_up(h2, ffn_w1[j], ffn_w3[j]), ffn_w2[j], x, mods3, i, n_p, ls)
    if isinstance(x, tuple):
        y_p, y_s = x
    else:
        y_p = _final_norm(x, final_norm_g, 0, n_p)
        y_s = _final_norm(x, final_norm_g, n_p, n_s)
    return (y_p.reshape(bp, lp, d), y_s.reshape(bs, ls, d), st_ssd, jnp.swapaxes(st_gla, -1, -2),
            st_lru.reshape(bp, depth, 2, LRU_W))
```

```python
import functools

import jax
import jax.numpy as jnp
from jax import lax
from jax.experimental import pallas as pl
from jax.experimental.pallas import tpu as pltpu

F32 = jnp.float32
BF16 = jnp.bfloat16

D_MODEL = 2048
GRID_W = 64
SSD_D = D_MODEL // 2
SSD_P = 64
SSD_H = SSD_D // SSD_P
SSD_G = 2
SSD_N = 128
GLA_H = 4
GLA_VW = D_MODEL // 4
GLA_DV = GLA_VW // GLA_H
GLA_DK = GLA_DV // 2
GLA_KW = GLA_H * GLA_DK
GLA_RANK = 16
GLA_TAU = 16.0
LRU_W = D_MODEL // 4
LRU_NB = 8
LRU_BW = LRU_W // LRU_NB
LRU_C = 8.0
SSD_CONV_CH = SSD_D + 2 * SSD_G * SSD_N
N_EXPERTS = 8
TOP_K = 2
EPS = 1e-6

LANE = 128
SSD_T = 128
GLA_T = 64
MIX_SPS = 2
MOD_ROWS = 8
VMEM_LIMIT = 56 * 1024 * 1024

COL_Z = 0
COL_XS = 1024
COL_BC = 2048
COL_Q = 2560
COL_K = 2816
COL_V = 3072
COL_G = 3584
COL_XB = 4096
COL_GB = 4608
COL_SMALL = 5120
IN_PAD = 5632


def _cparams(n_axes):
    return pltpu.CompilerParams(dimension_semantics=("arbitrary",) * n_axes,
                                vmem_limit_bytes=VMEM_LIMIT)


def _bdot(a, b):
    return jnp.dot(a.astype(BF16), b.astype(BF16), preferred_element_type=F32)


def _bdot_nt(a, b):
    return lax.dot_general(a.astype(BF16), b.astype(BF16), (((1,), (1,)), ((), ())),
                           preferred_element_type=F32)


def _bdot_tn(a, b):
    return lax.dot_general(a.astype(BF16), b.astype(BF16), (((0,), (0,)), ((), ())),
                           preferred_element_type=F32)


def _split2(a):
    a1 = a.astype(BF16)
    a2 = (a - a1.astype(F32)).astype(BF16)
    return a1, a2


def _split3(a):
    a1 = a.astype(BF16)
    r = a - a1.astype(F32)
    a2 = r.astype(BF16)
    a3 = (r - a2.astype(F32)).astype(BF16)
    return a1, a2, a3


def _exact_lhs_dot(m_bf16, a):
    a1, a2, a3 = _split3(a)
    f = lambda z: jnp.dot(m_bf16, z, preferred_element_type=F32)
    return f(a1) + f(a2) + f(a3)


def _dot3(a, b):
    a1, a2 = _split2(a)
    b1, b2 = _split2(b)
    f = lambda x, y: jnp.dot(x, y, preferred_element_type=F32)
    return f(a1, b1) + (f(a1, b2) + f(a2, b1))


def _softplus(x):
    return jnp.maximum(x, 0.0) + jnp.log1p(jnp.exp(-jnp.abs(x)))


def _silu(x):
    return x * jax.nn.sigmoid(x)


def _mod_row(m, tm, n_prompt_rows, dec_seq):
    r0 = m * tm
    return jnp.where(r0 < n_prompt_rows, 0, 1 + (r0 - n_prompt_rows) // dec_seq)


def _adaln_kernel(c_ref, w_ref, b_ref, o_ref):
    s = _silu(c_ref[...])
    o_ref[...] = _dot3(s, w_ref[...]) + b_ref[...]


def _adaln(cvec8, mod_w, mod_b):
    depth, d, n = mod_w.shape
    tn = 1024
    return pl.pallas_call(
        _adaln_kernel,
        grid=(depth, n // tn),
        in_specs=[pl.BlockSpec((MOD_ROWS, d), lambda i, j: (0, 0)),
                  pl.BlockSpec((None, d, tn), lambda i, j: (i, 0, j)),
                  pl.BlockSpec((None, 1, tn), lambda i, j: (i, 0, j))],
        out_specs=pl.BlockSpec((None, MOD_ROWS, tn), lambda i, j: (i, 0, j)),
        out_shape=jax.ShapeDtypeStruct((depth, MOD_ROWS, n), F32),
        compiler_params=_cparams(2),
        name="adaln",
    )(cvec8, mod_w, mod_b.reshape(depth, 1, n))


def _rms(x):
    return x * lax.rsqrt(jnp.mean(x * x, axis=-1, keepdims=True) + EPS)


def _stacked_specs(x, block, row_tile, col_block):
    if not isinstance(x, tuple):
        return [pl.BlockSpec(block, lambda *g: (row_tile(*g), col_block(*g)))], [x], 0
    split = x[0].shape[0] // block[0]
    specs = [pl.BlockSpec(block, lambda *g: (jnp.minimum(row_tile(*g), split - 1), col_block(*g))),
             pl.BlockSpec(block, lambda *g: (jnp.maximum(row_tile(*g) - split, 0), col_block(*g)))]
    return specs, list(x), split


def _stacked_tile(x_refs, m, split):
    if len(x_refs) == 1:
        return x_refs[0][...]
    return jnp.where(m < split, x_refs[0][...], x_refs[1][...])


def _norm_mod_kernel(*refs, split):
    g_ref, sh_ref, sc_ref, o_ref = refs[-4:]
    y = _rms(_stacked_tile(refs[:-4], pl.program_id(0), split)) * g_ref[...]
    o_ref[...] = (y * (1.0 + sc_ref[...]) + sh_ref[...]).astype(o_ref.dtype)


def _norm_mod_route_kernel(x_ref, g_ref, sh_ref, sc_ref, r_ref, o_ref, route_ref):
    y = _rms(x_ref[...]) * g_ref[...]
    h = y * (1.0 + sc_ref[...]) + sh_ref[...]
    o_ref[...] = h.astype(o_ref.dtype)
    logits = _dot3(h, r_ref[...])
    lane = lax.broadcasted_iota(jnp.int32, logits.shape, 1)
    neg = jnp.float32(-jnp.inf)
    lg = jnp.where(lane < N_EXPERTS, logits, neg)
    m1 = jnp.max(lg, axis=-1, keepdims=True)
    i1 = jnp.min(jnp.where(lg == m1, lane, LANE), axis=-1, keepdims=True)
    lg2 = jnp.where(lane == i1, neg, lg)
    m2 = jnp.max(lg2, axis=-1, keepdims=True)
    i2 = jnp.min(jnp.where(lg2 == m2, lane, LANE), axis=-1, keepdims=True)
    e2 = jnp.exp(m2 - m1)
    den = 1.0 + e2
    g1 = 1.0 / den
    g2 = e2 / den
    out = jnp.where(lane == 0, i1.astype(F32),
                    jnp.where(lane == 1, i2.astype(F32),
                              jnp.where(lane == 2, g1, jnp.where(lane == 3, g2, 0.0))))
    route_ref[...] = out


def _norm_mod(x, g, mods3, layer, which, n_prompt_rows, dec_seq, router=None):
    d = g.shape[0]
    m_rows = sum(part.shape[0] for part in x) if isinstance(x, tuple) else x.shape[0]
    tm = 512
    base = layer * MOD_ROWS * 6

    def mod_spec(k):
        return pl.BlockSpec((None, 1, d),
                            lambda m: (base + _mod_row(m, tm, n_prompt_rows, dec_seq) * 6 + k, 0, 0))

    x_specs, x_args, split = _stacked_specs(x, (tm, d), lambda m: m, lambda m: 0)
    in_specs = x_specs + [pl.BlockSpec((1, d), lambda m: (0, 0)), mod_spec(3 * which), mod_spec(3 * which + 1)]
    args = x_args + [g.reshape(1, d), mods3, mods3]
    h_spec = pl.BlockSpec((tm, d), lambda m: (m, 0))
    if router is None:
        return pl.pallas_call(
            functools.partial(_norm_mod_kernel, split=split), grid=(m_rows // tm,), in_specs=in_specs,
            out_specs=h_spec, out_shape=jax.ShapeDtypeStruct((m_rows, d), BF16),
            compiler_params=_cparams(1), name="norm_mod")(*args)
    assert not isinstance(x, tuple)
    r_pad = jnp.zeros((d, LANE), F32).at[:, :N_EXPERTS].set(router.astype(F32))
    return pl.pallas_call(
        _norm_mod_route_kernel, grid=(m_rows // tm,),
        in_specs=in_specs + [pl.BlockSpec((d, LANE), lambda m: (0, 0))],
        out_specs=(h_spec, pl.BlockSpec((tm, LANE), lambda m: (m, 0))),
        out_shape=(jax.ShapeDtypeStruct((m_rows, d), F32), jax.ShapeDtypeStruct((m_rows, LANE), F32)),
        compiler_params=_cparams(1), name="norm_mod_route")(*args, r_pad)


def _final_norm_kernel(x_ref, g_ref, o_ref):
    o_ref[...] = _rms(x_ref[...]) * g_ref[...]


def _final_norm(x, g, row0, n_rows):
    d = x.shape[1]
    tm = 512
    off = row0 // tm
    return pl.pallas_call(
        _final_norm_kernel, grid=(n_rows // tm,),
        in_specs=[pl.BlockSpec((tm, d), lambda m: (m + off, 0)), pl.BlockSpec((1, d), lambda m: (0, 0))],
        out_specs=pl.BlockSpec((tm, d), lambda m: (m, 0)),
        out_shape=jax.ShapeDtypeStruct((n_rows, d), F32),
        compiler_params=_cparams(1), name="final_norm")(x, g.reshape(1, d))


IN_TN = 512
IN_DT_COL = SSD_D + SSD_CONV_CH
IN_GRAW_COL = IN_DT_COL + SSD_H + 2 * GLA_KW + 2 * GLA_VW


def _in_proj_kernel(x_ref, wm_ref, wn_ref, wdt_ref, wgr_ref, o_ref, wbf_ref):
    j = pl.program_id(0)

    @pl.when(pl.program_id(1) == 0)
    def _():
        def shifted(s):
            w = jnp.concatenate([wm_ref[...], wn_ref[...]], axis=0)
            wbf_ref[...] = w[s:s + IN_TN].astype(BF16)

        @pl.when(j < COL_Q // IN_TN)
        def _():
            wbf_ref[...] = wm_ref[...].astype(BF16)

        @pl.when((j >= COL_Q // IN_TN) & (j < COL_XB // IN_TN))
        def _():
            shifted(SSD_H)

        @pl.when((j >= COL_XB // IN_TN) & (j < COL_SMALL // IN_TN))
        def _():
            shifted(SSD_H + GLA_RANK)

        @pl.when(j == COL_SMALL // IN_TN)
        def _():
            wbf_ref[...] = jnp.zeros_like(wbf_ref)
            wbf_ref[0:SSD_H] = wdt_ref[...].astype(BF16)
            wbf_ref[SSD_H:SSD_H + GLA_RANK] = wgr_ref[...].astype(BF16)

    nt = (((1,), (1,)), ((), ()))

    @pl.when(j < COL_SMALL // IN_TN)
    def _():
        o_ref[...] = lax.dot_general(x_ref[...], wbf_ref[...], nt, preferred_element_type=F32)

    @pl.when(j == COL_SMALL // IN_TN)
    def _():
        o_ref[:, :LANE] = lax.dot_general(x_ref[...], wbf_ref[0:LANE], nt, preferred_element_type=F32)
        o_ref[:, LANE:] = jnp.zeros((o_ref.shape[0], o_ref.shape[1] - LANE), F32)


def _in_proj(h, in_w_t, layer):
    m_rows, k = h.shape
    tm, tn = 2048, IN_TN
    n_main = COL_SMALL // tn
    shift_max = SSD_H + GLA_RANK
    last_next_block = in_w_t.shape[1] // shift_max - 1
    return pl.pallas_call(
        _in_proj_kernel, grid=(IN_PAD // tn, m_rows // tm),
        in_specs=[pl.BlockSpec((tm, k), lambda j, m: (m, 0)),
                  pl.BlockSpec((None, tn, k), lambda j, m: (layer, jnp.minimum(j, n_main - 1), 0)),
                  pl.BlockSpec((None, shift_max, k),
                               lambda j, m: (layer, jnp.minimum((j + 1) * (tn // shift_max), last_next_block), 0)),
                  pl.BlockSpec((None, SSD_H, k), lambda j, m: (layer, IN_DT_COL // SSD_H, 0)),
                  pl.BlockSpec((None, GLA_RANK, k), lambda j, m: (layer, IN_GRAW_COL // GLA_RANK, 0))],
        out_specs=pl.BlockSpec((tm, tn), lambda j, m: (m, j)),
        out_shape=jax.ShapeDtypeStruct((m_rows, IN_PAD), F32),
        scratch_shapes=[pltpu.VMEM((tn, k), BF16)],
        compiler_params=_cparams(2), name="in_proj")(h, in_w_t, in_w_t, in_w_t, in_w_t)


def _out_proj_kernel(y1_ref, y2_ref, y3_ref, w_ref, *rest, split):
    gate_ref, o_ref, wbf_ref = rest[-3:]

    @pl.when(pl.program_id(1) == 0)
    def _():
        wbf_ref[...] = w_ref[...].astype(BF16)

    k1 = y1_ref.shape[1]
    k2 = k1 + y2_ref.shape[1]
    acc = jnp.dot(y1_ref[...], wbf_ref[0:k1, :], preferred_element_type=F32)
    acc += jnp.dot(y2_ref[...], wbf_ref[k1:k2, :], preferred_element_type=F32)
    acc += jnp.dot(y3_ref[...], wbf_ref[k2:, :], preferred_element_type=F32)
    o_ref[...] = _stacked_tile(rest[:-3], pl.program_id(1), split) + gate_ref[...] * acc


def _out_proj(y1, y2, y3, w, res, mods3, layer, n_prompt_rows, dec_seq):
    m_rows = y1.shape[0]
    k, d = w.shape[1:]
    tm = 1024
    tn = 512 if isinstance(res, tuple) else 1024
    base = layer * MOD_ROWS * 6
    res_specs, res_args, split = _stacked_specs(res, (tm, tn), lambda j, m: m, lambda j, m: j)
    return pl.pallas_call(
        functools.partial(_out_proj_kernel, split=split), grid=(d // tn, m_rows // tm),
        in_specs=[pl.BlockSpec((tm, y1.shape[1]), lambda j, m: (m, 0)),
                  pl.BlockSpec((tm, y2.shape[1]), lambda j, m: (m, 0)),
                  pl.BlockSpec((tm, y3.shape[1]), lambda j, m: (m, 0)),
                  pl.BlockSpec((None, k, tn), lambda j, m: (layer, 0, j))] + res_specs + [
                  pl.BlockSpec((None, 1, tn),
                               lambda j, m: (base + _mod_row(m, tm, n_prompt_rows, dec_seq) * 6 + 2, 0, j))],
        out_specs=pl.BlockSpec((tm, tn), lambda j, m: (m, j)),
        out_shape=jax.ShapeDtypeStruct((m_rows, d), F32),
        scratch_shapes=[pltpu.VMEM((k, tn), BF16)],
        compiler_params=_cparams(2), name="out_proj")(y1, y2, y3, w, *res_args, mods3)


def _ffn_up_kernel(x_ref, w1_ref, w3_ref, o_ref, w1bf_ref, w3bf_ref):
    @pl.when(pl.program_id(1) == 0)
    def _():
        w1bf_ref[...] = w1_ref[...].astype(BF16)
        w3bf_ref[...] = w3_ref[...].astype(BF16)

    a = jnp.dot(x_ref[...], w1bf_ref[...], preferred_element_type=F32)
    b = jnp.dot(x_ref[...], w3bf_ref[...], preferred_element_type=F32)
    o_ref[...] = (_silu(a) * b).astype(o_ref.dtype)


def _ffn_up(h, w1, w3):
    m_rows, d = h.shape
    f = w1.shape[1]
    tm, tf = 2048, 512
    return pl.pallas_call(
        _ffn_up_kernel, grid=(pl.cdiv(f, tf), m_rows // tm),
        in_specs=[pl.BlockSpec((tm, d), lambda j, m: (m, 0)),
                  pl.BlockSpec((d, tf), lambda j, m: (0, j)),
                  pl.BlockSpec((d, tf), lambda j, m: (0, j))],
        out_specs=pl.BlockSpec((tm, tf), lambda j, m: (m, j)),
        out_shape=jax.ShapeDtypeStruct((m_rows, f), BF16),
        scratch_shapes=[pltpu.VMEM((d, tf), BF16), pltpu.VMEM((d, tf), BF16)],
        compiler_params=_cparams(2), name="ffn_up")(h, w1, w3)


def _ffn_down_kernel(g_ref, w_ref, res_ref, gate_ref, o_ref, wbf_ref):
    @pl.when(pl.program_id(1) == 0)
    def _():
        wbf_ref[...] = w_ref[...].astype(BF16)

    acc = jnp.dot(g_ref[...], wbf_ref[...], preferred_element_type=F32)
    o_ref[...] = res_ref[...] + gate_ref[...] * acc


def _ffn_down(g, w2, res, mods3, layer, n_prompt_rows, dec_seq):
    m_rows, d = res.shape
    f = w2.shape[0]
    tm, tn = 512, 512
    base = layer * MOD_ROWS * 6
    return pl.pallas_call(
        _ffn_down_kernel, grid=(d // tn, m_rows // tm),
        in_specs=[pl.BlockSpec((tm, f), lambda j, m: (m, 0)),
                  pl.BlockSpec((f, tn), lambda j, m: (0, j)),
                  pl.BlockSpec((tm, tn), lambda j, m: (m, j)),
                  pl.BlockSpec((None, 1, tn),
                               lambda j, m: (base + _mod_row(m, tm, n_prompt_rows, dec_seq) * 6 + 5, 0, j))],
        out_specs=pl.BlockSpec((tm, tn), lambda j, m: (m, j)),
        out_shape=jax.ShapeDtypeStruct((m_rows, d), F32),
        scratch_shapes=[pltpu.VMEM((f, tn), BF16)],
        compiler_params=_cparams(2), name="ffn_down")(g, w2, res, mods3)


MOE_TM = 256
MOE_ROW_QUARTERS = 4


def _grouped_weights(plan_refs, w_hbm, stages, casts, sems):
    te_ref, first_ref, next_e_ref, last_ref = plan_refs
    j = pl.program_id(0)
    m = pl.program_id(1)
    n_col_tiles = pl.num_programs(0)

    def fetch(e, jj, k):
        width = stages[k].shape[1]
        src = w_hbm[k].at[e, :, pl.ds(pl.multiple_of(jj * width, width), width)]
        return pltpu.make_async_copy(src, stages[k], sems.at[k])

    @pl.when((j == 0) & (m == 0))
    def _():
        for k in range(len(stages)):
            fetch(te_ref[0], 0, k).start()

    @pl.when(first_ref[m] == 1)
    def _():
        for k in range(len(stages)):
            fetch(te_ref[m], j, k).wait()
            casts[k][...] = stages[k][...].astype(BF16)
        next_j = j + last_ref[m]

        @pl.when(next_j < n_col_tiles)
        def _():
            for k in range(len(stages)):
                fetch(next_e_ref[m], next_j, k).start()


def _by_live_rows(live, o_ref, compute):
    rows = o_ref.shape[0]
    step = rows // MOE_ROW_QUARTERS
    for q in range(1, MOE_ROW_QUARTERS + 1):
        top = q * step

        @pl.when((live > top - step) & (live <= top))
        def _(top=top):
            o_ref[0:top] = compute(slice(0, top))
            if top < rows:
                o_ref[top:] = jnp.zeros((rows - top, o_ref.shape[1]), o_ref.dtype)

    @pl.when(live == 0)
    def _():
        o_ref[...] = jnp.zeros_like(o_ref)


def _moe_up_kernel(te_ref, first_ref, next_e_ref, last_ref, live_ref, x_ref, w1_hbm, w3_hbm, o_ref,
                   w1st_ref, w3st_ref, w1bf_ref, w3bf_ref, sems):
    _grouped_weights((te_ref, first_ref, next_e_ref, last_ref), (w1_hbm, w3_hbm), (w1st_ref, w3st_ref),
                     (w1bf_ref, w3bf_ref), sems)

    def compute(rows):
        x = x_ref[rows].astype(BF16)
        a = jnp.dot(x, w1bf_ref[...], preferred_element_type=F32)
        b = jnp.dot(x, w3bf_ref[...], preferred_element_type=F32)
        return (_silu(a) * b).astype(o_ref.dtype)

    _by_live_rows(live_ref[pl.program_id(1)], o_ref, compute)


def _moe_up(xs, w1, w3, plan):
    n_rows, d = xs.shape
    f = w1.shape[2]
    tm, tf = MOE_TM, 1792
    n_plan = len(plan)
    grid_spec = pltpu.PrefetchScalarGridSpec(
        num_scalar_prefetch=n_plan, grid=(f // tf, n_rows // tm),
        in_specs=[pl.BlockSpec((tm, d), lambda j, m, *_: (m, 0)),
                  pl.BlockSpec(memory_space=pl.ANY), pl.BlockSpec(memory_space=pl.ANY)],
        out_specs=pl.BlockSpec((tm, tf), lambda j, m, *_: (m, j)),
        scratch_shapes=[pltpu.VMEM((d, tf), F32), pltpu.VMEM((d, tf), F32),
                        pltpu.VMEM((d, tf), BF16), pltpu.VMEM((d, tf), BF16),
                        pltpu.SemaphoreType.DMA((2,))])
    return pl.pallas_call(
        _moe_up_kernel, grid_spec=grid_spec,
        out_shape=jax.ShapeDtypeStruct((n_rows, f), BF16),
        compiler_params=_cparams(2), name="moe_up")(*plan, xs, w1, w3)


def _moe_down_kernel(te_ref, first_ref, next_e_ref, last_ref, live_ref, g_ref, w_hbm, o_ref,
                     wst_ref, wbf_ref, sems):
    _grouped_weights((te_ref, first_ref, next_e_ref, last_ref), (w_hbm,), (wst_ref,), (wbf_ref,), sems)

    def compute(rows):
        return jnp.dot(g_ref[rows], wbf_ref[...], preferred_element_type=F32)

    _by_live_rows(live_ref[pl.program_id(1)], o_ref, compute)


def _moe_down(g, w2, plan):
    n_rows, f = g.shape
    d = w2.shape[2]
    tm, tn = MOE_TM, 1024
    grid_spec = pltpu.PrefetchScalarGridSpec(
        num_scalar_prefetch=len(plan), grid=(d // tn, n_rows // tm),
        in_specs=[pl.BlockSpec((tm, f), lambda j, m, *_: (m, 0)),
                  pl.BlockSpec(memory_space=pl.ANY)],
        out_specs=pl.BlockSpec((tm, tn), lambda j, m, *_: (m, j)),
        scratch_shapes=[pltpu.VMEM((f, tn), F32), pltpu.VMEM((f, tn), BF16),
                        pltpu.SemaphoreType.DMA((1,))])
    return pl.pallas_call(
        _moe_down_kernel, grid_spec=grid_spec,
        out_shape=jax.ShapeDtypeStruct((n_rows, d), F32),
        compiler_params=_cparams(2), name="moe_down")(*plan, g, w2)


def _combine_kernel(x_ref, ya_ref, yb_ref, route_ref, gate_ref, *rest):
    ga = route_ref[:, TOP_K:TOP_K + 1]
    gb = route_ref[:, TOP_K + 1:TOP_K + 2]
    v = x_ref[...] + gate_ref[...] * (ga * ya_ref[...] + gb * yb_ref[...])
    if len(rest) == 2:
        g_ref, o_ref = rest
        o_ref[...] = _rms(v) * g_ref[...]
    else:
        rest[0][...] = v


def _moe_combine(x, ya, yb, route, mods3, layer, n_prompt_rows, dec_seq, row0, n_rows, final_g):
    d = x.shape[1]
    tm = 512
    off = row0 // tm
    base = layer * MOD_ROWS * 6
    row = pl.BlockSpec((tm, d), lambda m: (m + off, 0))
    in_specs = [row, row, row, pl.BlockSpec((tm, LANE), lambda m: (m + off, 0)),
                pl.BlockSpec((None, 1, d),
                             lambda m: (base + _mod_row(m + off, tm, n_prompt_rows, dec_seq) * 6 + 5, 0, 0))]
    args = [x, ya, yb, route, mods3]
    if final_g is not None:
        in_specs.append(pl.BlockSpec((1, d), lambda m: (0, 0)))
        args.append(final_g.reshape(1, d))
    return pl.pallas_call(
        _combine_kernel, grid=(n_rows // tm,), in_specs=in_specs,
        out_specs=pl.BlockSpec((tm, d), lambda m: (m, 0)),
        out_shape=jax.ShapeDtypeStruct((n_rows, d), F32),
        compiler_params=_cparams(1), name="moe_combine")(*args)


def _take_rows(a, idx):
    return a.at[idx].get(mode="promise_in_bounds")


def _moe_ffn(x, h, route, w1, w3, w2, mods3, layer, n_prompt_rows, dec_seq, final_g=None):
    m_rows, d = x.shape
    tm = MOE_TM
    n_slots = m_rows * TOP_K
    n_rows = n_slots + N_EXPERTS * tm
    n_tiles = n_rows // tm
    top_i = route[:, 0:TOP_K].astype(jnp.int32)
    flat_e = top_i.reshape(n_slots)
    onehot = (flat_e[:, None] == jnp.arange(N_EXPERTS, dtype=jnp.int32)[None, :]).astype(jnp.int32)
    counts = jnp.sum(onehot, axis=0)
    rank = jnp.sum((jnp.cumsum(onehot, axis=0) - onehot) * onehot, axis=1)
    padded = ((counts + tm - 1) // tm) * tm
    group_end = jnp.cumsum(padded)
    group_start = group_end - padded
    pos = group_start[flat_e] + rank
    src_token = (jnp.arange(n_rows, dtype=jnp.int32) % m_rows).at[pos].set(
        jnp.arange(n_slots, dtype=jnp.int32) // TOP_K, unique_indices=True, mode="promise_in_bounds")
    n_used = (group_end[-1] // tm).astype(jnp.int32)
    tiles = jnp.arange(n_tiles, dtype=jnp.int32)
    tile_start = jnp.minimum(tiles, n_used - 1) * tm
    tile_expert = jnp.sum((tile_start[:, None] >= group_end[None, :]).astype(jnp.int32), axis=1)
    tile_expert = jnp.minimum(tile_expert, N_EXPERTS - 1).astype(jnp.int32)
    prev_expert = jnp.concatenate([jnp.full((1,), -1, jnp.int32), tile_expert[:-1]])
    first = ((tile_expert != prev_expert) & (tiles < n_used)).astype(jnp.int32)
    ids = jnp.arange(N_EXPERTS, dtype=jnp.int32)
    later = jnp.where((ids[None, :] > ids[:, None]) & (counts[None, :] > 0), ids[None, :], N_EXPERTS)
    next_expert = jnp.min(later, axis=1)
    is_last = next_expert == N_EXPERTS
    next_expert = jnp.where(is_last, tile_expert[0], next_expert)
    live = jnp.clip(group_start[tile_expert] + counts[tile_expert] - tiles * tm, 0, tm)
    live = jnp.where(tiles < n_used, live, 0).astype(jnp.int32)
    plan = (tile_expert, first, next_expert[tile_expert], is_last.astype(jnp.int32)[tile_expert], live)
    xs = _take_rows(h, src_token)
    g = _moe_up(xs, w1, w3, plan)
    ys = _moe_down(g, w2, plan)
    pos2 = pos.reshape(m_rows, TOP_K)
    ya = _take_rows(ys, pos2[:, 0])
    yb = _take_rows(ys, pos2[:, 1])
    common = (x, ya, yb, route, mods3, layer, n_prompt_rows, dec_seq)
    if final_g is None:
        return _moe_combine(*common, 0, m_rows, None)
    return (_moe_combine(*common, 0, n_prompt_rows, final_g),
            _moe_combine(*common, n_prompt_rows, m_rows - n_prompt_rows, final_g))


def _conv_cols(x_ref, w_ref, b_ref, o_ref, seg, act):
    length, ch = x_ref.shape
    t = lax.broadcasted_iota(jnp.int32, (length, LANE), 0) % seg

    def body(cb, carry):
        c0 = pl.multiple_of(cb * LANE, LANE)
        x = x_ref[:, pl.ds(c0, LANE)]
        w = w_ref[:, pl.ds(c0, LANE)]
        xm1 = jnp.where(t >= 1, pltpu.roll(x, 1, 0), 0.0)
        xp1 = jnp.where(t < seg - 1, pltpu.roll(x, length - 1, 0), 0.0)
        xp2 = jnp.where(t < seg - 2, pltpu.roll(x, length - 2, 0), 0.0)
        y = b_ref[:, pl.ds(c0, LANE)] + (w[0:1] * xm1 + w[1:2] * x + w[2:3] * xp1 + w[3:4] * xp2)
        if act:
            y = _silu(y)
        o_ref[:, pl.ds(c0, LANE)] = y
        return carry

    lax.fori_loop(0, ch // LANE, body, 0)


def _tri_masks(t_len):
    r = lax.broadcasted_iota(jnp.int32, (t_len, t_len), 0)
    c = lax.broadcasted_iota(jnp.int32, (t_len, t_len), 1)
    return r >= c, r <= c


def _mixer_call(body, name, proj, seq_cols, consts, init, st_shape, width, scratch, *, n_seq, length,
                row_off, n_rows_total, y_prev, emit_state, prev_state=None, **static):
    st_nd = len(st_shape)
    sps = static.get('sps')
    seq_dim = None if sps is None else sps
    sps = 1 if sps is None else sps
    assert n_seq % sps == 0 and row_off % sps == 0
    rows = sps * length
    row_off = row_off // sps
    n_steps = n_seq // sps
    n_fill = 0
    if y_prev is None:
        assert row_off == 0 and n_rows_total % rows == 0
        n_fill = n_rows_total // rows - n_steps
    own = lambda b: jnp.minimum(b, n_steps - 1)
    in_specs = [pl.BlockSpec((rows, w), lambda b, cb=cb: (own(b) + row_off, cb)) for w, cb in seq_cols]
    args = [proj] * len(seq_cols)
    for c in consts:
        in_specs.append(pl.BlockSpec(c.shape, lambda b, nd=c.ndim: (0,) * nd))
        args.append(c)
    if init is not None:
        in_specs.append(pl.BlockSpec((seq_dim,) + st_shape, lambda b: (own(b),) + (0,) * st_nd))
        args.append(init)
    n_prev = 0 if prev_state is None else prev_state.shape[1]
    if n_prev:
        in_specs.append(pl.BlockSpec((seq_dim, n_prev) + st_shape, lambda b: (own(b),) + (0,) * (st_nd + 1)))
        args.append(prev_state)
    aliases = {}
    n_alias = 0
    if y_prev is not None:
        aliases[len(args)] = 0
        in_specs.append(pl.BlockSpec(memory_space=pl.ANY))
        args.append(y_prev)
        n_alias += 1
    out_specs = [pl.BlockSpec((rows, width), lambda b: (b + row_off, 0))]
    out_shape = [jax.ShapeDtypeStruct((n_rows_total, width), BF16)]
    if emit_state:
        out_specs.append(pl.BlockSpec((seq_dim, n_prev + 1) + st_shape, lambda b: (own(b),) + (0,) * (st_nd + 1)))
        out_shape.append(jax.ShapeDtypeStruct((n_seq, n_prev + 1) + st_shape, F32))
        static = dict(static, st_layer=n_prev)
    layout = dict(n_seq_in=len(seq_cols), n_const=len(consts), has_init=init is not None, n_alias=n_alias,
                  emit_state=emit_state)
    prev_pos = len(seq_cols) + len(consts) + int(init is not None)

    def kern(*refs):
        step = pl.program_id(0)
        body_refs = refs[:prev_pos] + refs[prev_pos + 1:] if n_prev else refs

        @pl.when(step < n_steps)
        def _():
            if n_prev:
                st_out = _split_refs(body_refs, **layout)[4]
                if seq_dim is None:
                    st_out[0:n_prev] = refs[prev_pos][...]
                else:
                    st_out[:, 0:n_prev] = refs[prev_pos][...]
            body(*body_refs, **layout, **static)

        if n_fill:
            @pl.when(step >= n_steps)
            def _():
                y_ref = _split_refs(body_refs, **layout)[3]
                y_ref[...] = jnp.zeros_like(y_ref)

    return pl.pallas_call(
        kern, grid=(n_steps + n_fill,), in_specs=in_specs, out_specs=tuple(out_specs),
        out_shape=tuple(out_shape), scratch_shapes=scratch, input_output_aliases=aliases,
        compiler_params=_cparams(1), name=name)(*args)


def _split_refs(refs, n_seq_in, n_const, has_init, n_alias, emit_state):
    seq = refs[:n_seq_in]
    consts = refs[n_seq_in:n_seq_in + n_const]
    pos = n_seq_in + n_const
    init = refs[pos] if has_init else None
    pos += int(has_init) + n_alias
    y_ref = refs[pos]
    st_out = refs[pos + 1] if emit_state else None
    pos += 1 + int(emit_state)
    return seq, consts, init, y_ref, st_out, refs[pos:]


def _ssd_kernel(*refs, seg, st_layer=0, **layout):
    (z_ref, xs_ref, bc_ref, small_ref), consts, h0_ref, y_ref, hout_ref, scratch = _split_refs(refs, **layout)
    wx_ref, bx_ref, wbc_ref, bbc_ref, dtb_ref, nega_ref, dvec_ref, ng_ref = consts
    xc_ref, bcc_ref, yacc_ref, stt_ref = scratch
    length = z_ref.shape[0]
    t_len = SSD_T
    n_chunks = length // t_len
    n_pairs = SSD_H // 2
    _conv_cols(xs_ref, wx_ref, bx_ref, xc_ref, seg, True)
    _conv_cols(bc_ref, wbc_ref, bbc_ref, bcc_ref, seg, True)
    if h0_ref is None:
        stt_ref[...] = jnp.zeros_like(stt_ref)
    else:
        for d in range(2):
            for hp in range(n_pairs):
                pair = jnp.concatenate([h0_ref[d, 2 * hp], h0_ref[d, 2 * hp + 1]], axis=0)
                stt_ref[d, :, hp * LANE:(hp + 1) * LANE] = pair.T
    yacc_ref[...] = xc_ref[...] * dvec_ref[...]
    lower, upper = _tri_masks(t_len)
    tri = (lower.astype(BF16), upper.astype(BF16))
    masks = (lower, upper)
    lane_lo = lax.broadcasted_iota(jnp.int32, (t_len, LANE), 1) < SSD_P
    gn = SSD_N

    def chunk(c, d):
        r0 = pl.multiple_of(c * t_len, t_len)
        dtv = _softplus(small_ref[pl.ds(r0, t_len), :] + dtb_ref[d])
        cs = _exact_lhs_dot(tri[d], nega_ref[d] * dtv)
        cst = cs.T
        dtt = dtv.T
        end = t_len - 1 if d == 0 else 0
        bcv = bcc_ref[pl.ds(r0, t_len), :]
        groups = []
        for g in range(SSD_G):
            bg = bcv[:, g * gn:(g + 1) * gn]
            cg = bcv[:, (SSD_G + g) * gn:(SSD_G + g + 1) * gn]
            groups.append((_bdot_nt(cg, bg), bg.T, cg))
        for hp in range(n_pairs):
            gm, bgt, cg = groups[(2 * hp) // (SSD_H // SSD_G)]
            cols = slice(hp * LANE, (hp + 1) * LANE)
            x = xc_ref[pl.ds(r0, t_len), cols]
            st = stt_ref[d, :, cols]
            x_lo = jnp.where(lane_lo, x, 0.0).astype(BF16)
            x_hi = jnp.where(lane_lo, 0.0, x).astype(BF16)
            s_lo = jnp.where(lane_lo, st, 0.0).astype(BF16)
            s_hi = jnp.where(lane_lo, 0.0, st).astype(BF16)
            intra, carry_in, upd, edec = [], [], [], []
            for h in (2 * hp, 2 * hp + 1):
                colx = jnp.broadcast_to(cs[:, h:h + 1], (t_len, t_len))
                row = cst[h:h + 1, :]
                dtr = dtt[h:h + 1, :]
                cend = cst[h:h + 1, end:end + 1]
                intra.append((jnp.where(masks[d], jnp.exp(colx - row), 0.0) * (gm * dtr)).astype(BF16))
                carry_in.append((cg * jnp.exp(colx)).astype(BF16))
                upd.append((bgt * (jnp.exp(cend - row) * dtr)).astype(BF16))
                edec.append(jnp.exp(cend))
            xblk = jnp.concatenate([x_lo, x_hi], axis=0)
            y = jnp.dot(jnp.concatenate(intra + carry_in, axis=1),
                        jnp.concatenate([xblk, s_lo, s_hi], axis=0), preferred_element_type=F32)
            snew = jnp.dot(jnp.concatenate(upd, axis=1), xblk, preferred_element_type=F32)
            stt_ref[d, :, cols] = st * jnp.where(lane_lo[0:1], edec[0], edec[1]) + snew
            yacc_ref[pl.ds(r0, t_len), cols] += y

    def body(j, carry):
        chunk(j, 0)
        chunk(n_chunks - 1 - j, 1)
        return carry

    lax.fori_loop(0, n_chunks, body, 0)
    if hout_ref is not None:
        for d in range(2):
            for hp in range(n_pairs):
                pair = stt_ref[d, :, hp * LANE:(hp + 1) * LANE].T
                hout_ref[st_layer, d, 2 * hp] = pair[:SSD_P]
                hout_ref[st_layer, d, 2 * hp + 1] = pair[SSD_P:]

    def finish(c, carry):
        r0 = pl.multiple_of(c * t_len, t_len)
        y = yacc_ref[pl.ds(r0, t_len), :] * _silu(z_ref[pl.ds(r0, t_len), :])
        y_ref[pl.ds(r0, t_len), :] = (_rms(y) * ng_ref[...]).astype(y_ref.dtype)
        return carry

    lax.fori_loop(0, n_chunks, finish, 0)


def _ssd_call(proj, p, h0, **where):
    length = where['length']
    dtb = jnp.zeros((2, 1, LANE), F32).at[:, 0, :SSD_H].set(p['ssd_dt_bias'].astype(F32))
    nega = jnp.zeros((2, 1, LANE), F32).at[:, 0, :SSD_H].set(-jnp.exp(p['ssd_A_log'].astype(F32)))
    dvec = jnp.repeat(p['ssd_D'].astype(F32), SSD_P).reshape(1, SSD_D)
    cw = p['ssd_conv_w'].astype(F32)
    cb = p['ssd_conv_b'].astype(F32).reshape(1, SSD_CONV_CH)
    consts = [cw[:, :SSD_D], cb[:, :SSD_D], cw[:, SSD_D:], cb[:, SSD_D:], dtb, nega, dvec,
              p['ssd_norm_g'].astype(F32).reshape(1, SSD_D)]
    seq_cols = [(SSD_D, COL_Z // SSD_D), (SSD_D, COL_XS // SSD_D), (512, COL_BC // 512),
                (LANE, COL_SMALL // LANE)]
    scratch = [pltpu.VMEM((length, SSD_D), F32), pltpu.VMEM((length, 512), F32),
               pltpu.VMEM((length, SSD_D), F32), pltpu.VMEM((2, SSD_N, SSD_D), F32)]
    return _mixer_call(_ssd_kernel, "ssd_mixer", proj, seq_cols, consts, h0, (2, SSD_H, SSD_P, SSD_N),
                       SSD_D, scratch, **where)


def _log_sigmoid(x):
    return jnp.minimum(x, 0.0) - jnp.log1p(jnp.exp(-jnp.abs(x)))


def _gla_kernel(*refs, sps, st_layer=0, **layout):
    (q_ref, k_ref, v_ref, g_ref, small_ref), consts, s0_ref, y_ref, sout_ref, scratch = _split_refs(refs, **layout)
    gw_ref, gb_ref, ng_ref = consts
    oacc_ref, st_ref, bc_ref = scratch
    n_rows = q_ref.shape[0]
    length = n_rows // sps
    t_len = GLA_T
    n_chunks = length // t_len
    st_ref[...] = jnp.zeros_like(st_ref)
    if s0_ref is not None:
        for s in range(sps):
            for d in range(2):
                for h in range(GLA_H):
                    st_ref[s, d, h * GLA_DV:(h + 1) * GLA_DV, h * GLA_DK:(h + 1) * GLA_DK] = s0_ref[s, d, h]
    oacc_ref[...] = jnp.zeros_like(oacc_ref)
    lower, upper = _tri_masks(t_len)
    masks = tuple(jnp.concatenate([m] * GLA_H, axis=0) for m in (lower, upper))
    q_head = (lax.broadcasted_iota(jnp.int32, (GLA_H * t_len, GLA_KW), 0) // t_len
              == lax.broadcasted_iota(jnp.int32, (GLA_H * t_len, GLA_KW), 1) // GLA_DK)
    st_diag = (lax.broadcasted_iota(jnp.int32, (GLA_VW, GLA_KW), 0) // GLA_DV
               == lax.broadcasted_iota(jnp.int32, (GLA_VW, GLA_KW), 1) // GLA_DK)
    scale = GLA_DK ** -0.5

    blk = 256
    rr = lax.broadcasted_iota(jnp.int32, (blk, blk), 0)
    cc = lax.broadcasted_iota(jnp.int32, (blk, blk), 1)
    same_chunk = rr // t_len == cc // t_len
    tri_blk = ((same_chunk & (rr >= cc)).astype(BF16), (same_chunk & (rr <= cc)).astype(BF16))
    gate_w = []
    for d in range(2):
        w1, w2 = _split2(gw_ref[d])
        gate_w.append(jnp.concatenate([w1, w1, w2], axis=0))

    def decay_sums(b, carry):
        r0 = pl.multiple_of(b * blk, blk)
        s1, s2 = _split2(small_ref[pl.ds(r0, blk), :])
        lhs = jnp.concatenate([s1, s2, s1], axis=1)
        for d in range(2):
            logit = jnp.dot(lhs, gate_w[d], preferred_element_type=F32) + gb_ref[d]
            la = _log_sigmoid(logit) / GLA_TAU
            bc3 = jnp.dot(tri_blk[d], jnp.concatenate(_split3(la), axis=1), preferred_element_type=F32)
            bc_ref[d, pl.ds(r0, blk), :] = bc3[:, :GLA_KW] + bc3[:, GLA_KW:2 * GLA_KW] + bc3[:, 2 * GLA_KW:]
        return carry

    lax.fori_loop(0, n_rows // blk, decay_sums, 0)

    def chunk(c, d, s):
        r0 = pl.multiple_of(s * length + c * t_len, t_len)
        bc = bc_ref[d, pl.ds(r0, t_len), :]
        end = t_len - 1 if d == 0 else 0
        mid = bc[t_len // 2:t_len // 2 + 1, :]
        tot = bc[end:end + 1, :]
        q = q_ref[pl.ds(r0, t_len), :] * scale
        k = k_ref[pl.ds(r0, t_len), :]
        v = v_ref[pl.ds(r0, t_len), :].astype(BF16)
        qt = q * jnp.exp(bc - mid)
        kt = k * jnp.exp(mid - bc)
        qs = q * jnp.exp(bc)
        kd = k * jnp.exp(tot - bc)
        q_rows = jnp.where(q_head, jnp.concatenate([qt] * GLA_H, axis=0), 0.0)
        att = jnp.where(masks[d], _bdot_nt(q_rows, kt), 0.0)
        o_all = jnp.dot(att.astype(BF16), v, preferred_element_type=F32)
        o = jnp.concatenate([o_all[h * t_len:(h + 1) * t_len, h * GLA_DV:(h + 1) * GLA_DV]
                             for h in range(GLA_H)], axis=1)
        st = st_ref[s, d]
        o = o + _bdot_nt(qs, st)
        upd = lax.dot_general(v, kd.astype(BF16), (((0,), (0,)), ((), ())), preferred_element_type=F32)
        st_ref[s, d] = st * jnp.exp(tot) + jnp.where(st_diag, upd, 0.0)
        oacc_ref[pl.ds(r0, t_len), :] += o

    def body(j, carry):
        for s in range(sps):
            chunk(j, 0, s)
            chunk(n_chunks - 1 - j, 1, s)
        return carry

    lax.fori_loop(0, n_chunks, body, 0)
    if sout_ref is not None:
        for s in range(sps):
            for d in range(2):
                for h in range(GLA_H):
                    sout_ref[s, st_layer, d, h] = st_ref[s, d, h * GLA_DV:(h + 1) * GLA_DV,
                                                         h * GLA_DK:(h + 1) * GLA_DK]

    def finish(c, carry):
        r0 = pl.multiple_of(c * t_len, t_len)
        gate = _silu(g_ref[pl.ds(r0, t_len), :])
        for h in range(GLA_H):
            vs = slice(h * GLA_DV, (h + 1) * GLA_DV)
            o = _rms(oacc_ref[pl.ds(r0, t_len), vs]) * ng_ref[...]
            y_ref[pl.ds(r0, t_len), vs] = (o * gate[:, vs]).astype(y_ref.dtype)
        return carry

    lax.fori_loop(0, sps * n_chunks, finish, 0)


def _gla_call(proj, p, s0t, **where):
    length = where['length']
    sps = where['sps']
    gw = jnp.zeros((2, LANE, GLA_KW), F32).at[:, GLA_RANK:2 * GLA_RANK, :].set(p['gla_gate_w'].astype(F32))
    consts = [gw, p['gla_gate_b'].astype(F32).reshape(2, 1, GLA_KW),
              p['gla_norm_g'].astype(F32).reshape(1, GLA_DV)]
    seq_cols = [(GLA_KW, COL_Q // GLA_KW), (GLA_KW, COL_K // GLA_KW), (GLA_VW, COL_V // GLA_VW),
                (GLA_VW, COL_G // GLA_VW), (LANE, COL_SMALL // LANE)]
    st_shape = (2, GLA_H, GLA_DV, GLA_DK)
    scratch = [pltpu.VMEM((sps * length, GLA_VW), F32), pltpu.VMEM((sps, 2, GLA_VW, GLA_KW), F32),
               pltpu.VMEM((2, sps * length, GLA_KW), F32)]
    return _mixer_call(_gla_kernel, "gla_mixer", proj, seq_cols, consts, s0t, st_shape, GLA_VW, scratch, **where)


def _lru_kernel(*refs, seg, sps, st_layer=0, **layout):
    (xb_ref, gb_ref), consts, h0_ref, y_ref, hout_ref, scratch = _split_refs(refs, **layout)
    cw_ref, cb_ref, w_ref, bias_ref, sp_ref = consts
    xr_ref, a_ref, u_ref = scratch
    n_rows = xb_ref.shape[0]
    length = n_rows // sps
    blk = 256
    _conv_cols(xb_ref, cw_ref, cb_ref, xr_ref, seg, False)

    def gates(c, carry):
        r0 = pl.multiple_of(c * blk, blk)
        xr = xr_ref[pl.ds(r0, blk), :]
        xbf = xr.astype(BF16)
        for d in range(2):
            pre = jnp.dot(xbf, w_ref[d], preferred_element_type=F32) + bias_ref[d]
            r = jax.nn.sigmoid(pre[:, :LRU_W])
            i = jax.nn.sigmoid(pre[:, LRU_W:])
            log_a = (-LRU_C) * r * sp_ref[d]
            a = jnp.exp(log_a)
            a_ref[d, pl.ds(r0, blk), :] = a
            u_ref[d, pl.ds(r0, blk), :] = jnp.sqrt(-jnp.tanh(log_a) * (a * a + 1.0)) * (i * xr)
        return carry

    lax.fori_loop(0, n_rows // blk, gates, 0)

    def scan(t, carry):
        out = []
        for s in range(sps):
            for d in range(2):
                row = s * length + (t if d == 0 else length - 1 - t)
                h = a_ref[d, pl.ds(row, 1), :] * carry[2 * s + d] + u_ref[d, pl.ds(row, 1), :]
                u_ref[d, pl.ds(row, 1), :] = h
                out.append(h)
        return tuple(out)

    if h0_ref is None:
        start = tuple(jnp.zeros((1, LRU_W), F32) for _ in range(2 * sps))
    else:
        start = tuple(h0_ref[s, d] for s in range(sps) for d in range(2))
    final = lax.fori_loop(0, length, scan, start)
    if hout_ref is not None:
        for s in range(sps):
            for d in range(2):
                hout_ref[s, st_layer, d] = final[2 * s + d]

    def finish(c, carry):
        r0 = pl.multiple_of(c * blk, blk)
        y = (u_ref[0, pl.ds(r0, blk), :] + u_ref[1, pl.ds(r0, blk), :]) * jax.nn.gelu(gb_ref[pl.ds(r0, blk), :])
        y_ref[pl.ds(r0, blk), :] = y.astype(y_ref.dtype)
        return carry

    lax.fori_loop(0, n_rows // blk, finish, 0)


def _block_diag(w):
    nb, bw, _ = w.shape
    eye = jnp.eye(nb, dtype=w.dtype)
    return (eye[:, None, :, None] * w[:, :, None, :]).reshape(nb * bw, nb * bw)


def _lru_call(proj, p, h0, **where):
    length = where['length']
    w = jnp.stack([jnp.concatenate([_block_diag(p['lru_wa'][d].astype(F32)),
                                    _block_diag(p['lru_wx'][d].astype(F32))], axis=1) for d in range(2)])
    bias = jnp.stack([jnp.concatenate([p['lru_ba'][d], p['lru_bx'][d]]) for d in range(2)]).astype(F32)
    sp = jax.nn.softplus(-p['lru_lambda'].astype(F32)).reshape(2, 1, LRU_W)
    consts = [p['lru_conv_w'].astype(F32), p['lru_conv_b'].astype(F32).reshape(1, LRU_W), w.astype(BF16),
              bias.reshape(2, 1, 2 * LRU_W), sp]
    seq_cols = [(LRU_W, COL_XB // LRU_W), (LRU_W, COL_GB // LRU_W)]
    rows = where['sps'] * length
    scratch = [pltpu.VMEM((rows, LRU_W), F32), pltpu.VMEM((2, rows, LRU_W), F32),
               pltpu.VMEM((2, rows, LRU_W), F32)]
    return _mixer_call(_lru_kernel, "lru_mixer", proj, seq_cols, consts, h0, (2, 1, LRU_W), LRU_W, scratch,
                       **where)


def kernel(x_prompt, x_sample, state_ssd, state_gla, state_lru, c, c_ctx, mod_w, mod_b, norm1_g, norm2_g, in_w, ssd_conv_w, ssd_conv_b, ssd_A_log, ssd_dt_bias, ssd_D, ssd_norm_g, gla_gate_w, gla_gate_b, gla_norm_g, lru_conv_w, lru_conv_b, lru_wa, lru_ba, lru_wx, lru_bx, lru_lambda, out_w, ffn_w1, ffn_w3, ffn_w2, moe_router, moe_w1, moe_w3, moe_w2, final_norm_g):
    bp, lp, d = x_prompt.shape
    bs, ls, _ = x_sample.shape
    n_p = bp * lp
    n_s = bs * ls
    n_all = n_p + n_s
    depth = in_w.shape[0]
    assert n_p % ls == 0 and 1 + bs <= MOD_ROWS

    cvec = jnp.zeros((MOD_ROWS, d), F32).at[0].set(c_ctx.astype(F32)).at[1:1 + bs].set(c.astype(F32))
    mods = _adaln(cvec, mod_w, mod_b)
    mods3 = mods.reshape(depth * MOD_ROWS * 6, 1, d)
    x = (x_prompt.reshape(n_p, d).astype(F32), x_sample.reshape(n_s, d).astype(F32))

    in_w_t = jnp.swapaxes(in_w, 1, 2)
    st_ssd = st_gla = st_lru = None
    for i in range(depth):
        p = {'ssd_conv_w': ssd_conv_w[i], 'ssd_conv_b': ssd_conv_b[i], 'ssd_A_log': ssd_A_log[i],
             'ssd_dt_bias': ssd_dt_bias[i], 'ssd_D': ssd_D[i], 'ssd_norm_g': ssd_norm_g[i],
             'gla_gate_w': gla_gate_w[i], 'gla_gate_b': gla_gate_b[i], 'gla_norm_g': gla_norm_g[i],
             'lru_conv_w': lru_conv_w[i], 'lru_conv_b': lru_conv_b[i], 'lru_wa': lru_wa[i],
             'lru_ba': lru_ba[i], 'lru_wx': lru_wx[i], 'lru_bx': lru_bx[i], 'lru_lambda': lru_lambda[i]}
        h = _norm_mod(x, norm1_g[i], mods3, i, 0, n_p, ls)
        proj = _in_proj(h, in_w_t, i)
        ctx = dict(n_seq=bp, length=lp, row_off=0, n_rows_total=n_all, y_prev=None, emit_state=True)
        lat = dict(n_seq=bs, length=ls, row_off=n_p // ls, n_rows_total=n_all, emit_state=False)
        y_ssd, st_ssd = _ssd_call(proj, p, None, seg=lp, prev_state=st_ssd, **ctx)
        y_ssd, = _ssd_call(proj, p, state_ssd[:, i].astype(F32), seg=GRID_W, y_prev=y_ssd, **lat)
        y_gla, st_gla = _gla_call(proj, p, None, sps=MIX_SPS, prev_state=st_gla, **ctx)
        y_gla, = _gla_call(proj, p, jnp.swapaxes(state_gla[:, i].astype(F32), -1, -2), sps=MIX_SPS,
                           y_prev=y_gla, **lat)
        y_lru, st_lru = _lru_call(proj, p, None, seg=lp, sps=MIX_SPS, prev_state=st_lru, **ctx)
        y_lru, = _lru_call(proj, p, state_lru[:, i].astype(F32).reshape(bs, 2, 1, LRU_W), seg=GRID_W,
                           sps=MIX_SPS, y_prev=y_lru, **lat)
        x = _out_proj(y_ssd, y_gla, y_lru, out_w, x, mods3, i, n_p, ls)
        j = i // 2
        if i % 2 == 1:
            h2, route = _norm_mod(x, norm2_g[i], mods3, i, 1, n_p, ls, router=moe_router[j])
            x = _moe_ffn(x, h2, route, moe_w1[j], moe_w3[j], moe_w2[j], mods3, i, n_p, ls,
                         final_g=final_norm_g if i == depth - 1 else None)
        else:
            h2 = _norm_mod(x, norm2_g[i], mods3, i, 1, n_p, ls)
            x = _ffn_down(_ffn_up(h2, ffn_w1[j], ffn_w3[j]), ffn_w2[j], x, mods3, i, n_p, ls)
    if isinstance(x, tuple):
        y_p, y_s = x
    else:
        y_p = _final_norm(x, final_norm_g, 0, n_p)
        y_s = _final_norm(x, final_norm_g, n_p, n_s)
    return (y_p.reshape(bp, lp, d), y_s.reshape(bs, ls, d), st_ssd, jnp.swapaxes(st_gla, -1, -2),
            st_lru.reshape(bp, depth, 2, LRU_W))
```

```python
import functools

import jax
import jax.numpy as jnp
from jax import lax
from jax.experimental import pallas as pl
from jax.experimental.pallas import tpu as pltpu

F32 = jnp.float32
BF16 = jnp.bfloat16

D_MODEL = 2048
GRID_W = 64
SSD_D = D_MODEL // 2
SSD_P = 64
SSD_H = SSD_D // SSD_P
SSD_G = 2
SSD_N = 128
GLA_H = 4
GLA_VW = D_MODEL // 4
GLA_DV = GLA_VW // GLA_H
GLA_DK = GLA_DV // 2
GLA_KW = GLA_H * GLA_DK
GLA_RANK = 16
GLA_TAU = 16.0
LRU_W = D_MODEL // 4
LRU_NB = 8
LRU_BW = LRU_W // LRU_NB
LRU_C = 8.0
SSD_CONV_CH = SSD_D + 2 * SSD_G * SSD_N
N_EXPERTS = 8
TOP_K = 2
EPS = 1e-6

LANE = 128
SSD_T = 128
GLA_T = 64
MIX_SPS = 2
MOD_ROWS = 8
VMEM_LIMIT = 56 * 1024 * 1024

COL_Z = 0
COL_XS = 1024
COL_BC = 2048
COL_Q = 2560
COL_K = 2816
COL_V = 3072
COL_G = 3584
COL_XB = 4096
COL_GB = 4608
COL_SMALL = 5120
IN_PAD = 5632


def _cparams(n_axes):
    return pltpu.CompilerParams(dimension_semantics=("arbitrary",) * n_axes,
                                vmem_limit_bytes=VMEM_LIMIT)


def _bdot(a, b):
    return jnp.dot(a.astype(BF16), b.astype(BF16), preferred_element_type=F32)


def _bdot_nt(a, b):
    return lax.dot_general(a.astype(BF16), b.astype(BF16), (((1,), (1,)), ((), ())),
                           preferred_element_type=F32)


def _bdot_tn(a, b):
    return lax.dot_general(a.astype(BF16), b.astype(BF16), (((0,), (0,)), ((), ())),
                           preferred_element_type=F32)


def _split2(a):
    a1 = a.astype(BF16)
    a2 = (a - a1.astype(F32)).astype(BF16)
    return a1, a2


def _split3(a):
    a1 = a.astype(BF16)
    r = a - a1.astype(F32)
    a2 = r.astype(BF16)
    a3 = (r - a2.astype(F32)).astype(BF16)
    return a1, a2, a3


def _exact_lhs_dot(m_bf16, a):
    a1, a2, a3 = _split3(a)
    f = lambda z: jnp.dot(m_bf16, z, preferred_element_type=F32)
    return f(a1) + f(a2) + f(a3)


def _dot3(a, b):
    a1, a2 = _split2(a)
    b1, b2 = _split2(b)
    f = lambda x, y: jnp.dot(x, y, preferred_element_type=F32)
    return f(a1, b1) + (f(a1, b2) + f(a2, b1))


def _softplus(x):
    return jnp.maximum(x, 0.0) + jnp.log1p(jnp.exp(-jnp.abs(x)))


def _silu(x):
    return x * jax.nn.sigmoid(x)


def _mod_row(m, tm, n_prompt_rows, dec_seq):
    r0 = m * tm
    return jnp.where(r0 < n_prompt_rows, 0, 1 + (r0 - n_prompt_rows) // dec_seq)


def _adaln_kernel(c_ref, w_ref, b_ref, o_ref):
    s = _silu(c_ref[...])
    o_ref[...] = _dot3(s, w_ref[...]) + b_ref[...]


def _adaln(cvec8, mod_w, mod_b):
    depth, d, n = mod_w.shape
    tn = 1024
    return pl.pallas_call(
        _adaln_kernel,
        grid=(depth, n // tn),
        in_specs=[pl.BlockSpec((MOD_ROWS, d), lambda i, j: (0, 0)),
                  pl.BlockSpec((None, d, tn), lambda i, j: (i, 0, j)),
                  pl.BlockSpec((None, 1, tn), lambda i, j: (i, 0, j))],
        out_specs=pl.BlockSpec((None, MOD_ROWS, tn), lambda i, j: (i, 0, j)),
        out_shape=jax.ShapeDtypeStruct((depth, MOD_ROWS, n), F32),
        compiler_params=_cparams(2),
        name="adaln",
    )(cvec8, mod_w, mod_b.reshape(depth, 1, n))


def _rms(x):
    return x * lax.rsqrt(jnp.mean(x * x, axis=-1, keepdims=True) + EPS)


def _stacked_specs(x, block, row_tile, col_block):
    if not isinstance(x, tuple):
        return [pl.BlockSpec(block, lambda *g: (row_tile(*g), col_block(*g)))], [x], 0
    split = x[0].shape[0] // block[0]
    specs = [pl.BlockSpec(block, lambda *g: (jnp.minimum(row_tile(*g), split - 1), col_block(*g))),
             pl.BlockSpec(block, lambda *g: (jnp.maximum(row_tile(*g) - split, 0), col_block(*g)))]
    return specs, list(x), split


def _stacked_tile(x_refs, m, split):
    if len(x_refs) == 1:
        return x_refs[0][...]
    return jnp.where(m < split, x_refs[0][...], x_refs[1][...])


def _norm_mod_kernel(*refs, split):
    g_ref, sh_ref, sc_ref, o_ref = refs[-4:]
    y = _rms(_stacked_tile(refs[:-4], pl.program_id(0), split)) * g_ref[...]
    o_ref[...] = (y * (1.0 + sc_ref[...]) + sh_ref[...]).astype(o_ref.dtype)


def _top2_route(h, router_pad):
    logits = _dot3(h, router_pad)
    lane = lax.broadcasted_iota(jnp.int32, logits.shape, 1)
    neg = jnp.float32(-jnp.inf)
    lg = jnp.where(lane < N_EXPERTS, logits, neg)
    m1 = jnp.max(lg, axis=-1, keepdims=True)
    i1 = jnp.min(jnp.where(lg == m1, lane, LANE), axis=-1, keepdims=True)
    lg2 = jnp.where(lane == i1, neg, lg)
    m2 = jnp.max(lg2, axis=-1, keepdims=True)
    i2 = jnp.min(jnp.where(lg2 == m2, lane, LANE), axis=-1, keepdims=True)
    e2 = jnp.exp(m2 - m1)
    den = 1.0 + e2
    g1 = 1.0 / den
    g2 = e2 / den
    return jnp.where(lane == 0, i1.astype(F32),
                     jnp.where(lane == 1, i2.astype(F32),
                               jnp.where(lane == 2, g1, jnp.where(lane == 3, g2, 0.0))))


def _norm_mod(x, g, mods3, layer, which, n_prompt_rows, dec_seq):
    d = g.shape[0]
    m_rows = sum(part.shape[0] for part in x) if isinstance(x, tuple) else x.shape[0]
    tm = 512
    base = layer * MOD_ROWS * 6

    def mod_spec(k):
        return pl.BlockSpec((None, 1, d),
                            lambda m: (base + _mod_row(m, tm, n_prompt_rows, dec_seq) * 6 + k, 0, 0))

    x_specs, x_args, split = _stacked_specs(x, (tm, d), lambda m: m, lambda m: 0)
    in_specs = x_specs + [pl.BlockSpec((1, d), lambda m: (0, 0)), mod_spec(3 * which), mod_spec(3 * which + 1)]
    args = x_args + [g.reshape(1, d), mods3, mods3]
    return pl.pallas_call(
        functools.partial(_norm_mod_kernel, split=split), grid=(m_rows // tm,), in_specs=in_specs,
        out_specs=pl.BlockSpec((tm, d), lambda m: (m, 0)), out_shape=jax.ShapeDtypeStruct((m_rows, d), BF16),
        compiler_params=_cparams(1), name="norm_mod")(*args)


def _final_norm_kernel(x_ref, g_ref, o_ref):
    o_ref[...] = _rms(x_ref[...]) * g_ref[...]


def _final_norm(x, g, row0, n_rows):
    d = x.shape[1]
    tm = 512
    off = row0 // tm
    return pl.pallas_call(
        _final_norm_kernel, grid=(n_rows // tm,),
        in_specs=[pl.BlockSpec((tm, d), lambda m: (m + off, 0)), pl.BlockSpec((1, d), lambda m: (0, 0))],
        out_specs=pl.BlockSpec((tm, d), lambda m: (m, 0)),
        out_shape=jax.ShapeDtypeStruct((n_rows, d), F32),
        compiler_params=_cparams(1), name="final_norm")(x, g.reshape(1, d))


IN_TN = 512
IN_DT_COL = SSD_D + SSD_CONV_CH
IN_GRAW_COL = IN_DT_COL + SSD_H + 2 * GLA_KW + 2 * GLA_VW


def _in_proj_kernel(x_ref, wm_ref, wn_ref, wdt_ref, wgr_ref, o_ref, wbf_ref):
    j = pl.program_id(0)

    @pl.when(pl.program_id(1) == 0)
    def _():
        def shifted(s):
            w = jnp.concatenate([wm_ref[...], wn_ref[...]], axis=0)
            wbf_ref[...] = w[s:s + IN_TN].astype(BF16)

        @pl.when(j < COL_Q // IN_TN)
        def _():
            wbf_ref[...] = wm_ref[...].astype(BF16)

        @pl.when((j >= COL_Q // IN_TN) & (j < COL_XB // IN_TN))
        def _():
            shifted(SSD_H)

        @pl.when((j >= COL_XB // IN_TN) & (j < COL_SMALL // IN_TN))
        def _():
            shifted(SSD_H + GLA_RANK)

        @pl.when(j == COL_SMALL // IN_TN)
        def _():
            wbf_ref[...] = jnp.zeros_like(wbf_ref)
            wbf_ref[0:SSD_H] = wdt_ref[...].astype(BF16)
            wbf_ref[SSD_H:SSD_H + GLA_RANK] = wgr_ref[...].astype(BF16)

    nt = (((1,), (1,)), ((), ()))

    @pl.when(j < COL_SMALL // IN_TN)
    def _():
        o_ref[...] = lax.dot_general(x_ref[...], wbf_ref[...], nt, preferred_element_type=F32)

    @pl.when(j == COL_SMALL // IN_TN)
    def _():
        o_ref[:, :LANE] = lax.dot_general(x_ref[...], wbf_ref[0:LANE], nt, preferred_element_type=F32)
        o_ref[:, LANE:] = jnp.zeros((o_ref.shape[0], o_ref.shape[1] - LANE), F32)


def _in_proj(h, in_w_t, layer):
    m_rows, k = h.shape
    tm, tn = 2048, IN_TN
    n_main = COL_SMALL // tn
    shift_max = SSD_H + GLA_RANK
    last_next_block = in_w_t.shape[1] // shift_max - 1
    return pl.pallas_call(
        _in_proj_kernel, grid=(IN_PAD // tn, m_rows // tm),
        in_specs=[pl.BlockSpec((tm, k), lambda j, m: (m, 0)),
                  pl.BlockSpec((None, tn, k), lambda j, m: (layer, jnp.minimum(j, n_main - 1), 0)),
                  pl.BlockSpec((None, shift_max, k),
                               lambda j, m: (layer, jnp.minimum((j + 1) * (tn // shift_max), last_next_block), 0)),
                  pl.BlockSpec((None, SSD_H, k), lambda j, m: (layer, IN_DT_COL // SSD_H, 0)),
                  pl.BlockSpec((None, GLA_RANK, k), lambda j, m: (layer, IN_GRAW_COL // GLA_RANK, 0))],
        out_specs=pl.BlockSpec((tm, tn), lambda j, m: (m, j)),
        out_shape=jax.ShapeDtypeStruct((m_rows, IN_PAD), F32),
        scratch_shapes=[pltpu.VMEM((tn, k), BF16)],
        compiler_params=_cparams(2), name="in_proj")(h, in_w_t, in_w_t, in_w_t, in_w_t)


def _out_proj_kernel(y1_ref, y2_ref, y3_ref, w_hbm, *rest, n_res, split, layer, route):
    res_refs = rest[:n_res]
    gate_ref, g_ref, sh_ref, sc_ref = rest[n_res:n_res + 4]
    pos = n_res + 4
    r_ref = rest[pos] if route else None
    pos += int(route)
    x_out, h_out = rest[pos:pos + 2]
    pos += 2
    route_out = rest[pos] if route else None
    pos += int(route)
    wst_ref, wbf_ref, sem = rest[pos:]
    m = pl.program_id(0)

    @pl.when(m == 0)
    def _():
        copy = pltpu.make_async_copy(w_hbm.at[layer], wst_ref, sem.at[0])
        copy.start()
        copy.wait()
        wbf_ref[...] = wst_ref[...].astype(BF16)

    k1 = y1_ref.shape[1]
    k2 = k1 + y2_ref.shape[1]
    acc = jnp.dot(y1_ref[...], wbf_ref[0:k1, :], preferred_element_type=F32)
    acc += jnp.dot(y2_ref[...], wbf_ref[k1:k2, :], preferred_element_type=F32)
    acc += jnp.dot(y3_ref[...], wbf_ref[k2:, :], preferred_element_type=F32)
    x_new = _stacked_tile(res_refs, m, split) + gate_ref[...] * acc
    x_out[...] = x_new
    h = (_rms(x_new) * g_ref[...]) * (1.0 + sc_ref[...]) + sh_ref[...]
    h_out[...] = h.astype(h_out.dtype)
    if route:
        route_out[...] = _top2_route(h, r_ref[...])


def _out_proj(y1, y2, y3, w, res, norm_g, mods3, layer, n_prompt_rows, dec_seq, router=None):
    m_rows = y1.shape[0]
    k, d = w.shape[1:]
    tm = 256
    base = layer * MOD_ROWS * 6
    route = router is not None

    def mod_spec(which):
        return pl.BlockSpec((None, 1, d),
                            lambda m: (base + _mod_row(m, tm, n_prompt_rows, dec_seq) * 6 + which, 0, 0))

    row = lambda width: pl.BlockSpec((tm, width), lambda m: (m, 0))
    res_specs, res_args, split = _stacked_specs(res, (tm, d), lambda m: m, lambda m: 0)
    in_specs = [row(y1.shape[1]), row(y2.shape[1]), row(y3.shape[1]), pl.BlockSpec(memory_space=pl.ANY)]
    in_specs += res_specs + [mod_spec(2), pl.BlockSpec((1, d), lambda m: (0, 0)), mod_spec(3), mod_spec(4)]
    args = [y1, y2, y3, w] + res_args + [mods3, norm_g.reshape(1, d), mods3, mods3]
    out_specs = [row(d), row(d)]
    out_shape = [jax.ShapeDtypeStruct((m_rows, d), F32), jax.ShapeDtypeStruct((m_rows, d), F32 if route else BF16)]
    if route:
        in_specs.append(pl.BlockSpec((d, LANE), lambda m: (0, 0)))
        args.append(jnp.zeros((d, LANE), F32).at[:, :N_EXPERTS].set(router.astype(F32)))
        out_specs.append(row(LANE))
        out_shape.append(jax.ShapeDtypeStruct((m_rows, LANE), F32))
    kern = functools.partial(_out_proj_kernel, n_res=len(res_args), split=split, layer=layer, route=route)
    return pl.pallas_call(
        kern, grid=(m_rows // tm,), in_specs=in_specs, out_specs=tuple(out_specs), out_shape=tuple(out_shape),
        scratch_shapes=[pltpu.VMEM((k, d), F32), pltpu.VMEM((k, d), BF16), pltpu.SemaphoreType.DMA((1,))],
        compiler_params=_cparams(1), name="out_proj")(*args)


def _ffn_up_kernel(x_ref, w1_ref, w3_ref, o_ref, w1bf_ref, w3bf_ref):
    @pl.when(pl.program_id(1) == 0)
    def _():
        w1bf_ref[...] = w1_ref[...].astype(BF16)
        w3bf_ref[...] = w3_ref[...].astype(BF16)

    a = jnp.dot(x_ref[...], w1bf_ref[...], preferred_element_type=F32)
    b = jnp.dot(x_ref[...], w3bf_ref[...], preferred_element_type=F32)
    o_ref[...] = (_silu(a) * b).astype(o_ref.dtype)


def _ffn_up(h, w1, w3):
    m_rows, d = h.shape
    f = w1.shape[1]
    tm, tf = 1024, 512
    return pl.pallas_call(
        _ffn_up_kernel, grid=(pl.cdiv(f, tf), m_rows // tm),
        in_specs=[pl.BlockSpec((tm, d), lambda j, m: (m, 0)),
                  pl.BlockSpec((d, tf), lambda j, m: (0, j)),
                  pl.BlockSpec((d, tf), lambda j, m: (0, j))],
        out_specs=pl.BlockSpec((tm, tf), lambda j, m: (m, j)),
        out_shape=jax.ShapeDtypeStruct((m_rows, f), BF16),
        scratch_shapes=[pltpu.VMEM((d, tf), BF16), pltpu.VMEM((d, tf), BF16)],
        compiler_params=_cparams(2), name="ffn_up")(h, w1, w3)


def _ffn_down_kernel(g_ref, w_ref, res_ref, gate_ref, o_ref, wbf_ref):
    @pl.when(pl.program_id(1) == 0)
    def _():
        wbf_ref[...] = w_ref[...].astype(BF16)

    acc = jnp.dot(g_ref[...], wbf_ref[...], preferred_element_type=F32)
    o_ref[...] = res_ref[...] + gate_ref[...] * acc


def _ffn_down(g, w2, res, mods3, layer, n_prompt_rows, dec_seq):
    m_rows, d = res.shape
    f = w2.shape[0]
    tm, tn = 512, 512
    base = layer * MOD_ROWS * 6
    return pl.pallas_call(
        _ffn_down_kernel, grid=(d // tn, m_rows // tm),
        in_specs=[pl.BlockSpec((tm, f), lambda j, m: (m, 0)),
                  pl.BlockSpec((f, tn), lambda j, m: (0, j)),
                  pl.BlockSpec((tm, tn), lambda j, m: (m, j)),
                  pl.BlockSpec((None, 1, tn),
                               lambda j, m: (base + _mod_row(m, tm, n_prompt_rows, dec_seq) * 6 + 5, 0, j))],
        out_specs=pl.BlockSpec((tm, tn), lambda j, m: (m, j)),
        out_shape=jax.ShapeDtypeStruct((m_rows, d), F32),
        scratch_shapes=[pltpu.VMEM((f, tn), BF16)],
        compiler_params=_cparams(2), name="ffn_down")(g, w2, res, mods3)


MOE_TM = 256
MOE_ROW_QUARTERS = 4


def _grouped_weights(plan_refs, w_hbm, stages, casts, sems):
    te_ref, first_ref, next_e_ref, last_ref = plan_refs
    j = pl.program_id(0)
    m = pl.program_id(1)
    n_col_tiles = pl.num_programs(0)

    def fetch(e, jj, k):
        width = stages[k].shape[1]
        src = w_hbm[k].at[e, :, pl.ds(pl.multiple_of(jj * width, width), width)]
        return pltpu.make_async_copy(src, stages[k], sems.at[k])

    @pl.when((j == 0) & (m == 0))
    def _():
        for k in range(len(stages)):
            fetch(te_ref[0], 0, k).start()

    @pl.when(first_ref[m] == 1)
    def _():
        for k in range(len(stages)):
            fetch(te_ref[m], j, k).wait()
            casts[k][...] = stages[k][...].astype(BF16)
        next_j = j + last_ref[m]

        @pl.when(next_j < n_col_tiles)
        def _():
            for k in range(len(stages)):
                fetch(next_e_ref[m], next_j, k).start()


def _by_live_rows(live, o_ref, compute):
    rows = o_ref.shape[0]
    step = rows // MOE_ROW_QUARTERS
    for q in range(1, MOE_ROW_QUARTERS + 1):
        top = q * step

        @pl.when((live > top - step) & (live <= top))
        def _(top=top):
            o_ref[0:top] = compute(slice(0, top))
            if top < rows:
                o_ref[top:] = jnp.zeros((rows - top, o_ref.shape[1]), o_ref.dtype)

    @pl.when(live == 0)
    def _():
        o_ref[...] = jnp.zeros_like(o_ref)


def _moe_up_kernel(te_ref, first_ref, next_e_ref, last_ref, live_ref, x_ref, w1_hbm, w3_hbm, o_ref,
                   w1st_ref, w3st_ref, w1bf_ref, w3bf_ref, sems):
    _grouped_weights((te_ref, first_ref, next_e_ref, last_ref), (w1_hbm, w3_hbm), (w1st_ref, w3st_ref),
                     (w1bf_ref, w3bf_ref), sems)

    def compute(rows):
        x = x_ref[rows].astype(BF16)
        a = jnp.dot(x, w1bf_ref[...], preferred_element_type=F32)
        b = jnp.dot(x, w3bf_ref[...], preferred_element_type=F32)
        return (_silu(a) * b).astype(o_ref.dtype)

    _by_live_rows(live_ref[pl.program_id(1)], o_ref, compute)


def _moe_up(xs, w1, w3, plan):
    n_rows, d = xs.shape
    f = w1.shape[2]
    tm, tf = MOE_TM, 1792
    n_plan = len(plan)
    grid_spec = pltpu.PrefetchScalarGridSpec(
        num_scalar_prefetch=n_plan, grid=(f // tf, n_rows // tm),
        in_specs=[pl.BlockSpec((tm, d), lambda j, m, *_: (m, 0)),
                  pl.BlockSpec(memory_space=pl.ANY), pl.BlockSpec(memory_space=pl.ANY)],
        out_specs=pl.BlockSpec((tm, tf), lambda j, m, *_: (m, j)),
        scratch_shapes=[pltpu.VMEM((d, tf), F32), pltpu.VMEM((d, tf), F32),
                        pltpu.VMEM((d, tf), BF16), pltpu.VMEM((d, tf), BF16),
                        pltpu.SemaphoreType.DMA((2,))])
    return pl.pallas_call(
        _moe_up_kernel, grid_spec=grid_spec,
        out_shape=jax.ShapeDtypeStruct((n_rows, f), BF16),
        compiler_params=_cparams(2), name="moe_up")(*plan, xs, w1, w3)


def _moe_down_kernel(te_ref, first_ref, next_e_ref, last_ref, live_ref, g_ref, w_hbm, o_ref,
                     wst_ref, wbf_ref, sems):
    _grouped_weights((te_ref, first_ref, next_e_ref, last_ref), (w_hbm,), (wst_ref,), (wbf_ref,), sems)

    def compute(rows):
        return jnp.dot(g_ref[rows], wbf_ref[...], preferred_element_type=F32)

    _by_live_rows(live_ref[pl.program_id(1)], o_ref, compute)


def _moe_down(g, w2, plan):
    n_rows, f = g.shape
    d = w2.shape[2]
    tm, tn = MOE_TM, 1024
    grid_spec = pltpu.PrefetchScalarGridSpec(
        num_scalar_prefetch=len(plan), grid=(d // tn, n_rows // tm),
        in_specs=[pl.BlockSpec((tm, f), lambda j, m, *_: (m, 0)),
                  pl.BlockSpec(memory_space=pl.ANY)],
        out_specs=pl.BlockSpec((tm, tn), lambda j, m, *_: (m, j)),
        scratch_shapes=[pltpu.VMEM((f, tn), F32), pltpu.VMEM((f, tn), BF16),
                        pltpu.SemaphoreType.DMA((1,))])
    return pl.pallas_call(
        _moe_down_kernel, grid_spec=grid_spec,
        out_shape=jax.ShapeDtypeStruct((n_rows, d), F32),
        compiler_params=_cparams(2), name="moe_down")(*plan, g, w2)


def _combine_kernel(x_ref, ya_ref, yb_ref, route_ref, gate_ref, *rest):
    ga = route_ref[:, TOP_K:TOP_K + 1]
    gb = route_ref[:, TOP_K + 1:TOP_K + 2]
    v = x_ref[...] + gate_ref[...] * (ga * ya_ref[...] + gb * yb_ref[...])
    if len(rest) == 2:
        g_ref, o_ref = rest
        o_ref[...] = _rms(v) * g_ref[...]
    else:
        rest[0][...] = v


def _moe_combine(x, ya, yb, route, mods3, layer, n_prompt_rows, dec_seq, row0, n_rows, final_g):
    d = x.shape[1]
    tm = 512
    off = row0 // tm
    base = layer * MOD_ROWS * 6
    row = pl.BlockSpec((tm, d), lambda m: (m + off, 0))
    in_specs = [row, row, row, pl.BlockSpec((tm, LANE), lambda m: (m + off, 0)),
                pl.BlockSpec((None, 1, d),
                             lambda m: (base + _mod_row(m + off, tm, n_prompt_rows, dec_seq) * 6 + 5, 0, 0))]
    args = [x, ya, yb, route, mods3]
    if final_g is not None:
        in_specs.append(pl.BlockSpec((1, d), lambda m: (0, 0)))
        args.append(final_g.reshape(1, d))
    return pl.pallas_call(
        _combine_kernel, grid=(n_rows // tm,), in_specs=in_specs,
        out_specs=pl.BlockSpec((tm, d), lambda m: (m, 0)),
        out_shape=jax.ShapeDtypeStruct((n_rows, d), F32),
        compiler_params=_cparams(1), name="moe_combine")(*args)


def _take_rows(a, idx):
    return a.at[idx].get(mode="promise_in_bounds")


def _moe_ffn(x, h, route, w1, w3, w2, mods3, layer, n_prompt_rows, dec_seq, final_g=None):
    m_rows, d = x.shape
    tm = MOE_TM
    n_slots = m_rows * TOP_K
    n_rows = n_slots + N_EXPERTS * tm
    n_tiles = n_rows // tm
    top_i = route[:, 0:TOP_K].astype(jnp.int32)
    flat_e = top_i.reshape(n_slots)
    onehot = (flat_e[:, None] == jnp.arange(N_EXPERTS, dtype=jnp.int32)[None, :]).astype(jnp.int32)
    counts = jnp.sum(onehot, axis=0)
    rank = jnp.sum((jnp.cumsum(onehot, axis=0) - onehot) * onehot, axis=1)
    padded = ((counts + tm - 1) // tm) * tm
    group_end = jnp.cumsum(padded)
    group_start = group_end - padded
    pos = group_start[flat_e] + rank
    src_token = (jnp.arange(n_rows, dtype=jnp.int32) % m_rows).at[pos].set(
        jnp.arange(n_slots, dtype=jnp.int32) // TOP_K, unique_indices=True, mode="promise_in_bounds")
    n_used = (group_end[-1] // tm).astype(jnp.int32)
    tiles = jnp.arange(n_tiles, dtype=jnp.int32)
    tile_start = jnp.minimum(tiles, n_used - 1) * tm
    tile_expert = jnp.sum((tile_start[:, None] >= group_end[None, :]).astype(jnp.int32), axis=1)
    tile_expert = jnp.minimum(tile_expert, N_EXPERTS - 1).astype(jnp.int32)
    prev_expert = jnp.concatenate([jnp.full((1,), -1, jnp.int32), tile_expert[:-1]])
    first = ((tile_expert != prev_expert) & (tiles < n_used)).astype(jnp.int32)
    ids = jnp.arange(N_EXPERTS, dtype=jnp.int32)
    later = jnp.where((ids[None, :] > ids[:, None]) & (counts[None, :] > 0), ids[None, :], N_EXPERTS)
    next_expert = jnp.min(later, axis=1)
    is_last = next_expert == N_EXPERTS
    next_expert = jnp.where(is_last, tile_expert[0], next_expert)
    live = jnp.clip(group_start[tile_expert] + counts[tile_expert] - tiles * tm, 0, tm)
    live = jnp.where(tiles < n_used, live, 0).astype(jnp.int32)
    plan = (tile_expert, first, next_expert[tile_expert], is_last.astype(jnp.int32)[tile_expert], live)
    xs = _take_rows(h, src_token)
    g = _moe_up(xs, w1, w3, plan)
    ys = _moe_down(g, w2, plan)
    pos2 = pos.reshape(m_rows, TOP_K)
    ya = _take_rows(ys, pos2[:, 0])
    yb = _take_rows(ys, pos2[:, 1])
    common = (x, ya, yb, route, mods3, layer, n_prompt_rows, dec_seq)
    if final_g is None:
        return _moe_combine(*common, 0, m_rows, None)
    return (_moe_combine(*common, 0, n_prompt_rows, final_g),
            _moe_combine(*common, n_prompt_rows, m_rows - n_prompt_rows, final_g))


def _conv_cols(x_ref, w_ref, b_ref, o_ref, seg, act):
    length, ch = x_ref.shape
    t = lax.broadcasted_iota(jnp.int32, (length, LANE), 0) % seg

    def body(cb, carry):
        c0 = pl.multiple_of(cb * LANE, LANE)
        x = x_ref[:, pl.ds(c0, LANE)]
        w = w_ref[:, pl.ds(c0, LANE)]
        xm1 = jnp.where(t >= 1, pltpu.roll(x, 1, 0), 0.0)
        xp1 = jnp.where(t < seg - 1, pltpu.roll(x, length - 1, 0), 0.0)
        xp2 = jnp.where(t < seg - 2, pltpu.roll(x, length - 2, 0), 0.0)
        y = b_ref[:, pl.ds(c0, LANE)] + (w[0:1] * xm1 + w[1:2] * x + w[2:3] * xp1 + w[3:4] * xp2)
        if act:
            y = _silu(y)
        o_ref[:, pl.ds(c0, LANE)] = y
        return carry

    lax.fori_loop(0, ch // LANE, body, 0)


def _tri_masks(t_len):
    r = lax.broadcasted_iota(jnp.int32, (t_len, t_len), 0)
    c = lax.broadcasted_iota(jnp.int32, (t_len, t_len), 1)
    return r >= c, r <= c


def _mixer_call(body, name, proj, seq_cols, consts, init, st_shape, width, scratch, *, n_seq, length,
                row_off, n_rows_total, y_prev, emit_state, prev_state=None, **static):
    st_nd = len(st_shape)
    sps = static.get('sps')
    seq_dim = None if sps is None else sps
    sps = 1 if sps is None else sps
    assert n_seq % sps == 0 and row_off % sps == 0
    rows = sps * length
    row_off = row_off // sps
    n_steps = n_seq // sps
    n_fill = 0
    if y_prev is None:
        assert row_off == 0 and n_rows_total % rows == 0
        n_fill = n_rows_total // rows - n_steps
    own = lambda b: jnp.minimum(b, n_steps - 1)
    in_specs = [pl.BlockSpec((rows, w), lambda b, cb=cb: (own(b) + row_off, cb)) for w, cb in seq_cols]
    args = [proj] * len(seq_cols)
    for c in consts:
        in_specs.append(pl.BlockSpec(c.shape, lambda b, nd=c.ndim: (0,) * nd))
        args.append(c)
    if init is not None:
        in_specs.append(pl.BlockSpec((seq_dim,) + st_shape, lambda b: (own(b),) + (0,) * st_nd))
        args.append(init)
    n_prev = 0 if prev_state is None else prev_state.shape[1]
    if n_prev:
        in_specs.append(pl.BlockSpec((seq_dim, n_prev) + st_shape, lambda b: (own(b),) + (0,) * (st_nd + 1)))
        args.append(prev_state)
    aliases = {}
    n_alias = 0
    if y_prev is not None:
        aliases[len(args)] = 0
        in_specs.append(pl.BlockSpec(memory_space=pl.ANY))
        args.append(y_prev)
        n_alias += 1
    out_specs = [pl.BlockSpec((rows, width), lambda b: (b + row_off, 0))]
    out_shape = [jax.ShapeDtypeStruct((n_rows_total, width), BF16)]
    if emit_state:
        out_specs.append(pl.BlockSpec((seq_dim, n_prev + 1) + st_shape, lambda b: (own(b),) + (0,) * (st_nd + 1)))
        out_shape.append(jax.ShapeDtypeStruct((n_seq, n_prev + 1) + st_shape, F32))
        static = dict(static, st_layer=n_prev)
    layout = dict(n_seq_in=len(seq_cols), n_const=len(consts), has_init=init is not None, n_alias=n_alias,
                  emit_state=emit_state)
    prev_pos = len(seq_cols) + len(consts) + int(init is not None)

    def kern(*refs):
        step = pl.program_id(0)
        body_refs = refs[:prev_pos] + refs[prev_pos + 1:] if n_prev else refs

        @pl.when(step < n_steps)
        def _():
            if n_prev:
                st_out = _split_refs(body_refs, **layout)[4]
                if seq_dim is None:
                    st_out[0:n_prev] = refs[prev_pos][...]
                else:
                    st_out[:, 0:n_prev] = refs[prev_pos][...]
            body(*body_refs, **layout, **static)

        if n_fill:
            @pl.when(step >= n_steps)
            def _():
                y_ref = _split_refs(body_refs, **layout)[3]
                y_ref[...] = jnp.zeros_like(y_ref)

    return pl.pallas_call(
        kern, grid=(n_steps + n_fill,), in_specs=in_specs, out_specs=tuple(out_specs),
        out_shape=tuple(out_shape), scratch_shapes=scratch, input_output_aliases=aliases,
        compiler_params=_cparams(1), name=name)(*args)


def _split_refs(refs, n_seq_in, n_const, has_init, n_alias, emit_state):
    seq = refs[:n_seq_in]
    consts = refs[n_seq_in:n_seq_in + n_const]
    pos = n_seq_in + n_const
    init = refs[pos] if has_init else None
    pos += int(has_init) + n_alias
    y_ref = refs[pos]
    st_out = refs[pos + 1] if emit_state else None
    pos += 1 + int(emit_state)
    return seq, consts, init, y_ref, st_out, refs[pos:]


def _ssd_kernel(*refs, seg, st_layer=0, **layout):
    (z_ref, xs_ref, bc_ref, small_ref), consts, h0_ref, y_ref, hout_ref, scratch = _split_refs(refs, **layout)
    wx_ref, bx_ref, wbc_ref, bbc_ref, dtb_ref, nega_ref, dvec_ref, ng_ref = consts
    xc_ref, bcc_ref, yacc_ref, stt_ref = scratch
    length = z_ref.shape[0]
    t_len = SSD_T
    n_chunks = length // t_len
    n_pairs = SSD_H // 2
    _conv_cols(xs_ref, wx_ref, bx_ref, xc_ref, seg, True)
    _conv_cols(bc_ref, wbc_ref, bbc_ref, bcc_ref, seg, True)
    if h0_ref is None:
        stt_ref[...] = jnp.zeros_like(stt_ref)
    else:
        for d in range(2):
            for hp in range(n_pairs):
                pair = jnp.concatenate([h0_ref[d, 2 * hp], h0_ref[d, 2 * hp + 1]], axis=0)
                stt_ref[d, :, hp * LANE:(hp + 1) * LANE] = pair.T
    yacc_ref[...] = xc_ref[...] * dvec_ref[...]
    lower, upper = _tri_masks(t_len)
    tri = (lower.astype(BF16), upper.astype(BF16))
    masks = (lower, upper)
    lane_lo = lax.broadcasted_iota(jnp.int32, (t_len, LANE), 1) < SSD_P
    gn = SSD_N

    def chunk(c, d):
        r0 = pl.multiple_of(c * t_len, t_len)
        dtv = _softplus(small_ref[pl.ds(r0, t_len), :] + dtb_ref[d])
        cs = _exact_lhs_dot(tri[d], nega_ref[d] * dtv)
        cst = cs.T
        dtt = dtv.T
        end = t_len - 1 if d == 0 else 0
        bcv = bcc_ref[pl.ds(r0, t_len), :]
        groups = []
        for g in range(SSD_G):
            bg = bcv[:, g * gn:(g + 1) * gn]
            cg = bcv[:, (SSD_G + g) * gn:(SSD_G + g + 1) * gn]
            groups.append((_bdot_nt(cg, bg), bg.T, cg))
        for hp in range(n_pairs):
            gm, bgt, cg = groups[(2 * hp) // (SSD_H // SSD_G)]
            cols = slice(hp * LANE, (hp + 1) * LANE)
            x = xc_ref[pl.ds(r0, t_len), cols]
            st = stt_ref[d, :, cols]
            x_lo = jnp.where(lane_lo, x, 0.0).astype(BF16)
            x_hi = jnp.where(lane_lo, 0.0, x).astype(BF16)
            s_lo = jnp.where(lane_lo, st, 0.0).astype(BF16)
            s_hi = jnp.where(lane_lo, 0.0, st).astype(BF16)
            intra, carry_in, upd, edec = [], [], [], []
            for h in (2 * hp, 2 * hp + 1):
                colx = jnp.broadcast_to(cs[:, h:h + 1], (t_len, t_len))
                row = cst[h:h + 1, :]
                dtr = dtt[h:h + 1, :]
                cend = cst[h:h + 1, end:end + 1]
                intra.append((jnp.where(masks[d], jnp.exp(colx - row), 0.0) * (gm * dtr)).astype(BF16))
                carry_in.append((cg * jnp.exp(colx)).astype(BF16))
                upd.append((bgt * (jnp.exp(cend - row) * dtr)).astype(BF16))
                edec.append(jnp.exp(cend))
            xblk = jnp.concatenate([x_lo, x_hi], axis=0)
            y = jnp.dot(jnp.concatenate(intra + carry_in, axis=1),
                        jnp.concatenate([xblk, s_lo, s_hi], axis=0), preferred_element_type=F32)
            snew = jnp.dot(jnp.concatenate(upd, axis=1), xblk, preferred_element_type=F32)
            stt_ref[d, :, cols] = st * jnp.where(lane_lo[0:1], edec[0], edec[1]) + snew
            yacc_ref[pl.ds(r0, t_len), cols] += y

    def body(j, carry):
        chunk(j, 0)
        chunk(n_chunks - 1 - j, 1)
        return carry

    lax.fori_loop(0, n_chunks, body, 0)
    if hout_ref is not None:
        for d in range(2):
            for hp in range(n_pairs):
                pair = stt_ref[d, :, hp * LANE:(hp + 1) * LANE].T
                hout_ref[st_layer, d, 2 * hp] = pair[:SSD_P]
                hout_ref[st_layer, d, 2 * hp + 1] = pair[SSD_P:]

    def finish(c, carry):
        r0 = pl.multiple_of(c * t_len, t_len)
        y = yacc_ref[pl.ds(r0, t_len), :] * _silu(z_ref[pl.ds(r0, t_len), :])
        y_ref[pl.ds(r0, t_len), :] = (_rms(y) * ng_ref[...]).astype(y_ref.dtype)
        return carry

    lax.fori_loop(0, n_chunks, finish, 0)


def _ssd_call(proj, p, h0, **where):
    length = where['length']
    dtb = jnp.zeros((2, 1, LANE), F32).at[:, 0, :SSD_H].set(p['ssd_dt_bias'].astype(F32))
    nega = jnp.zeros((2, 1, LANE), F32).at[:, 0, :SSD_H].set(-jnp.exp(p['ssd_A_log'].astype(F32)))
    dvec = jnp.repeat(p['ssd_D'].astype(F32), SSD_P).reshape(1, SSD_D)
    cw = p['ssd_conv_w'].astype(F32)
    cb = p['ssd_conv_b'].astype(F32).reshape(1, SSD_CONV_CH)
    consts = [cw[:, :SSD_D], cb[:, :SSD_D], cw[:, SSD_D:], cb[:, SSD_D:], dtb, nega, dvec,
              p['ssd_norm_g'].astype(F32).reshape(1, SSD_D)]
    seq_cols = [(SSD_D, COL_Z // SSD_D), (SSD_D, COL_XS // SSD_D), (512, COL_BC // 512),
                (LANE, COL_SMALL // LANE)]
    scratch = [pltpu.VMEM((length, SSD_D), F32), pltpu.VMEM((length, 512), F32),
               pltpu.VMEM((length, SSD_D), F32), pltpu.VMEM((2, SSD_N, SSD_D), F32)]
    return _mixer_call(_ssd_kernel, "ssd_mixer", proj, seq_cols, consts, h0, (2, SSD_H, SSD_P, SSD_N),
                       SSD_D, scratch, **where)


def _log_sigmoid(x):
    return jnp.minimum(x, 0.0) - jnp.log1p(jnp.exp(-jnp.abs(x)))


def _gla_kernel(*refs, sps, st_layer=0, **layout):
    (q_ref, k_ref, v_ref, g_ref, small_ref), consts, s0_ref, y_ref, sout_ref, scratch = _split_refs(refs, **layout)
    gw_ref, gb_ref, ng_ref = consts
    oacc_ref, st_ref, bc_ref = scratch
    n_rows = q_ref.shape[0]
    length = n_rows // sps
    t_len = GLA_T
    n_chunks = length // t_len
    st_ref[...] = jnp.zeros_like(st_ref)
    if s0_ref is not None:
        for s in range(sps):
            for d in range(2):
                for h in range(GLA_H):
                    st_ref[s, d, h * GLA_DV:(h + 1) * GLA_DV, h * GLA_DK:(h + 1) * GLA_DK] = s0_ref[s, d, h]
    oacc_ref[...] = jnp.zeros_like(oacc_ref)
    lower, upper = _tri_masks(t_len)
    masks = tuple(jnp.concatenate([m] * GLA_H, axis=0) for m in (lower, upper))
    q_head = (lax.broadcasted_iota(jnp.int32, (GLA_H * t_len, GLA_KW), 0) // t_len
              == lax.broadcasted_iota(jnp.int32, (GLA_H * t_len, GLA_KW), 1) // GLA_DK)
    st_diag = (lax.broadcasted_iota(jnp.int32, (GLA_VW, GLA_KW), 0) // GLA_DV
               == lax.broadcasted_iota(jnp.int32, (GLA_VW, GLA_KW), 1) // GLA_DK)
    scale = GLA_DK ** -0.5

    blk = 256
    rr = lax.broadcasted_iota(jnp.int32, (blk, blk), 0)
    cc = lax.broadcasted_iota(jnp.int32, (blk, blk), 1)
    same_chunk = rr // t_len == cc // t_len
    tri_blk = ((same_chunk & (rr >= cc)).astype(BF16), (same_chunk & (rr <= cc)).astype(BF16))
    gate_w = []
    for d in range(2):
        w1, w2 = _split2(gw_ref[d])
        gate_w.append(jnp.concatenate([w1, w1, w2], axis=0))

    def decay_sums(b, carry):
        r0 = pl.multiple_of(b * blk, blk)
        s1, s2 = _split2(small_ref[pl.ds(r0, blk), :])
        lhs = jnp.concatenate([s1, s2, s1], axis=1)
        for d in range(2):
            logit = jnp.dot(lhs, gate_w[d], preferred_element_type=F32) + gb_ref[d]
            la = _log_sigmoid(logit) / GLA_TAU
            bc3 = jnp.dot(tri_blk[d], jnp.concatenate(_split3(la), axis=1), preferred_element_type=F32)
            bc_ref[d, pl.ds(r0, blk), :] = bc3[:, :GLA_KW] + bc3[:, GLA_KW:2 * GLA_KW] + bc3[:, 2 * GLA_KW:]
        return carry

    lax.fori_loop(0, n_rows // blk, decay_sums, 0)

    def chunk(c, d, s):
        r0 = pl.multiple_of(s * length + c * t_len, t_len)
        bc = bc_ref[d, pl.ds(r0, t_len), :]
        end = t_len - 1 if d == 0 else 0
        mid = bc[t_len // 2:t_len // 2 + 1, :]
        tot = bc[end:end + 1, :]
        q = q_ref[pl.ds(r0, t_len), :] * scale
        k = k_ref[pl.ds(r0, t_len), :]
        v = v_ref[pl.ds(r0, t_len), :].astype(BF16)
        qt = q * jnp.exp(bc - mid)
        kt = k * jnp.exp(mid - bc)
        qs = q * jnp.exp(bc)
        kd = k * jnp.exp(tot - bc)
        q_rows = jnp.where(q_head, jnp.concatenate([qt] * GLA_H, axis=0), 0.0)
        att = jnp.where(masks[d], _bdot_nt(q_rows, kt), 0.0)
        o_all = jnp.dot(att.astype(BF16), v, preferred_element_type=F32)
        o = jnp.concatenate([o_all[h * t_len:(h + 1) * t_len, h * GLA_DV:(h + 1) * GLA_DV]
                             for h in range(GLA_H)], axis=1)
        st = st_ref[s, d]
        o = o + _bdot_nt(qs, st)
        upd = lax.dot_general(v, kd.astype(BF16), (((0,), (0,)), ((), ())), preferred_element_type=F32)
        st_ref[s, d] = st * jnp.exp(tot) + jnp.where(st_diag, upd, 0.0)
        oacc_ref[pl.ds(r0, t_len), :] += o

    def body(j, carry):
        for s in range(sps):
            chunk(j, 0, s)
            chunk(n_chunks - 1 - j, 1, s)
        return carry

    lax.fori_loop(0, n_chunks, body, 0)
    if sout_ref is not None:
        for s in range(sps):
            for d in range(2):
                for h in range(GLA_H):
                    sout_ref[s, st_layer, d, h] = st_ref[s, d, h * GLA_DV:(h + 1) * GLA_DV,
                                                         h * GLA_DK:(h + 1) * GLA_DK]

    def finish(c, carry):
        r0 = pl.multiple_of(c * t_len, t_len)
        gate = _silu(g_ref[pl.ds(r0, t_len), :])
        for h in range(GLA_H):
            vs = slice(h * GLA_DV, (h + 1) * GLA_DV)
            o = _rms(oacc_ref[pl.ds(r0, t_len), vs]) * ng_ref[...]
            y_ref[pl.ds(r0, t_len), vs] = (o * gate[:, vs]).astype(y_ref.dtype)
        return carry

    lax.fori_loop(0, sps * n_chunks, finish, 0)


def _gla_call(proj, p, s0t, **where):
    length = where['length']
    sps = where['sps']
    gw = jnp.zeros((2, LANE, GLA_KW), F32).at[:, GLA_RANK:2 * GLA_RANK, :].set(p['gla_gate_w'].astype(F32))
    consts = [gw, p['gla_gate_b'].astype(F32).reshape(2, 1, GLA_KW),
              p['gla_norm_g'].astype(F32).reshape(1, GLA_DV)]
    seq_cols = [(GLA_KW, COL_Q // GLA_KW), (GLA_KW, COL_K // GLA_KW), (GLA_VW, COL_V // GLA_VW),
                (GLA_VW, COL_G // GLA_VW), (LANE, COL_SMALL // LANE)]
    st_shape = (2, GLA_H, GLA_DV, GLA_DK)
    scratch = [pltpu.VMEM((sps * length, GLA_VW), F32), pltpu.VMEM((sps, 2, GLA_VW, GLA_KW), F32),
               pltpu.VMEM((2, sps * length, GLA_KW), F32)]
    return _mixer_call(_gla_kernel, "gla_mixer", proj, seq_cols, consts, s0t, st_shape, GLA_VW, scratch, **where)


def _lru_kernel(*refs, seg, sps, st_layer=0, **layout):
    (xb_ref, gb_ref), consts, h0_ref, y_ref, hout_ref, scratch = _split_refs(refs, **layout)
    cw_ref, cb_ref, w_ref, bias_ref, sp_ref = consts
    xr_ref, a_ref, u_ref = scratch
    n_rows = xb_ref.shape[0]
    length = n_rows // sps
    blk = 256
    _conv_cols(xb_ref, cw_ref, cb_ref, xr_ref, seg, False)

    def gates(c, carry):
        r0 = pl.multiple_of(c * blk, blk)
        xr = xr_ref[pl.ds(r0, blk), :]
        xbf = xr.astype(BF16)
        for d in range(2):
            pre = jnp.dot(xbf, w_ref[d], preferred_element_type=F32) + bias_ref[d]
            r = jax.nn.sigmoid(pre[:, :LRU_W])
            i = jax.nn.sigmoid(pre[:, LRU_W:])
            log_a = (-LRU_C) * r * sp_ref[d]
            a = jnp.exp(log_a)
            a_ref[d, pl.ds(r0, blk), :] = a
            u_ref[d, pl.ds(r0, blk), :] = jnp.sqrt(-jnp.tanh(log_a) * (a * a + 1.0)) * (i * xr)
        return carry

    lax.fori_loop(0, n_rows // blk, gates, 0)

    def scan(t, carry):
        out = []
        for s in range(sps):
            for d in range(2):
                row = s * length + (t if d == 0 else length - 1 - t)
                h = a_ref[d, pl.ds(row, 1), :] * carry[2 * s + d] + u_ref[d, pl.ds(row, 1), :]
                u_ref[d, pl.ds(row, 1), :] = h
                out.append(h)
        return tuple(out)

    if h0_ref is None:
        start = tuple(jnp.zeros((1, LRU_W), F32) for _ in range(2 * sps))
    else:
        start = tuple(h0_ref[s, d] for s in range(sps) for d in range(2))
    final = lax.fori_loop(0, length, scan, start)
    if hout_ref is not None:
        for s in range(sps):
            for d in range(2):
                hout_ref[s, st_layer, d] = final[2 * s + d]

    def finish(c, carry):
        r0 = pl.multiple_of(c * blk, blk)
        y = (u_ref[0, pl.ds(r0, blk), :] + u_ref[1, pl.ds(r0, blk), :]) * jax.nn.gelu(gb_ref[pl.ds(r0, blk), :])
        y_ref[pl.ds(r0, blk), :] = y.astype(y_ref.dtype)
        return carry

    lax.fori_loop(0, n_rows // blk, finish, 0)


def _block_diag(w):
    nb, bw, _ = w.shape
    eye = jnp.eye(nb, dtype=w.dtype)
    return (eye[:, None, :, None] * w[:, :, None, :]).reshape(nb * bw, nb * bw)


def _lru_call(proj, p, h0, **where):
    length = where['length']
    w = jnp.stack([jnp.concatenate([_block_diag(p['lru_wa'][d].astype(F32)),
                                    _block_diag(p['lru_wx'][d].astype(F32))], axis=1) for d in range(2)])
    bias = jnp.stack([jnp.concatenate([p['lru_ba'][d], p['lru_bx'][d]]) for d in range(2)]).astype(F32)
    sp = jax.nn.softplus(-p['lru_lambda'].astype(F32)).reshape(2, 1, LRU_W)
    consts = [p['lru_conv_w'].astype(F32), p['lru_conv_b'].astype(F32).reshape(1, LRU_W), w.astype(BF16),
              bias.reshape(2, 1, 2 * LRU_W), sp]
    seq_cols = [(LRU_W, COL_XB // LRU_W), (LRU_W, COL_GB // LRU_W)]
    rows = where['sps'] * length
    scratch = [pltpu.VMEM((rows, LRU_W), F32), pltpu.VMEM((2, rows, LRU_W), F32),
               pltpu.VMEM((2, rows, LRU_W), F32)]
    return _mixer_call(_lru_kernel, "lru_mixer", proj, seq_cols, consts, h0, (2, 1, LRU_W), LRU_W, scratch,
                       **where)


def kernel(x_prompt, x_sample, state_ssd, state_gla, state_lru, c, c_ctx, mod_w, mod_b, norm1_g, norm2_g, in_w, ssd_conv_w, ssd_conv_b, ssd_A_log, ssd_dt_bias, ssd_D, ssd_norm_g, gla_gate_w, gla_gate_b, gla_norm_g, lru_conv_w, lru_conv_b, lru_wa, lru_ba, lru_wx, lru_bx, lru_lambda, out_w, ffn_w1, ffn_w3, ffn_w2, moe_router, moe_w1, moe_w3, moe_w2, final_norm_g):
    bp, lp, d = x_prompt.shape
    bs, ls, _ = x_sample.shape
    n_p = bp * lp
    n_s = bs * ls
    n_all = n_p + n_s
    depth = in_w.shape[0]
    assert n_p % ls == 0 and 1 + bs <= MOD_ROWS

    cvec = jnp.zeros((MOD_ROWS, d), F32).at[0].set(c_ctx.astype(F32)).at[1:1 + bs].set(c.astype(F32))
    mods = _adaln(cvec, mod_w, mod_b)
    mods3 = mods.reshape(depth * MOD_ROWS * 6, 1, d)
    x = (x_prompt.reshape(n_p, d).astype(F32), x_sample.reshape(n_s, d).astype(F32))

    in_w_t = jnp.swapaxes(in_w, 1, 2)
    st_ssd = st_gla = st_lru = None
    for i in range(depth):
        p = {'ssd_conv_w': ssd_conv_w[i], 'ssd_conv_b': ssd_conv_b[i], 'ssd_A_log': ssd_A_log[i],
             'ssd_dt_bias': ssd_dt_bias[i], 'ssd_D': ssd_D[i], 'ssd_norm_g': ssd_norm_g[i],
             'gla_gate_w': gla_gate_w[i], 'gla_gate_b': gla_gate_b[i], 'gla_norm_g': gla_norm_g[i],
             'lru_conv_w': lru_conv_w[i], 'lru_conv_b': lru_conv_b[i], 'lru_wa': lru_wa[i],
             'lru_ba': lru_ba[i], 'lru_wx': lru_wx[i], 'lru_bx': lru_bx[i], 'lru_lambda': lru_lambda[i]}
        h = _norm_mod(x, norm1_g[i], mods3, i, 0, n_p, ls)
        proj = _in_proj(h, in_w_t, i)
        ctx = dict(n_seq=bp, length=lp, row_off=0, n_rows_total=n_all, y_prev=None, emit_state=True)
        lat = dict(n_seq=bs, length=ls, row_off=n_p // ls, n_rows_total=n_all, emit_state=False)
        y_ssd, st_ssd = _ssd_call(proj, p, None, seg=lp, prev_state=st_ssd, **ctx)
        y_ssd, = _ssd_call(proj, p, state_ssd[:, i].astype(F32), seg=GRID_W, y_prev=y_ssd, **lat)
        y_gla, st_gla = _gla_call(proj, p, None, sps=MIX_SPS, prev_state=st_gla, **ctx)
        y_gla, = _gla_call(proj, p, jnp.swapaxes(state_gla[:, i].astype(F32), -1, -2), sps=MIX_SPS,
                           y_prev=y_gla, **lat)
        y_lru, st_lru = _lru_call(proj, p, None, seg=lp, sps=MIX_SPS, prev_state=st_lru, **ctx)
        y_lru, = _lru_call(proj, p, state_lru[:, i].astype(F32).reshape(bs, 2, 1, LRU_W), seg=GRID_W,
                           sps=MIX_SPS, y_prev=y_lru, **lat)
        j = i // 2
        if i % 2 == 1:
            x, h2, route = _out_proj(y_ssd, y_gla, y_lru, out_w, x, norm2_g[i], mods3, i, n_p, ls,
                                     router=moe_router[j])
            x = _moe_ffn(x, h2, route, moe_w1[j], moe_w3[j], moe_w2[j], mods3, i, n_p, ls,
                         final_g=final_norm_g if i == depth - 1 else None)
        else:
            x, h2 = _out_proj(y_ssd, y_gla, y_lru, out_w, x, norm2_g[i], mods3, i, n_p, ls)
            x = _ffn_down(_ffn_up(h2, ffn_w1[j], ffn_w3[j]), ffn_w2[j], x, mods3, i, n_p, ls)
    if isinstance(x, tuple):
        y_p, y_s = x
    else:
        y_p = _final_norm(x, final_norm_g, 0, n_p)
        y_s = _final_norm(x, final_norm_g, n_p, n_s)
    return (y_p.reshape(bp, lp, d), y_s.reshape(bs, ls, d), st_ssd, jnp.swapaxes(st_gla, -1, -2),
            st_lru.reshape(bp, depth, 2, LRU_W))
```

```python
import functools

import jax
import jax.numpy as jnp
from jax import lax
from jax.experimental import pallas as pl
from jax.experimental.pallas import tpu as pltpu

F32 = jnp.float32
BF16 = jnp.bfloat16

D_MODEL = 2048
GRID_W = 64
SSD_D = D_MODEL // 2
SSD_P = 64
SSD_H = SSD_D // SSD_P
SSD_G = 2
SSD_N = 128
GLA_H = 4
GLA_VW = D_MODEL // 4
GLA_DV = GLA_VW // GLA_H
GLA_DK = GLA_DV // 2
GLA_KW = GLA_H * GLA_DK
GLA_RANK = 16
GLA_TAU = 16.0
LRU_W = D_MODEL // 4
LRU_NB = 8
LRU_BW = LRU_W // LRU_NB
LRU_C = 8.0
SSD_CONV_CH = SSD_D + 2 * SSD_G * SSD_N
N_EXPERTS = 8
TOP_K = 2
EPS = 1e-6

LANE = 128
SSD_T = 128
GLA_T = 64
MIX_SPS = 2
MOD_ROWS = 8
VMEM_LIMIT = 56 * 1024 * 1024

COL_Z = 0
COL_XS = 1024
COL_BC = 2048
COL_Q = 2560
COL_K = 2816
COL_V = 3072
COL_G = 3584
COL_XB = 4096
COL_GB = 4608
COL_SMALL = 5120
IN_PAD = 5632


def _cparams(n_axes):
    return pltpu.CompilerParams(dimension_semantics=("arbitrary",) * n_axes,
                                vmem_limit_bytes=VMEM_LIMIT)


def _bdot(a, b):
    return jnp.dot(a.astype(BF16), b.astype(BF16), preferred_element_type=F32)


def _bdot_nt(a, b):
    return lax.dot_general(a.astype(BF16), b.astype(BF16), (((1,), (1,)), ((), ())),
                           preferred_element_type=F32)


def _bdot_tn(a, b):
    return lax.dot_general(a.astype(BF16), b.astype(BF16), (((0,), (0,)), ((), ())),
                           preferred_element_type=F32)


def _split2(a):
    a1 = a.astype(BF16)
    a2 = (a - a1.astype(F32)).astype(BF16)
    return a1, a2


def _split3(a):
    a1 = a.astype(BF16)
    r = a - a1.astype(F32)
    a2 = r.astype(BF16)
    a3 = (r - a2.astype(F32)).astype(BF16)
    return a1, a2, a3


def _exact_lhs_dot(m_bf16, a):
    a1, a2, a3 = _split3(a)
    f = lambda z: jnp.dot(m_bf16, z, preferred_element_type=F32)
    return f(a1) + f(a2) + f(a3)


def _dot3(a, b):
    a1, a2 = _split2(a)
    b1, b2 = _split2(b)
    f = lambda x, y: jnp.dot(x, y, preferred_element_type=F32)
    return f(a1, b1) + (f(a1, b2) + f(a2, b1))


def _softplus(x):
    return jnp.maximum(x, 0.0) + jnp.log1p(jnp.exp(-jnp.abs(x)))


def _silu(x):
    return x * jax.nn.sigmoid(x)


def _mod_row(m, tm, n_prompt_rows, dec_seq):
    r0 = m * tm
    return jnp.where(r0 < n_prompt_rows, 0, 1 + (r0 - n_prompt_rows) // dec_seq)


def _adaln_kernel(c_ref, w_ref, b_ref, o_ref):
    s = _silu(c_ref[...])
    o_ref[...] = _dot3(s, w_ref[...]) + b_ref[...]


def _adaln(cvec8, mod_w, mod_b):
    depth, d, n = mod_w.shape
    tn = 1024
    return pl.pallas_call(
        _adaln_kernel,
        grid=(depth, n // tn),
        in_specs=[pl.BlockSpec((MOD_ROWS, d), lambda i, j: (0, 0)),
                  pl.BlockSpec((None, d, tn), lambda i, j: (i, 0, j)),
                  pl.BlockSpec((None, 1, tn), lambda i, j: (i, 0, j))],
        out_specs=pl.BlockSpec((None, MOD_ROWS, tn), lambda i, j: (i, 0, j)),
        out_shape=jax.ShapeDtypeStruct((depth, MOD_ROWS, n), F32),
        compiler_params=_cparams(2),
        name="adaln",
    )(cvec8, mod_w, mod_b.reshape(depth, 1, n))


def _rms(x):
    return x * lax.rsqrt(jnp.mean(x * x, axis=-1, keepdims=True) + EPS)


def _stacked_specs(x, block, row_tile, col_block):
    if not isinstance(x, tuple):
        return [pl.BlockSpec(block, lambda *g: (row_tile(*g), col_block(*g)))], [x], 0
    split = x[0].shape[0] // block[0]
    specs = [pl.BlockSpec(block, lambda *g: (jnp.minimum(row_tile(*g), split - 1), col_block(*g))),
             pl.BlockSpec(block, lambda *g: (jnp.maximum(row_tile(*g) - split, 0), col_block(*g)))]
    return specs, list(x), split


def _stacked_tile(x_refs, m, split):
    if len(x_refs) == 1:
        return x_refs[0][...]
    return jnp.where(m < split, x_refs[0][...], x_refs[1][...])


def _norm_mod_kernel(*refs, split):
    g_ref, sh_ref, sc_ref, o_ref = refs[-4:]
    y = _rms(_stacked_tile(refs[:-4], pl.program_id(0), split)) * g_ref[...]
    o_ref[...] = (y * (1.0 + sc_ref[...]) + sh_ref[...]).astype(o_ref.dtype)


def _top2_route(h, router_pad):
    logits = _dot3(h, router_pad)
    lane = lax.broadcasted_iota(jnp.int32, logits.shape, 1)
    neg = jnp.float32(-jnp.inf)
    lg = jnp.where(lane < N_EXPERTS, logits, neg)
    m1 = jnp.max(lg, axis=-1, keepdims=True)
    i1 = jnp.min(jnp.where(lg == m1, lane, LANE), axis=-1, keepdims=True)
    lg2 = jnp.where(lane == i1, neg, lg)
    m2 = jnp.max(lg2, axis=-1, keepdims=True)
    i2 = jnp.min(jnp.where(lg2 == m2, lane, LANE), axis=-1, keepdims=True)
    e2 = jnp.exp(m2 - m1)
    den = 1.0 + e2
    g1 = 1.0 / den
    g2 = e2 / den
    return jnp.where(lane == 0, i1.astype(F32),
                     jnp.where(lane == 1, i2.astype(F32),
                               jnp.where(lane == 2, g1, jnp.where(lane == 3, g2, 0.0))))


def _norm_mod(x, g, mods3, layer, which, n_prompt_rows, dec_seq):
    d = g.shape[0]
    m_rows = sum(part.shape[0] for part in x) if isinstance(x, tuple) else x.shape[0]
    tm = 512
    base = layer * MOD_ROWS * 6

    def mod_spec(k):
        return pl.BlockSpec((None, 1, d),
                            lambda m: (base + _mod_row(m, tm, n_prompt_rows, dec_seq) * 6 + k, 0, 0))

    x_specs, x_args, split = _stacked_specs(x, (tm, d), lambda m: m, lambda m: 0)
    in_specs = x_specs + [pl.BlockSpec((1, d), lambda m: (0, 0)), mod_spec(3 * which), mod_spec(3 * which + 1)]
    args = x_args + [g.reshape(1, d), mods3, mods3]
    return pl.pallas_call(
        functools.partial(_norm_mod_kernel, split=split), grid=(m_rows // tm,), in_specs=in_specs,
        out_specs=pl.BlockSpec((tm, d), lambda m: (m, 0)), out_shape=jax.ShapeDtypeStruct((m_rows, d), BF16),
        compiler_params=_cparams(1), name="norm_mod")(*args)


def _final_norm_kernel(x_ref, g_ref, o_ref):
    o_ref[...] = _rms(x_ref[...]) * g_ref[...]


def _final_norm(x, g, row0, n_rows):
    d = x.shape[1]
    tm = 512
    off = row0 // tm
    return pl.pallas_call(
        _final_norm_kernel, grid=(n_rows // tm,),
        in_specs=[pl.BlockSpec((tm, d), lambda m: (m + off, 0)), pl.BlockSpec((1, d), lambda m: (0, 0))],
        out_specs=pl.BlockSpec((tm, d), lambda m: (m, 0)),
        out_shape=jax.ShapeDtypeStruct((n_rows, d), F32),
        compiler_params=_cparams(1), name="final_norm")(x, g.reshape(1, d))


IN_TN = 512
IN_DT_COL = SSD_D + SSD_CONV_CH
IN_GRAW_COL = IN_DT_COL + SSD_H + 2 * GLA_KW + 2 * GLA_VW


IN_UNITS = 2


def _in_proj_kernel(x_ref, *rest):
    unit_refs = [rest[2 * u:2 * u + 2] for u in range(IN_UNITS)]
    wdt_ref, wgr_ref, o_ref, wbf_ref = rest[2 * IN_UNITS:]
    j = pl.program_id(0)
    small_unit = COL_SMALL // IN_TN

    @pl.when(pl.program_id(1) == 0)
    def _():
        for u, (wm_ref, wn_ref) in enumerate(unit_refs):
            unit = j * IN_UNITS + u
            rows = slice(u * IN_TN, (u + 1) * IN_TN)

            def shifted(s, wm_ref=wm_ref, wn_ref=wn_ref, rows=rows):
                w = jnp.concatenate([wm_ref[...], wn_ref[...]], axis=0)
                wbf_ref[rows] = w[s:s + IN_TN].astype(BF16)

            @pl.when(unit < COL_Q // IN_TN)
            def _(wm_ref=wm_ref, rows=rows):
                wbf_ref[rows] = wm_ref[...].astype(BF16)

            @pl.when((unit >= COL_Q // IN_TN) & (unit < COL_XB // IN_TN))
            def _(shifted=shifted):
                shifted(SSD_H)

            @pl.when((unit >= COL_XB // IN_TN) & (unit < small_unit))
            def _(shifted=shifted):
                shifted(SSD_H + GLA_RANK)

            @pl.when(unit >= small_unit)
            def _(rows=rows):
                wbf_ref[rows] = jnp.zeros((IN_TN, wbf_ref.shape[1]), BF16)

            @pl.when(unit == small_unit)
            def _(u=u):
                wbf_ref[u * IN_TN:u * IN_TN + SSD_H] = wdt_ref[...].astype(BF16)
                wbf_ref[u * IN_TN + SSD_H:u * IN_TN + SSD_H + GLA_RANK] = wgr_ref[...].astype(BF16)

    nt = (((1,), (1,)), ((), ()))
    has_small = (j + 1) * IN_UNITS > small_unit

    @pl.when(jnp.logical_not(has_small))
    def _():
        o_ref[...] = lax.dot_general(x_ref[...], wbf_ref[...], nt, preferred_element_type=F32)

    @pl.when(has_small)
    def _():
        full = (small_unit % IN_UNITS) * IN_TN
        o_ref[:, :full + LANE] = lax.dot_general(x_ref[...], wbf_ref[0:full + LANE], nt,
                                                 preferred_element_type=F32)
        o_ref[:, full + LANE:] = jnp.zeros((o_ref.shape[0], o_ref.shape[1] - full - LANE), F32)


def _in_proj(h, in_w_t, layer):
    m_rows, k = h.shape
    tm, tn = 1024, IN_TN * IN_UNITS
    n_main = COL_SMALL // IN_TN
    shift_max = SSD_H + GLA_RANK
    last_next_block = in_w_t.shape[1] // shift_max - 1
    unit_specs = []
    for u in range(IN_UNITS):
        unit_specs += [
            pl.BlockSpec((None, IN_TN, k),
                         lambda j, m, u=u: (layer, jnp.minimum(j * IN_UNITS + u, n_main - 1), 0)),
            pl.BlockSpec((None, shift_max, k),
                         lambda j, m, u=u: (layer, jnp.minimum((j * IN_UNITS + u + 1) * (IN_TN // shift_max),
                                                               last_next_block), 0))]
    return pl.pallas_call(
        _in_proj_kernel, grid=(pl.cdiv(IN_PAD, tn), m_rows // tm),
        in_specs=[pl.BlockSpec((tm, k), lambda j, m: (m, 0))] + unit_specs + [
                  pl.BlockSpec((None, SSD_H, k), lambda j, m: (layer, IN_DT_COL // SSD_H, 0)),
                  pl.BlockSpec((None, GLA_RANK, k), lambda j, m: (layer, IN_GRAW_COL // GLA_RANK, 0))],
        out_specs=pl.BlockSpec((tm, tn), lambda j, m: (m, j)),
        out_shape=jax.ShapeDtypeStruct((m_rows, IN_PAD), F32),
        scratch_shapes=[pltpu.VMEM((tn, k), BF16)],
        compiler_params=_cparams(2), name="in_proj")(h, *([in_w_t] * (2 * IN_UNITS + 2)))


def _out_proj_kernel(y1_ref, y2_ref, y3_ref, w_hbm, *rest, n_res, split, layer, route):
    res_refs = rest[:n_res]
    gate_ref, g_ref, sh_ref, sc_ref = rest[n_res:n_res + 4]
    pos = n_res + 4
    r_ref = rest[pos] if route else None
    pos += int(route)
    x_out, h_out = rest[pos:pos + 2]
    pos += 2
    route_out = rest[pos] if route else None
    pos += int(route)
    wst_ref, wbf_ref, sem = rest[pos:]
    m = pl.program_id(0)

    @pl.when(m == 0)
    def _():
        copy = pltpu.make_async_copy(w_hbm.at[layer], wst_ref, sem.at[0])
        copy.start()
        copy.wait()
        wbf_ref[...] = wst_ref[...].astype(BF16)

    k1 = y1_ref.shape[1]
    k2 = k1 + y2_ref.shape[1]
    acc = jnp.dot(y1_ref[...], wbf_ref[0:k1, :], preferred_element_type=F32)
    acc += jnp.dot(y2_ref[...], wbf_ref[k1:k2, :], preferred_element_type=F32)
    acc += jnp.dot(y3_ref[...], wbf_ref[k2:, :], preferred_element_type=F32)
    x_new = _stacked_tile(res_refs, m, split) + gate_ref[...] * acc
    x_out[...] = x_new
    h = (_rms(x_new) * g_ref[...]) * (1.0 + sc_ref[...]) + sh_ref[...]
    h_out[...] = h.astype(h_out.dtype)
    if route:
        route_out[...] = _top2_route(h, r_ref[...])


def _out_proj(y1, y2, y3, w, res, norm_g, mods3, layer, n_prompt_rows, dec_seq, router=None):
    m_rows = y1.shape[0]
    k, d = w.shape[1:]
    tm = 256
    base = layer * MOD_ROWS * 6
    route = router is not None

    def mod_spec(which):
        return pl.BlockSpec((None, 1, d),
                            lambda m: (base + _mod_row(m, tm, n_prompt_rows, dec_seq) * 6 + which, 0, 0))

    row = lambda width: pl.BlockSpec((tm, width), lambda m: (m, 0))
    res_specs, res_args, split = _stacked_specs(res, (tm, d), lambda m: m, lambda m: 0)
    in_specs = [row(y1.shape[1]), row(y2.shape[1]), row(y3.shape[1]), pl.BlockSpec(memory_space=pl.ANY)]
    in_specs += res_specs + [mod_spec(2), pl.BlockSpec((1, d), lambda m: (0, 0)), mod_spec(3), mod_spec(4)]
    args = [y1, y2, y3, w] + res_args + [mods3, norm_g.reshape(1, d), mods3, mods3]
    out_specs = [row(d), row(d)]
    out_shape = [jax.ShapeDtypeStruct((m_rows, d), F32), jax.ShapeDtypeStruct((m_rows, d), F32 if route else BF16)]
    if route:
        in_specs.append(pl.BlockSpec((d, LANE), lambda m: (0, 0)))
        args.append(jnp.zeros((d, LANE), F32).at[:, :N_EXPERTS].set(router.astype(F32)))
        out_specs.append(row(LANE))
        out_shape.append(jax.ShapeDtypeStruct((m_rows, LANE), F32))
    kern = functools.partial(_out_proj_kernel, n_res=len(res_args), split=split, layer=layer, route=route)
    return pl.pallas_call(
        kern, grid=(m_rows // tm,), in_specs=in_specs, out_specs=tuple(out_specs), out_shape=tuple(out_shape),
        scratch_shapes=[pltpu.VMEM((k, d), F32), pltpu.VMEM((k, d), BF16), pltpu.SemaphoreType.DMA((1,))],
        compiler_params=_cparams(1), name="out_proj")(*args)


def _ffn_up_kernel(x_ref, w1_ref, w3_ref, o_ref, w1bf_ref, w3bf_ref):
    @pl.when(pl.program_id(1) == 0)
    def _():
        w1bf_ref[...] = w1_ref[...].astype(BF16)
        w3bf_ref[...] = w3_ref[...].astype(BF16)

    a = jnp.dot(x_ref[...], w1bf_ref[...], preferred_element_type=F32)
    b = jnp.dot(x_ref[...], w3bf_ref[...], preferred_element_type=F32)
    o_ref[...] = (_silu(a) * b).astype(o_ref.dtype)


def _ffn_up(h, w1, w3):
    m_rows, d = h.shape
    f = w1.shape[1]
    tm, tf = 1024, 512
    return pl.pallas_call(
        _ffn_up_kernel, grid=(pl.cdiv(f, tf), m_rows // tm),
        in_specs=[pl.BlockSpec((tm, d), lambda j, m: (m, 0)),
                  pl.BlockSpec((d, tf), lambda j, m: (0, j)),
                  pl.BlockSpec((d, tf), lambda j, m: (0, j))],
        out_specs=pl.BlockSpec((tm, tf), lambda j, m: (m, j)),
        out_shape=jax.ShapeDtypeStruct((m_rows, f), BF16),
        scratch_shapes=[pltpu.VMEM((d, tf), BF16), pltpu.VMEM((d, tf), BF16)],
        compiler_params=_cparams(2), name="ffn_up")(h, w1, w3)


def _ffn_down_kernel(g_ref, w_ref, res_ref, gate_ref, o_ref, wbf_ref):
    @pl.when(pl.program_id(1) == 0)
    def _():
        wbf_ref[...] = w_ref[...].astype(BF16)

    acc = jnp.dot(g_ref[...], wbf_ref[...], preferred_element_type=F32)
    o_ref[...] = res_ref[...] + gate_ref[...] * acc


def _ffn_down(g, w2, res, mods3, layer, n_prompt_rows, dec_seq):
    m_rows, d = res.shape
    f = w2.shape[0]
    tm, tn = 512, 512
    base = layer * MOD_ROWS * 6
    return pl.pallas_call(
        _ffn_down_kernel, grid=(d // tn, m_rows // tm),
        in_specs=[pl.BlockSpec((tm, f), lambda j, m: (m, 0)),
                  pl.BlockSpec((f, tn), lambda j, m: (0, j)),
                  pl.BlockSpec((tm, tn), lambda j, m: (m, j)),
                  pl.BlockSpec((None, 1, tn),
                               lambda j, m: (base + _mod_row(m, tm, n_prompt_rows, dec_seq) * 6 + 5, 0, j))],
        out_specs=pl.BlockSpec((tm, tn), lambda j, m: (m, j)),
        out_shape=jax.ShapeDtypeStruct((m_rows, d), F32),
        scratch_shapes=[pltpu.VMEM((f, tn), BF16)],
        compiler_params=_cparams(2), name="ffn_down")(g, w2, res, mods3)


MOE_TM = 256
MOE_ROW_QUARTERS = 4


def _grouped_weights(plan_refs, w_hbm, stages, casts, sems):
    te_ref, first_ref, next_e_ref, last_ref = plan_refs
    j = pl.program_id(0)
    m = pl.program_id(1)
    n_col_tiles = pl.num_programs(0)

    def fetch(e, jj, k):
        width = stages[k].shape[1]
        src = w_hbm[k].at[e, :, pl.ds(pl.multiple_of(jj * width, width), width)]
        return pltpu.make_async_copy(src, stages[k], sems.at[k])

    @pl.when((j == 0) & (m == 0))
    def _():
        for k in range(len(stages)):
            fetch(te_ref[0], 0, k).start()

    @pl.when(first_ref[m] == 1)
    def _():
        for k in range(len(stages)):
            fetch(te_ref[m], j, k).wait()
            casts[k][...] = stages[k][...].astype(BF16)
        next_j = j + last_ref[m]

        @pl.when(next_j < n_col_tiles)
        def _():
            for k in range(len(stages)):
                fetch(next_e_ref[m], next_j, k).start()


def _by_live_rows(live, o_ref, compute):
    rows = o_ref.shape[0]
    step = rows // MOE_ROW_QUARTERS
    for q in range(1, MOE_ROW_QUARTERS + 1):
        top = q * step

        @pl.when((live > top - step) & (live <= top))
        def _(top=top):
            o_ref[0:top] = compute(slice(0, top))
            if top < rows:
                o_ref[top:] = jnp.zeros((rows - top, o_ref.shape[1]), o_ref.dtype)

    @pl.when(live == 0)
    def _():
        o_ref[...] = jnp.zeros_like(o_ref)


def _moe_up_kernel(te_ref, first_ref, next_e_ref, last_ref, live_ref, x_ref, w1_hbm, w3_hbm, o_ref,
                   w1st_ref, w3st_ref, w1bf_ref, w3bf_ref, sems):
    _grouped_weights((te_ref, first_ref, next_e_ref, last_ref), (w1_hbm, w3_hbm), (w1st_ref, w3st_ref),
                     (w1bf_ref, w3bf_ref), sems)

    def compute(rows):
        x = x_ref[rows].astype(BF16)
        a = jnp.dot(x, w1bf_ref[...], preferred_element_type=F32)
        b = jnp.dot(x, w3bf_ref[...], preferred_element_type=F32)
        return (_silu(a) * b).astype(o_ref.dtype)

    _by_live_rows(live_ref[pl.program_id(1)], o_ref, compute)


def _moe_up(xs, w1, w3, plan):
    n_rows, d = xs.shape
    f = w1.shape[2]
    tm, tf = MOE_TM, 1792
    n_plan = len(plan)
    grid_spec = pltpu.PrefetchScalarGridSpec(
        num_scalar_prefetch=n_plan, grid=(f // tf, n_rows // tm),
        in_specs=[pl.BlockSpec((tm, d), lambda j, m, *_: (m, 0)),
                  pl.BlockSpec(memory_space=pl.ANY), pl.BlockSpec(memory_space=pl.ANY)],
        out_specs=pl.BlockSpec((tm, tf), lambda j, m, *_: (m, j)),
        scratch_shapes=[pltpu.VMEM((d, tf), F32), pltpu.VMEM((d, tf), F32),
                        pltpu.VMEM((d, tf), BF16), pltpu.VMEM((d, tf), BF16),
                        pltpu.SemaphoreType.DMA((2,))])
    return pl.pallas_call(
        _moe_up_kernel, grid_spec=grid_spec,
        out_shape=jax.ShapeDtypeStruct((n_rows, f), BF16),
        compiler_params=_cparams(2), name="moe_up")(*plan, xs, w1, w3)


def _moe_down_kernel(te_ref, first_ref, next_e_ref, last_ref, live_ref, g_ref, w_hbm, o_ref,
                     wst_ref, wbf_ref, sems):
    _grouped_weights((te_ref, first_ref, next_e_ref, last_ref), (w_hbm,), (wst_ref,), (wbf_ref,), sems)

    def compute(rows):
        return jnp.dot(g_ref[rows], wbf_ref[...], preferred_element_type=F32)

    _by_live_rows(live_ref[pl.program_id(1)], o_ref, compute)


def _moe_down(g, w2, plan):
    n_rows, f = g.shape
    d = w2.shape[2]
    tm, tn = MOE_TM, 1024
    grid_spec = pltpu.PrefetchScalarGridSpec(
        num_scalar_prefetch=len(plan), grid=(d // tn, n_rows // tm),
        in_specs=[pl.BlockSpec((tm, f), lambda j, m, *_: (m, 0)),
                  pl.BlockSpec(memory_space=pl.ANY)],
        out_specs=pl.BlockSpec((tm, tn), lambda j, m, *_: (m, j)),
        scratch_shapes=[pltpu.VMEM((f, tn), F32), pltpu.VMEM((f, tn), BF16),
                        pltpu.SemaphoreType.DMA((1,))])
    return pl.pallas_call(
        _moe_down_kernel, grid_spec=grid_spec,
        out_shape=jax.ShapeDtypeStruct((n_rows, d), F32),
        compiler_params=_cparams(2), name="moe_down")(*plan, g, w2)


def _combine_kernel(x_ref, ya_ref, yb_ref, route_ref, gate_ref, *rest):
    ga = route_ref[:, TOP_K:TOP_K + 1]
    gb = route_ref[:, TOP_K + 1:TOP_K + 2]
    v = x_ref[...] + gate_ref[...] * (ga * ya_ref[...] + gb * yb_ref[...])
    if len(rest) == 2:
        g_ref, o_ref = rest
        o_ref[...] = _rms(v) * g_ref[...]
    else:
        rest[0][...] = v


def _moe_combine(x, ya, yb, route, mods3, layer, n_prompt_rows, dec_seq, row0, n_rows, final_g):
    d = x.shape[1]
    tm = 512
    off = row0 // tm
    base = layer * MOD_ROWS * 6
    row = pl.BlockSpec((tm, d), lambda m: (m + off, 0))
    in_specs = [row, row, row, pl.BlockSpec((tm, LANE), lambda m: (m + off, 0)),
                pl.BlockSpec((None, 1, d),
                             lambda m: (base + _mod_row(m + off, tm, n_prompt_rows, dec_seq) * 6 + 5, 0, 0))]
    args = [x, ya, yb, route, mods3]
    if final_g is not None:
        in_specs.append(pl.BlockSpec((1, d), lambda m: (0, 0)))
        args.append(final_g.reshape(1, d))
    return pl.pallas_call(
        _combine_kernel, grid=(n_rows // tm,), in_specs=in_specs,
        out_specs=pl.BlockSpec((tm, d), lambda m: (m, 0)),
        out_shape=jax.ShapeDtypeStruct((n_rows, d), F32),
        compiler_params=_cparams(1), name="moe_combine")(*args)


def _take_rows(a, idx):
    return a.at[idx].get(mode="promise_in_bounds")


def _moe_ffn(x, h, route, w1, w3, w2, mods3, layer, n_prompt_rows, dec_seq, final_g=None):
    m_rows, d = x.shape
    tm = MOE_TM
    n_slots = m_rows * TOP_K
    n_rows = n_slots + N_EXPERTS * tm
    n_tiles = n_rows // tm
    top_i = route[:, 0:TOP_K].astype(jnp.int32)
    flat_e = top_i.reshape(n_slots)
    onehot = (flat_e[:, None] == jnp.arange(N_EXPERTS, dtype=jnp.int32)[None, :]).astype(jnp.int32)
    counts = jnp.sum(onehot, axis=0)
    rank = jnp.sum((jnp.cumsum(onehot, axis=0) - onehot) * onehot, axis=1)
    padded = ((counts + tm - 1) // tm) * tm
    group_end = jnp.cumsum(padded)
    group_start = group_end - padded
    pos = group_start[flat_e] + rank
    src_token = (jnp.arange(n_rows, dtype=jnp.int32) % m_rows).at[pos].set(
        jnp.arange(n_slots, dtype=jnp.int32) // TOP_K, unique_indices=True, mode="promise_in_bounds")
    n_used = (group_end[-1] // tm).astype(jnp.int32)
    tiles = jnp.arange(n_tiles, dtype=jnp.int32)
    tile_start = jnp.minimum(tiles, n_used - 1) * tm
    tile_expert = jnp.sum((tile_start[:, None] >= group_end[None, :]).astype(jnp.int32), axis=1)
    tile_expert = jnp.minimum(tile_expert, N_EXPERTS - 1).astype(jnp.int32)
    prev_expert = jnp.concatenate([jnp.full((1,), -1, jnp.int32), tile_expert[:-1]])
    first = ((tile_expert != prev_expert) & (tiles < n_used)).astype(jnp.int32)
    ids = jnp.arange(N_EXPERTS, dtype=jnp.int32)
    later = jnp.where((ids[None, :] > ids[:, None]) & (counts[None, :] > 0), ids[None, :], N_EXPERTS)
    next_expert = jnp.min(later, axis=1)
    is_last = next_expert == N_EXPERTS
    next_expert = jnp.where(is_last, tile_expert[0], next_expert)
    live = jnp.clip(group_start[tile_expert] + counts[tile_expert] - tiles * tm, 0, tm)
    live = jnp.where(tiles < n_used, live, 0).astype(jnp.int32)
    plan = (tile_expert, first, next_expert[tile_expert], is_last.astype(jnp.int32)[tile_expert], live)
    xs = _take_rows(h, src_token)
    g = _moe_up(xs, w1, w3, plan)
    ys = _moe_down(g, w2, plan)
    pos2 = pos.reshape(m_rows, TOP_K)
    ya = _take_rows(ys, pos2[:, 0])
    yb = _take_rows(ys, pos2[:, 1])
    common = (x, ya, yb, route, mods3, layer, n_prompt_rows, dec_seq)
    if final_g is None:
        return _moe_combine(*common, 0, m_rows, None)
    return (_moe_combine(*common, 0, n_prompt_rows, final_g),
            _moe_combine(*common, n_prompt_rows, m_rows - n_prompt_rows, final_g))


def _conv_cols(x_ref, w_ref, b_ref, o_ref, seg, act):
    length, ch = x_ref.shape
    t = lax.broadcasted_iota(jnp.int32, (length, LANE), 0) % seg

    def body(cb, carry):
        c0 = pl.multiple_of(cb * LANE, LANE)
        x = x_ref[:, pl.ds(c0, LANE)]
        w = w_ref[:, pl.ds(c0, LANE)]
        xm1 = jnp.where(t >= 1, pltpu.roll(x, 1, 0), 0.0)
        xp1 = jnp.where(t < seg - 1, pltpu.roll(x, length - 1, 0), 0.0)
        xp2 = jnp.where(t < seg - 2, pltpu.roll(x, length - 2, 0), 0.0)
        y = b_ref[:, pl.ds(c0, LANE)] + (w[0:1] * xm1 + w[1:2] * x + w[2:3] * xp1 + w[3:4] * xp2)
        if act:
            y = _silu(y)
        o_ref[:, pl.ds(c0, LANE)] = y
        return carry

    lax.fori_loop(0, ch // LANE, body, 0)


def _tri_masks(t_len):
    r = lax.broadcasted_iota(jnp.int32, (t_len, t_len), 0)
    c = lax.broadcasted_iota(jnp.int32, (t_len, t_len), 1)
    return r >= c, r <= c


def _mixer_call(body, name, proj, seq_cols, consts, init, st_shape, width, scratch, *, n_seq, length,
                row_off, n_rows_total, y_prev, emit_state, prev_state=None, **static):
    st_nd = len(st_shape)
    sps = static.get('sps')
    seq_dim = None if sps is None else sps
    sps = 1 if sps is None else sps
    assert n_seq % sps == 0 and row_off % sps == 0
    rows = sps * length
    row_off = row_off // sps
    n_steps = n_seq // sps
    n_fill = 0
    if y_prev is None:
        assert row_off == 0 and n_rows_total % rows == 0
        n_fill = n_rows_total // rows - n_steps
    own = lambda b: jnp.minimum(b, n_steps - 1)
    in_specs = [pl.BlockSpec((rows, w), lambda b, cb=cb: (own(b) + row_off, cb)) for w, cb in seq_cols]
    args = [proj] * len(seq_cols)
    for c in consts:
        in_specs.append(pl.BlockSpec(c.shape, lambda b, nd=c.ndim: (0,) * nd))
        args.append(c)
    if init is not None:
        in_specs.append(pl.BlockSpec((seq_dim,) + st_shape, lambda b: (own(b),) + (0,) * st_nd))
        args.append(init)
    n_prev = 0 if prev_state is None else prev_state.shape[1]
    if n_prev:
        in_specs.append(pl.BlockSpec((seq_dim, n_prev) + st_shape, lambda b: (own(b),) + (0,) * (st_nd + 1)))
        args.append(prev_state)
    aliases = {}
    n_alias = 0
    if y_prev is not None:
        aliases[len(args)] = 0
        in_specs.append(pl.BlockSpec(memory_space=pl.ANY))
        args.append(y_prev)
        n_alias += 1
    out_specs = [pl.BlockSpec((rows, width), lambda b: (b + row_off, 0))]
    out_shape = [jax.ShapeDtypeStruct((n_rows_total, width), BF16)]
    if emit_state:
        out_specs.append(pl.BlockSpec((seq_dim, n_prev + 1) + st_shape, lambda b: (own(b),) + (0,) * (st_nd + 1)))
        out_shape.append(jax.ShapeDtypeStruct((n_seq, n_prev + 1) + st_shape, F32))
        static = dict(static, st_layer=n_prev)
    layout = dict(n_seq_in=len(seq_cols), n_const=len(consts), has_init=init is not None, n_alias=n_alias,
                  emit_state=emit_state)
    prev_pos = len(seq_cols) + len(consts) + int(init is not None)

    def kern(*refs):
        step = pl.program_id(0)
        body_refs = refs[:prev_pos] + refs[prev_pos + 1:] if n_prev else refs

        @pl.when(step < n_steps)
        def _():
            if n_prev:
                st_out = _split_refs(body_refs, **layout)[4]
                if seq_dim is None:
                    st_out[0:n_prev] = refs[prev_pos][...]
                else:
                    st_out[:, 0:n_prev] = refs[prev_pos][...]
            body(*body_refs, **layout, **static)

        if n_fill:
            @pl.when(step >= n_steps)
            def _():
                y_ref = _split_refs(body_refs, **layout)[3]
                y_ref[...] = jnp.zeros_like(y_ref)

    return pl.pallas_call(
        kern, grid=(n_steps + n_fill,), in_specs=in_specs, out_specs=tuple(out_specs),
        out_shape=tuple(out_shape), scratch_shapes=scratch, input_output_aliases=aliases,
        compiler_params=_cparams(1), name=name)(*args)


def _split_refs(refs, n_seq_in, n_const, has_init, n_alias, emit_state):
    seq = refs[:n_seq_in]
    consts = refs[n_seq_in:n_seq_in + n_const]
    pos = n_seq_in + n_const
    init = refs[pos] if has_init else None
    pos += int(has_init) + n_alias
    y_ref = refs[pos]
    st_out = refs[pos + 1] if emit_state else None
    pos += 1 + int(emit_state)
    return seq, consts, init, y_ref, st_out, refs[pos:]


def _ssd_kernel(*refs, seg, st_layer=0, **layout):
    (z_ref, xs_ref, bc_ref, small_ref), consts, h0_ref, y_ref, hout_ref, scratch = _split_refs(refs, **layout)
    wx_ref, bx_ref, wbc_ref, bbc_ref, dtb_ref, nega_ref, dvec_ref, ng_ref = consts
    xc_ref, bcc_ref, yacc_ref, stt_ref = scratch
    length = z_ref.shape[0]
    t_len = SSD_T
    n_chunks = length // t_len
    n_pairs = SSD_H // 2
    _conv_cols(xs_ref, wx_ref, bx_ref, xc_ref, seg, True)
    _conv_cols(bc_ref, wbc_ref, bbc_ref, bcc_ref, seg, True)
    if h0_ref is None:
        stt_ref[...] = jnp.zeros_like(stt_ref)
    else:
        for d in range(2):
            for hp in range(n_pairs):
                pair = jnp.concatenate([h0_ref[d, 2 * hp], h0_ref[d, 2 * hp + 1]], axis=0)
                stt_ref[d, :, hp * LANE:(hp + 1) * LANE] = pair.T
    yacc_ref[...] = xc_ref[...] * dvec_ref[...]
    lower, upper = _tri_masks(t_len)
    tri = (lower.astype(BF16), upper.astype(BF16))
    masks = (lower, upper)
    lane_lo = lax.broadcasted_iota(jnp.int32, (t_len, LANE), 1) < SSD_P
    gn = SSD_N

    def chunk(c, d):
        r0 = pl.multiple_of(c * t_len, t_len)
        dtv = _softplus(small_ref[pl.ds(r0, t_len), :] + dtb_ref[d])
        cs = _exact_lhs_dot(tri[d], nega_ref[d] * dtv)
        cst = cs.T
        dtt = dtv.T
        end = t_len - 1 if d == 0 else 0
        bcv = bcc_ref[pl.ds(r0, t_len), :]
        groups = []
        for g in range(SSD_G):
            bg = bcv[:, g * gn:(g + 1) * gn]
            cg = bcv[:, (SSD_G + g) * gn:(SSD_G + g + 1) * gn]
            groups.append((_bdot_nt(cg, bg), bg.T, cg))
        for hp in range(n_pairs):
            gm, bgt, cg = groups[(2 * hp) // (SSD_H // SSD_G)]
            cols = slice(hp * LANE, (hp + 1) * LANE)
            x = xc_ref[pl.ds(r0, t_len), cols]
            st = stt_ref[d, :, cols]
            x_lo = jnp.where(lane_lo, x, 0.0).astype(BF16)
            x_hi = jnp.where(lane_lo, 0.0, x).astype(BF16)
            s_lo = jnp.where(lane_lo, st, 0.0).astype(BF16)
            s_hi = jnp.where(lane_lo, 0.0, st).astype(BF16)
            intra, carry_in, upd, edec = [], [], [], []
            for h in (2 * hp, 2 * hp + 1):
                colx = jnp.broadcast_to(cs[:, h:h + 1], (t_len, t_len))
                row = cst[h:h + 1, :]
                dtr = dtt[h:h + 1, :]
                cend = cst[h:h + 1, end:end + 1]
                intra.append((jnp.where(masks[d], jnp.exp(colx - row), 0.0) * (gm * dtr)).astype(BF16))
                carry_in.append((cg * jnp.exp(colx)).astype(BF16))
                upd.append((bgt * (jnp.exp(cend - row) * dtr)).astype(BF16))
                edec.append(jnp.exp(cend))
            xblk = jnp.concatenate([x_lo, x_hi], axis=0)
            y = jnp.dot(jnp.concatenate(intra + carry_in, axis=1),
                        jnp.concatenate([xblk, s_lo, s_hi], axis=0), preferred_element_type=F32)
            snew = jnp.dot(jnp.concatenate(upd, axis=1), xblk, preferred_element_type=F32)
            stt_ref[d, :, cols] = st * jnp.where(lane_lo[0:1], edec[0], edec[1]) + snew
            yacc_ref[pl.ds(r0, t_len), cols] += y

    def body(j, carry):
        chunk(j, 0)
        chunk(n_chunks - 1 - j, 1)
        return carry

    lax.fori_loop(0, n_chunks, body, 0)
    if hout_ref is not None:
        for d in range(2):
            for hp in range(n_pairs):
                pair = stt_ref[d, :, hp * LANE:(hp + 1) * LANE].T
                hout_ref[st_layer, d, 2 * hp] = pair[:SSD_P]
                hout_ref[st_layer, d, 2 * hp + 1] = pair[SSD_P:]

    def finish(c, carry):
        r0 = pl.multiple_of(c * t_len, t_len)
        y = yacc_ref[pl.ds(r0, t_len), :] * _silu(z_ref[pl.ds(r0, t_len), :])
        y_ref[pl.ds(r0, t_len), :] = (_rms(y) * ng_ref[...]).astype(y_ref.dtype)
        return carry

    lax.fori_loop(0, n_chunks, finish, 0)


def _ssd_call(proj, p, h0, **where):
    length = where['length']
    dtb = jnp.zeros((2, 1, LANE), F32).at[:, 0, :SSD_H].set(p['ssd_dt_bias'].astype(F32))
    nega = jnp.zeros((2, 1, LANE), F32).at[:, 0, :SSD_H].set(-jnp.exp(p['ssd_A_log'].astype(F32)))
    dvec = jnp.repeat(p['ssd_D'].astype(F32), SSD_P).reshape(1, SSD_D)
    cw = p['ssd_conv_w'].astype(F32)
    cb = p['ssd_conv_b'].astype(F32).reshape(1, SSD_CONV_CH)
    consts = [cw[:, :SSD_D], cb[:, :SSD_D], cw[:, SSD_D:], cb[:, SSD_D:], dtb, nega, dvec,
              p['ssd_norm_g'].astype(F32).reshape(1, SSD_D)]
    seq_cols = [(SSD_D, COL_Z // SSD_D), (SSD_D, COL_XS // SSD_D), (512, COL_BC // 512),
                (LANE, COL_SMALL // LANE)]
    scratch = [pltpu.VMEM((length, SSD_D), F32), pltpu.VMEM((length, 512), F32),
               pltpu.VMEM((length, SSD_D), F32), pltpu.VMEM((2, SSD_N, SSD_D), F32)]
    return _mixer_call(_ssd_kernel, "ssd_mixer", proj, seq_cols, consts, h0, (2, SSD_H, SSD_P, SSD_N),
                       SSD_D, scratch, **where)


def _log_sigmoid(x):
    return jnp.minimum(x, 0.0) - jnp.log1p(jnp.exp(-jnp.abs(x)))


def _gla_kernel(*refs, sps, st_layer=0, **layout):
    (q_ref, k_ref, v_ref, g_ref, small_ref), consts, s0_ref, y_ref, sout_ref, scratch = _split_refs(refs, **layout)
    gw_ref, gb_ref, ng_ref = consts
    oacc_ref, st_ref, bc_ref = scratch
    n_rows = q_ref.shape[0]
    length = n_rows // sps
    t_len = GLA_T
    n_chunks = length // t_len
    st_ref[...] = jnp.zeros_like(st_ref)
    if s0_ref is not None:
        for s in range(sps):
            for d in range(2):
                for h in range(GLA_H):
                    st_ref[s, d, h * GLA_DV:(h + 1) * GLA_DV, h * GLA_DK:(h + 1) * GLA_DK] = s0_ref[s, d, h]
    oacc_ref[...] = jnp.zeros_like(oacc_ref)
    lower, upper = _tri_masks(t_len)
    masks = tuple(jnp.concatenate([m] * GLA_H, axis=0) for m in (lower, upper))
    q_head = (lax.broadcasted_iota(jnp.int32, (GLA_H * t_len, GLA_KW), 0) // t_len
              == lax.broadcasted_iota(jnp.int32, (GLA_H * t_len, GLA_KW), 1) // GLA_DK)
    st_diag = (lax.broadcasted_iota(jnp.int32, (GLA_VW, GLA_KW), 0) // GLA_DV
               == lax.broadcasted_iota(jnp.int32, (GLA_VW, GLA_KW), 1) // GLA_DK)
    scale = GLA_DK ** -0.5

    blk = 256
    rr = lax.broadcasted_iota(jnp.int32, (blk, blk), 0)
    cc = lax.broadcasted_iota(jnp.int32, (blk, blk), 1)
    same_chunk = rr // t_len == cc // t_len
    tri_blk = ((same_chunk & (rr >= cc)).astype(BF16), (same_chunk & (rr <= cc)).astype(BF16))
    gate_w = []
    for d in range(2):
        w1, w2 = _split2(gw_ref[d])
        gate_w.append(jnp.concatenate([w1, w1, w2], axis=0))

    def decay_sums(b, carry):
        r0 = pl.multiple_of(b * blk, blk)
        s1, s2 = _split2(small_ref[pl.ds(r0, blk), :])
        lhs = jnp.concatenate([s1, s2, s1], axis=1)
        for d in range(2):
            logit = jnp.dot(lhs, gate_w[d], preferred_element_type=F32) + gb_ref[d]
            la = _log_sigmoid(logit) / GLA_TAU
            bc3 = jnp.dot(tri_blk[d], jnp.concatenate(_split3(la), axis=1), preferred_element_type=F32)
            bc_ref[d, pl.ds(r0, blk), :] = bc3[:, :GLA_KW] + bc3[:, GLA_KW:2 * GLA_KW] + bc3[:, 2 * GLA_KW:]
        return carry

    lax.fori_loop(0, n_rows // blk, decay_sums, 0)

    def chunk(c, d, s):
        r0 = pl.multiple_of(s * length + c * t_len, t_len)
        bc = bc_ref[d, pl.ds(r0, t_len), :]
        end = t_len - 1 if d == 0 else 0
        mid = bc[t_len // 2:t_len // 2 + 1, :]
        tot = bc[end:end + 1, :]
        q = q_ref[pl.ds(r0, t_len), :] * scale
        k = k_ref[pl.ds(r0, t_len), :]
        v = v_ref[pl.ds(r0, t_len), :].astype(BF16)
        qt = q * jnp.exp(bc - mid)
        kt = k * jnp.exp(mid - bc)
        qs = q * jnp.exp(bc)
        kd = k * jnp.exp(tot - bc)
        q_rows = jnp.where(q_head, jnp.concatenate([qt] * GLA_H, axis=0), 0.0)
        att = jnp.where(masks[d], _bdot_nt(q_rows, kt), 0.0)
        o_all = jnp.dot(att.astype(BF16), v, preferred_element_type=F32)
        o = jnp.concatenate([o_all[h * t_len:(h + 1) * t_len, h * GLA_DV:(h + 1) * GLA_DV]
                             for h in range(GLA_H)], axis=1)
        st = st_ref[s, d]
        o = o + _bdot_nt(qs, st)
        upd = lax.dot_general(v, kd.astype(BF16), (((0,), (0,)), ((), ())), preferred_element_type=F32)
        st_ref[s, d] = st * jnp.exp(tot) + jnp.where(st_diag, upd, 0.0)
        oacc_ref[pl.ds(r0, t_len), :] += o

    def body(j, carry):
        for s in range(sps):
            chunk(j, 0, s)
            chunk(n_chunks - 1 - j, 1, s)
        return carry

    lax.fori_loop(0, n_chunks, body, 0)
    if sout_ref is not None:
        for s in range(sps):
            for d in range(2):
                for h in range(GLA_H):
                    sout_ref[s, st_layer, d, h] = st_ref[s, d, h * GLA_DV:(h + 1) * GLA_DV,
                                                         h * GLA_DK:(h + 1) * GLA_DK]

    def finish(c, carry):
        r0 = pl.multiple_of(c * t_len, t_len)
        gate = _silu(g_ref[pl.ds(r0, t_len), :])
        for h in range(GLA_H):
            vs = slice(h * GLA_DV, (h + 1) * GLA_DV)
            o = _rms(oacc_ref[pl.ds(r0, t_len), vs]) * ng_ref[...]
            y_ref[pl.ds(r0, t_len), vs] = (o * gate[:, vs]).astype(y_ref.dtype)
        return carry

    lax.fori_loop(0, sps * n_chunks, finish, 0)


def _gla_call(proj, p, s0t, **where):
    length = where['length']
    sps = where['sps']
    gw = jnp.zeros((2, LANE, GLA_KW), F32).at[:, GLA_RANK:2 * GLA_RANK, :].set(p['gla_gate_w'].astype(F32))
    consts = [gw, p['gla_gate_b'].astype(F32).reshape(2, 1, GLA_KW),
              p['gla_norm_g'].astype(F32).reshape(1, GLA_DV)]
    seq_cols = [(GLA_KW, COL_Q // GLA_KW), (GLA_KW, COL_K // GLA_KW), (GLA_VW, COL_V // GLA_VW),
                (GLA_VW, COL_G // GLA_VW), (LANE, COL_SMALL // LANE)]
    st_shape = (2, GLA_H, GLA_DV, GLA_DK)
    scratch = [pltpu.VMEM((sps * length, GLA_VW), F32), pltpu.VMEM((sps, 2, GLA_VW, GLA_KW), F32),
               pltpu.VMEM((2, sps * length, GLA_KW), F32)]
    return _mixer_call(_gla_kernel, "gla_mixer", proj, seq_cols, consts, s0t, st_shape, GLA_VW, scratch, **where)


def _lru_kernel(*refs, seg, sps, st_layer=0, **layout):
    (xb_ref, gb_ref), consts, h0_ref, y_ref, hout_ref, scratch = _split_refs(refs, **layout)
    cw_ref, cb_ref, w_ref, bias_ref, sp_ref = consts
    xr_ref, a_ref, u_ref = scratch
    n_rows = xb_ref.shape[0]
    length = n_rows // sps
    blk = 256
    _conv_cols(xb_ref, cw_ref, cb_ref, xr_ref, seg, False)

    def gates(c, carry):
        r0 = pl.multiple_of(c * blk, blk)
        xr = xr_ref[pl.ds(r0, blk), :]
        xbf = xr.astype(BF16)
        for d in range(2):
            pre = jnp.dot(xbf, w_ref[d], preferred_element_type=F32) + bias_ref[d]
            gates = 0.5 * (jnp.tanh(0.5 * pre) + 1.0)
            r = gates[:, :LRU_W]
            i = gates[:, LRU_W:]
            log_a = (-LRU_C) * r * sp_ref[d]
            a = jnp.exp(log_a)
            a_ref[d, pl.ds(r0, blk), :] = a
            u_ref[d, pl.ds(r0, blk), :] = jnp.sqrt(-jnp.tanh(log_a) * (a * a + 1.0)) * (i * xr)
        return carry

    lax.fori_loop(0, n_rows // blk, gates, 0)

    def scan(t, carry):
        out = []
        for s in range(sps):
            for d in range(2):
                row = s * length + (t if d == 0 else length - 1 - t)
                h = a_ref[d, pl.ds(row, 1), :] * carry[2 * s + d] + u_ref[d, pl.ds(row, 1), :]
                u_ref[d, pl.ds(row, 1), :] = h
                out.append(h)
        return tuple(out)

    if h0_ref is None:
        start = tuple(jnp.zeros((1, LRU_W), F32) for _ in range(2 * sps))
    else:
        start = tuple(h0_ref[s, d] for s in range(sps) for d in range(2))
    final = lax.fori_loop(0, length, scan, start)
    if hout_ref is not None:
        for s in range(sps):
            for d in range(2):
                hout_ref[s, st_layer, d] = final[2 * s + d]

    def finish(c, carry):
        r0 = pl.multiple_of(c * blk, blk)
        y = (u_ref[0, pl.ds(r0, blk), :] + u_ref[1, pl.ds(r0, blk), :]) * jax.nn.gelu(gb_ref[pl.ds(r0, blk), :])
        y_ref[pl.ds(r0, blk), :] = y.astype(y_ref.dtype)
        return carry

    lax.fori_loop(0, n_rows // blk, finish, 0)


def _block_diag(w):
    nb, bw, _ = w.shape
    eye = jnp.eye(nb, dtype=w.dtype)
    return (eye[:, None, :, None] * w[:, :, None, :]).reshape(nb * bw, nb * bw)


def _lru_call(proj, p, h0, **where):
    length = where['length']
    w = jnp.stack([jnp.concatenate([_block_diag(p['lru_wa'][d].astype(F32)),
                                    _block_diag(p['lru_wx'][d].astype(F32))], axis=1) for d in range(2)])
    bias = jnp.stack([jnp.concatenate([p['lru_ba'][d], p['lru_bx'][d]]) for d in range(2)]).astype(F32)
    sp = jax.nn.softplus(-p['lru_lambda'].astype(F32)).reshape(2, 1, LRU_W)
    consts = [p['lru_conv_w'].astype(F32), p['lru_conv_b'].astype(F32).reshape(1, LRU_W), w.astype(BF16),
              bias.reshape(2, 1, 2 * LRU_W), sp]
    seq_cols = [(LRU_W, COL_XB // LRU_W), (LRU_W, COL_GB // LRU_W)]
    rows = where['sps'] * length
    scratch = [pltpu.VMEM((rows, LRU_W), F32), pltpu.VMEM((2, rows, LRU_W), F32),
               pltpu.VMEM((2, rows, LRU_W), F32)]
    return _mixer_call(_lru_kernel, "lru_mixer", proj, seq_cols, consts, h0, (2, 1, LRU_W), LRU_W, scratch,
                       **where)


def kernel(x_prompt, x_sample, state_ssd, state_gla, state_lru, c, c_ctx, mod_w, mod_b, norm1_g, norm2_g, in_w, ssd_conv_w, ssd_conv_b, ssd_A_log, ssd_dt_bias, ssd_D, ssd_norm_g, gla_gate_w, gla_gate_b, gla_norm_g, lru_conv_w, lru_conv_b, lru_wa, lru_ba, lru_wx, lru_bx, lru_lambda, out_w, ffn_w1, ffn_w3, ffn_w2, moe_router, moe_w1, moe_w3, moe_w2, final_norm_g):
    bp, lp, d = x_prompt.shape
    bs, ls, _ = x_sample.shape
    n_p = bp * lp
    n_s = bs * ls
    n_all = n_p + n_s
    depth = in_w.shape[0]
    assert n_p % ls == 0 and 1 + bs <= MOD_ROWS

    cvec = jnp.zeros((MOD_ROWS, d), F32).at[0].set(c_ctx.astype(F32)).at[1:1 + bs].set(c.astype(F32))
    mods = _adaln(cvec, mod_w, mod_b)
    mods3 = mods.reshape(depth * MOD_ROWS * 6, 1, d)
    x = (x_prompt.reshape(n_p, d).astype(F32), x_sample.reshape(n_s, d).astype(F32))

    in_w_t = jnp.swapaxes(in_w, 1, 2)
    st_ssd = st_gla = st_lru = None
    for i in range(depth):
        p = {'ssd_conv_w': ssd_conv_w[i], 'ssd_conv_b': ssd_conv_b[i], 'ssd_A_log': ssd_A_log[i],
             'ssd_dt_bias': ssd_dt_bias[i], 'ssd_D': ssd_D[i], 'ssd_norm_g': ssd_norm_g[i],
             'gla_gate_w': gla_gate_w[i], 'gla_gate_b': gla_gate_b[i], 'gla_norm_g': gla_norm_g[i],
             'lru_conv_w': lru_conv_w[i], 'lru_conv_b': lru_conv_b[i], 'lru_wa': lru_wa[i],
             'lru_ba': lru_ba[i], 'lru_wx': lru_wx[i], 'lru_bx': lru_bx[i], 'lru_lambda': lru_lambda[i]}
        h = _norm_mod(x, norm1_g[i], mods3, i, 0, n_p, ls)
        proj = _in_proj(h, in_w_t, i)
        ctx = dict(n_seq=bp, length=lp, row_off=0, n_rows_total=n_all, y_prev=None, emit_state=True)
        lat = dict(n_seq=bs, length=ls, row_off=n_p // ls, n_rows_total=n_all, emit_state=False)
        y_ssd, st_ssd = _ssd_call(proj, p, None, seg=lp, prev_state=st_ssd, **ctx)
        y_ssd, = _ssd_call(proj, p, state_ssd[:, i].astype(F32), seg=GRID_W, y_prev=y_ssd, **lat)
        y_gla, st_gla = _gla_call(proj, p, None, sps=MIX_SPS, prev_state=st_gla, **ctx)
        y_gla, = _gla_call(proj, p, jnp.swapaxes(state_gla[:, i].astype(F32), -1, -2), sps=MIX_SPS,
                           y_prev=y_gla, **lat)
        y_lru, st_lru = _lru_call(proj, p, None, seg=lp, sps=MIX_SPS, prev_state=st_lru, **ctx)
        y_lru, = _lru_call(proj, p, state_lru[:, i].astype(F32).reshape(bs, 2, 1, LRU_W), seg=GRID_W,
                           sps=MIX_SPS, y_prev=y_lru, **lat)
        j = i // 2
        if i % 2 == 1:
            x, h2, route = _out_proj(y_ssd, y_gla, y_lru, out_w, x, norm2_g[i], mods3, i, n_p, ls,
                                     router=moe_router[j])
            x = _moe_ffn(x, h2, route, moe_w1[j], moe_w3[j], moe_w2[j], mods3, i, n_p, ls,
                         final_g=final_norm_g if i == depth - 1 else None)
        else:
            x, h2 = _out_proj(y_ssd, y_gla, y_lru, out_w, x, norm2_g[i], mods3, i, n_p, ls)
            x = _ffn_down(_ffn_up(h2, ffn_w1[j], ffn_w3[j]), ffn_w2[j], x, mods3, i, n_p, ls)
    if isinstance(x, tuple):
        y_p, y_s = x
    else:
        y_p = _final_norm(x, final_norm_g, 0, n_p)
        y_s = _final_norm(x, final_norm_g, n_p, n_s)
    return (y_p.reshape(bp, lp, d), y_s.reshape(bs, ls, d), st_ssd, jnp.swapaxes(st_gla, -1, -2),
            st_lru.reshape(bp, depth, 2, LRU_W))
```

```python
import functools

import jax
import jax.numpy as jnp
from jax import lax
from jax.experimental import pallas as pl
from jax.experimental.pallas import tpu as pltpu

F32 = jnp.float32
BF16 = jnp.bfloat16

D_MODEL = 2048
GRID_W = 64
SSD_D = D_MODEL // 2
SSD_P = 64
SSD_H = SSD_D // SSD_P
SSD_G = 2
SSD_N = 128
GLA_H = 4
GLA_VW = D_MODEL // 4
GLA_DV = GLA_VW // GLA_H
GLA_DK = GLA_DV // 2
GLA_KW = GLA_H * GLA_DK
GLA_RANK = 16
GLA_TAU = 16.0
LRU_W = D_MODEL // 4
LRU_NB = 8
LRU_BW = LRU_W // LRU_NB
LRU_C = 8.0
SSD_CONV_CH = SSD_D + 2 * SSD_G * SSD_N
N_EXPERTS = 8
TOP_K = 2
EPS = 1e-6

LANE = 128
SSD_T = 128
GLA_T = 64
MIX_SPS = 2
MOD_ROWS = 8
VMEM_LIMIT = 56 * 1024 * 1024

COL_Z = 0
COL_XS = 1024
COL_BC = 2048
COL_Q = 2560
COL_K = 2816
COL_V = 3072
COL_G = 3584
COL_XB = 4096
COL_GB = 4608
COL_SMALL = 5120
IN_PAD = 5632


def _cparams(n_axes):
    return pltpu.CompilerParams(dimension_semantics=("arbitrary",) * n_axes,
                                vmem_limit_bytes=VMEM_LIMIT)


def _bdot(a, b):
    return jnp.dot(a.astype(BF16), b.astype(BF16), preferred_element_type=F32)


def _bdot_nt(a, b):
    return lax.dot_general(a.astype(BF16), b.astype(BF16), (((1,), (1,)), ((), ())),
                           preferred_element_type=F32)


def _bdot_tn(a, b):
    return lax.dot_general(a.astype(BF16), b.astype(BF16), (((0,), (0,)), ((), ())),
                           preferred_element_type=F32)


def _split2(a):
    a1 = a.astype(BF16)
    a2 = (a - a1.astype(F32)).astype(BF16)
    return a1, a2


def _split3(a):
    a1 = a.astype(BF16)
    r = a - a1.astype(F32)
    a2 = r.astype(BF16)
    a3 = (r - a2.astype(F32)).astype(BF16)
    return a1, a2, a3


def _exact_lhs_dot(m_bf16, a):
    a1, a2, a3 = _split3(a)
    f = lambda z: jnp.dot(m_bf16, z, preferred_element_type=F32)
    return f(a1) + f(a2) + f(a3)


def _dot3(a, b):
    a1, a2 = _split2(a)
    b1, b2 = _split2(b)
    f = lambda x, y: jnp.dot(x, y, preferred_element_type=F32)
    rows = a.shape[0]
    hi = f(jnp.concatenate([a1, a2], axis=0), b1)
    return hi[:rows] + (f(a1, b2) + hi[rows:])


def _softplus(x):
    return jnp.maximum(x, 0.0) + jnp.log1p(jnp.exp(-jnp.abs(x)))


def _silu(x):
    return x * jax.nn.sigmoid(x)


def _mod_row(m, tm, n_prompt_rows, dec_seq):
    r0 = m * tm
    return jnp.where(r0 < n_prompt_rows, 0, 1 + (r0 - n_prompt_rows) // dec_seq)


def _adaln_kernel(c_ref, w_ref, b_ref, o_ref):
    s = _silu(c_ref[...])
    o_ref[...] = _dot3(s, w_ref[...]) + b_ref[...]


def _adaln(cvec8, mod_w, mod_b):
    depth, d, n = mod_w.shape
    tn = 1024
    return pl.pallas_call(
        _adaln_kernel,
        grid=(depth, n // tn),
        in_specs=[pl.BlockSpec((MOD_ROWS, d), lambda i, j: (0, 0)),
                  pl.BlockSpec((None, d, tn), lambda i, j: (i, 0, j)),
                  pl.BlockSpec((None, 1, tn), lambda i, j: (i, 0, j))],
        out_specs=pl.BlockSpec((None, MOD_ROWS, tn), lambda i, j: (i, 0, j)),
        out_shape=jax.ShapeDtypeStruct((depth, MOD_ROWS, n), F32),
        compiler_params=_cparams(2),
        name="adaln",
    )(cvec8, mod_w, mod_b.reshape(depth, 1, n))


def _rms(x):
    return x * lax.rsqrt(jnp.mean(x * x, axis=-1, keepdims=True) + EPS)


def _stacked_specs(x, block, row_tile, col_block):
    if not isinstance(x, tuple):
        return [pl.BlockSpec(block, lambda *g: (row_tile(*g), col_block(*g)))], [x], 0
    split = x[0].shape[0] // block[0]
    specs = [pl.BlockSpec(block, lambda *g: (jnp.minimum(row_tile(*g), split - 1), col_block(*g))),
             pl.BlockSpec(block, lambda *g: (jnp.maximum(row_tile(*g) - split, 0), col_block(*g)))]
    return specs, list(x), split


def _stacked_tile(x_refs, m, split):
    if len(x_refs) == 1:
        return x_refs[0][...]
    return jnp.where(m < split, x_refs[0][...], x_refs[1][...])


def _norm_mod_kernel(*refs, split):
    g_ref, sh_ref, sc_ref, o_ref = refs[-4:]
    y = _rms(_stacked_tile(refs[:-4], pl.program_id(0), split)) * g_ref[...]
    o_ref[...] = (y * (1.0 + sc_ref[...]) + sh_ref[...]).astype(o_ref.dtype)


def _top2_route(h, router_pad):
    logits = _dot3(h, router_pad)
    lane = lax.broadcasted_iota(jnp.int32, logits.shape, 1)
    neg = jnp.float32(-jnp.inf)
    lg = jnp.where(lane < N_EXPERTS, logits, neg)
    m1 = jnp.max(lg, axis=-1, keepdims=True)
    i1 = jnp.min(jnp.where(lg == m1, lane, LANE), axis=-1, keepdims=True)
    lg2 = jnp.where(lane == i1, neg, lg)
    m2 = jnp.max(lg2, axis=-1, keepdims=True)
    i2 = jnp.min(jnp.where(lg2 == m2, lane, LANE), axis=-1, keepdims=True)
    e2 = jnp.exp(m2 - m1)
    den = 1.0 + e2
    g1 = 1.0 / den
    g2 = e2 / den
    return jnp.where(lane == 0, i1.astype(F32),
                     jnp.where(lane == 1, i2.astype(F32),
                               jnp.where(lane == 2, g1, jnp.where(lane == 3, g2, 0.0))))


def _norm_mod(x, g, mods3, layer, which, n_prompt_rows, dec_seq):
    d = g.shape[0]
    m_rows = sum(part.shape[0] for part in x) if isinstance(x, tuple) else x.shape[0]
    tm = 512
    base = layer * MOD_ROWS * 6

    def mod_spec(k):
        return pl.BlockSpec((None, 1, d),
                            lambda m: (base + _mod_row(m, tm, n_prompt_rows, dec_seq) * 6 + k, 0, 0))

    x_specs, x_args, split = _stacked_specs(x, (tm, d), lambda m: m, lambda m: 0)
    in_specs = x_specs + [pl.BlockSpec((1, d), lambda m: (0, 0)), mod_spec(3 * which), mod_spec(3 * which + 1)]
    args = x_args + [g.reshape(1, d), mods3, mods3]
    return pl.pallas_call(
        functools.partial(_norm_mod_kernel, split=split), grid=(m_rows // tm,), in_specs=in_specs,
        out_specs=pl.BlockSpec((tm, d), lambda m: (m, 0)), out_shape=jax.ShapeDtypeStruct((m_rows, d), BF16),
        compiler_params=_cparams(1), name="norm_mod")(*args)


def _final_norm_kernel(x_ref, g_ref, o_ref):
    o_ref[...] = _rms(x_ref[...]) * g_ref[...]


def _final_norm(x, g, row0, n_rows):
    d = x.shape[1]
    tm = 512
    off = row0 // tm
    return pl.pallas_call(
        _final_norm_kernel, grid=(n_rows // tm,),
        in_specs=[pl.BlockSpec((tm, d), lambda m: (m + off, 0)), pl.BlockSpec((1, d), lambda m: (0, 0))],
        out_specs=pl.BlockSpec((tm, d), lambda m: (m, 0)),
        out_shape=jax.ShapeDtypeStruct((n_rows, d), F32),
        compiler_params=_cparams(1), name="final_norm")(x, g.reshape(1, d))


IN_TN = 512
IN_DT_COL = SSD_D + SSD_CONV_CH
IN_GRAW_COL = IN_DT_COL + SSD_H + 2 * GLA_KW + 2 * GLA_VW


IN_UNITS = 2


def _in_proj_kernel(x_ref, *rest):
    unit_refs = [rest[2 * u:2 * u + 2] for u in range(IN_UNITS)]
    wdt_ref, wgr_ref, o_ref, wbf_ref = rest[2 * IN_UNITS:]
    j = pl.program_id(0)
    small_unit = COL_SMALL // IN_TN

    @pl.when(pl.program_id(1) == 0)
    def _():
        for u, (wm_ref, wn_ref) in enumerate(unit_refs):
            unit = j * IN_UNITS + u
            rows = slice(u * IN_TN, (u + 1) * IN_TN)

            def shifted(s, wm_ref=wm_ref, wn_ref=wn_ref, rows=rows):
                w = jnp.concatenate([wm_ref[...], wn_ref[...]], axis=0)
                wbf_ref[rows] = w[s:s + IN_TN].astype(BF16)

            @pl.when(unit < COL_Q // IN_TN)
            def _(wm_ref=wm_ref, rows=rows):
                wbf_ref[rows] = wm_ref[...].astype(BF16)

            @pl.when((unit >= COL_Q // IN_TN) & (unit < COL_XB // IN_TN))
            def _(shifted=shifted):
                shifted(SSD_H)

            @pl.when((unit >= COL_XB // IN_TN) & (unit < small_unit))
            def _(shifted=shifted):
                shifted(SSD_H + GLA_RANK)

            @pl.when(unit >= small_unit)
            def _(rows=rows):
                wbf_ref[rows] = jnp.zeros((IN_TN, wbf_ref.shape[1]), BF16)

            @pl.when(unit == small_unit)
            def _(u=u):
                wbf_ref[u * IN_TN:u * IN_TN + SSD_H] = wdt_ref[...].astype(BF16)
                wbf_ref[u * IN_TN + SSD_H:u * IN_TN + SSD_H + GLA_RANK] = wgr_ref[...].astype(BF16)

    nt = (((1,), (1,)), ((), ()))
    has_small = (j + 1) * IN_UNITS > small_unit

    @pl.when(jnp.logical_not(has_small))
    def _():
        o_ref[...] = lax.dot_general(x_ref[...], wbf_ref[...], nt, preferred_element_type=F32)

    @pl.when(has_small)
    def _():
        full = (small_unit % IN_UNITS) * IN_TN
        o_ref[:, :full + LANE] = lax.dot_general(x_ref[...], wbf_ref[0:full + LANE], nt,
                                                 preferred_element_type=F32)
        o_ref[:, full + LANE:] = jnp.zeros((o_ref.shape[0], o_ref.shape[1] - full - LANE), F32)


def _in_proj(h, in_w_t, layer):
    m_rows, k = h.shape
    tm, tn = 1024, IN_TN * IN_UNITS
    n_main = COL_SMALL // IN_TN
    shift_max = SSD_H + GLA_RANK
    last_next_block = in_w_t.shape[1] // shift_max - 1
    unit_specs = []
    for u in range(IN_UNITS):
        unit_specs += [
            pl.BlockSpec((None, IN_TN, k),
                         lambda j, m, u=u: (layer, jnp.minimum(j * IN_UNITS + u, n_main - 1), 0)),
            pl.BlockSpec((None, shift_max, k),
                         lambda j, m, u=u: (layer, jnp.minimum((j * IN_UNITS + u + 1) * (IN_TN // shift_max),
                                                               last_next_block), 0))]
    return pl.pallas_call(
        _in_proj_kernel, grid=(pl.cdiv(IN_PAD, tn), m_rows // tm),
        in_specs=[pl.BlockSpec((tm, k), lambda j, m: (m, 0))] + unit_specs + [
                  pl.BlockSpec((None, SSD_H, k), lambda j, m: (layer, IN_DT_COL // SSD_H, 0)),
                  pl.BlockSpec((None, GLA_RANK, k), lambda j, m: (layer, IN_GRAW_COL // GLA_RANK, 0))],
        out_specs=pl.BlockSpec((tm, tn), lambda j, m: (m, j)),
        out_shape=jax.ShapeDtypeStruct((m_rows, IN_PAD), F32),
        scratch_shapes=[pltpu.VMEM((tn, k), BF16)],
        compiler_params=_cparams(2), name="in_proj")(h, *([in_w_t] * (2 * IN_UNITS + 2)))


def _out_proj_kernel(y1_ref, y2_ref, y3_ref, w_hbm, *rest, n_res, split, layer, route):
    res_refs = rest[:n_res]
    gate_ref, g_ref, sh_ref, sc_ref = rest[n_res:n_res + 4]
    pos = n_res + 4
    r_ref = rest[pos] if route else None
    pos += int(route)
    x_out, h_out = rest[pos:pos + 2]
    pos += 2
    route_out = rest[pos] if route else None
    pos += int(route)
    wst_ref, wbf_ref, sem = rest[pos:]
    m = pl.program_id(0)

    @pl.when(m == 0)
    def _():
        copy = pltpu.make_async_copy(w_hbm.at[layer], wst_ref, sem.at[0])
        copy.start()
        copy.wait()
        wbf_ref[...] = wst_ref[...].astype(BF16)

    k1 = y1_ref.shape[1]
    k2 = k1 + y2_ref.shape[1]
    acc = jnp.dot(y1_ref[...], wbf_ref[0:k1, :], preferred_element_type=F32)
    acc += jnp.dot(y2_ref[...], wbf_ref[k1:k2, :], preferred_element_type=F32)
    acc += jnp.dot(y3_ref[...], wbf_ref[k2:, :], preferred_element_type=F32)
    x_new = _stacked_tile(res_refs, m, split) + gate_ref[...] * acc
    x_out[...] = x_new
    h = (_rms(x_new) * g_ref[...]) * (1.0 + sc_ref[...]) + sh_ref[...]
    h_out[...] = h.astype(h_out.dtype)
    if route:
        route_out[...] = _top2_route(h, r_ref[...])


def _out_proj(y1, y2, y3, w, res, norm_g, mods3, layer, n_prompt_rows, dec_seq, router=None):
    m_rows = y1.shape[0]
    k, d = w.shape[1:]
    tm = 256
    base = layer * MOD_ROWS * 6
    route = router is not None

    def mod_spec(which):
        return pl.BlockSpec((None, 1, d),
                            lambda m: (base + _mod_row(m, tm, n_prompt_rows, dec_seq) * 6 + which, 0, 0))

    row = lambda width: pl.BlockSpec((tm, width), lambda m: (m, 0))
    res_specs, res_args, split = _stacked_specs(res, (tm, d), lambda m: m, lambda m: 0)
    in_specs = [row(y1.shape[1]), row(y2.shape[1]), row(y3.shape[1]), pl.BlockSpec(memory_space=pl.ANY)]
    in_specs += res_specs + [mod_spec(2), pl.BlockSpec((1, d), lambda m: (0, 0)), mod_spec(3), mod_spec(4)]
    args = [y1, y2, y3, w] + res_args + [mods3, norm_g.reshape(1, d), mods3, mods3]
    out_specs = [row(d), row(d)]
    out_shape = [jax.ShapeDtypeStruct((m_rows, d), F32), jax.ShapeDtypeStruct((m_rows, d), F32 if route else BF16)]
    if route:
        in_specs.append(pl.BlockSpec((d, LANE), lambda m: (0, 0)))
        args.append(jnp.zeros((d, LANE), F32).at[:, :N_EXPERTS].set(router.astype(F32)))
        out_specs.append(row(LANE))
        out_shape.append(jax.ShapeDtypeStruct((m_rows, LANE), F32))
    kern = functools.partial(_out_proj_kernel, n_res=len(res_args), split=split, layer=layer, route=route)
    return pl.pallas_call(
        kern, grid=(m_rows // tm,), in_specs=in_specs, out_specs=tuple(out_specs), out_shape=tuple(out_shape),
        scratch_shapes=[pltpu.VMEM((k, d), F32), pltpu.VMEM((k, d), BF16), pltpu.SemaphoreType.DMA((1,))],
        compiler_params=_cparams(1), name="out_proj")(*args)


def _ffn_up_kernel(x_ref, w1_ref, w3_ref, o_ref, w1bf_ref, w3bf_ref):
    @pl.when(pl.program_id(1) == 0)
    def _():
        w1bf_ref[...] = w1_ref[...].astype(BF16)
        w3bf_ref[...] = w3_ref[...].astype(BF16)

    a = jnp.dot(x_ref[...], w1bf_ref[...], preferred_element_type=F32)
    b = jnp.dot(x_ref[...], w3bf_ref[...], preferred_element_type=F32)
    o_ref[...] = (_silu(a) * b).astype(o_ref.dtype)


def _ffn_up(h, w1, w3):
    m_rows, d = h.shape
    f = w1.shape[1]
    tm, tf = 1024, 512
    return pl.pallas_call(
        _ffn_up_kernel, grid=(pl.cdiv(f, tf), m_rows // tm),
        in_specs=[pl.BlockSpec((tm, d), lambda j, m: (m, 0)),
                  pl.BlockSpec((d, tf), lambda j, m: (0, j)),
                  pl.BlockSpec((d, tf), lambda j, m: (0, j))],
        out_specs=pl.BlockSpec((tm, tf), lambda j, m: (m, j)),
        out_shape=jax.ShapeDtypeStruct((m_rows, f), BF16),
        scratch_shapes=[pltpu.VMEM((d, tf), BF16), pltpu.VMEM((d, tf), BF16)],
        compiler_params=_cparams(2), name="ffn_up")(h, w1, w3)


def _ffn_down_kernel(g_ref, w_ref, res_ref, gate_ref, o_ref, wbf_ref):
    @pl.when(pl.program_id(1) == 0)
    def _():
        wbf_ref[...] = w_ref[...].astype(BF16)

    acc = jnp.dot(g_ref[...], wbf_ref[...], preferred_element_type=F32)
    o_ref[...] = res_ref[...] + gate_ref[...] * acc


def _ffn_down(g, w2, res, mods3, layer, n_prompt_rows, dec_seq):
    m_rows, d = res.shape
    f = w2.shape[0]
    tm, tn = 512, 512
    base = layer * MOD_ROWS * 6
    return pl.pallas_call(
        _ffn_down_kernel, grid=(d // tn, m_rows // tm),
        in_specs=[pl.BlockSpec((tm, f), lambda j, m: (m, 0)),
                  pl.BlockSpec((f, tn), lambda j, m: (0, j)),
                  pl.BlockSpec((tm, tn), lambda j, m: (m, j)),
                  pl.BlockSpec((None, 1, tn),
                               lambda j, m: (base + _mod_row(m, tm, n_prompt_rows, dec_seq) * 6 + 5, 0, j))],
        out_specs=pl.BlockSpec((tm, tn), lambda j, m: (m, j)),
        out_shape=jax.ShapeDtypeStruct((m_rows, d), F32),
        scratch_shapes=[pltpu.VMEM((f, tn), BF16)],
        compiler_params=_cparams(2), name="ffn_down")(g, w2, res, mods3)


MOE_TM = 256
MOE_ROW_QUARTERS = 4


def _grouped_weights(plan_refs, w_hbm, stages, casts, sems):
    te_ref, first_ref, next_e_ref, last_ref = plan_refs
    j = pl.program_id(0)
    m = pl.program_id(1)
    n_col_tiles = pl.num_programs(0)

    def fetch(e, jj, k):
        width = stages[k].shape[1]
        src = w_hbm[k].at[e, :, pl.ds(pl.multiple_of(jj * width, width), width)]
        return pltpu.make_async_copy(src, stages[k], sems.at[k])

    @pl.when((j == 0) & (m == 0))
    def _():
        for k in range(len(stages)):
            fetch(te_ref[0], 0, k).start()

    @pl.when(first_ref[m] == 1)
    def _():
        for k in range(len(stages)):
            fetch(te_ref[m], j, k).wait()
            casts[k][...] = stages[k][...].astype(BF16)
        next_j = j + last_ref[m]

        @pl.when(next_j < n_col_tiles)
        def _():
            for k in range(len(stages)):
                fetch(next_e_ref[m], next_j, k).start()


def _by_live_rows(live, o_ref, compute):
    rows = o_ref.shape[0]
    step = rows // MOE_ROW_QUARTERS
    for q in range(1, MOE_ROW_QUARTERS + 1):
        top = q * step

        @pl.when((live > top - step) & (live <= top))
        def _(top=top):
            o_ref[0:top] = compute(slice(0, top))
            if top < rows:
                o_ref[top:] = jnp.zeros((rows - top, o_ref.shape[1]), o_ref.dtype)

    @pl.when(live == 0)
    def _():
        o_ref[...] = jnp.zeros_like(o_ref)


def _moe_up_kernel(te_ref, first_ref, next_e_ref, last_ref, live_ref, x_ref, w1_hbm, w3_hbm, o_ref,
                   w1st_ref, w3st_ref, w1bf_ref, w3bf_ref, sems):
    _grouped_weights((te_ref, first_ref, next_e_ref, last_ref), (w1_hbm, w3_hbm), (w1st_ref, w3st_ref),
                     (w1bf_ref, w3bf_ref), sems)

    def compute(rows):
        x = x_ref[rows].astype(BF16)
        a = jnp.dot(x, w1bf_ref[...], preferred_element_type=F32)
        b = jnp.dot(x, w3bf_ref[...], preferred_element_type=F32)
        return (_silu(a) * b).astype(o_ref.dtype)

    _by_live_rows(live_ref[pl.program_id(1)], o_ref, compute)


def _moe_up(xs, w1, w3, plan):
    n_rows, d = xs.shape
    f = w1.shape[2]
    tm, tf = MOE_TM, 1792
    n_plan = len(plan)
    grid_spec = pltpu.PrefetchScalarGridSpec(
        num_scalar_prefetch=n_plan, grid=(f // tf, n_rows // tm),
        in_specs=[pl.BlockSpec((tm, d), lambda j, m, *_: (m, 0)),
                  pl.BlockSpec(memory_space=pl.ANY), pl.BlockSpec(memory_space=pl.ANY)],
        out_specs=pl.BlockSpec((tm, tf), lambda j, m, *_: (m, j)),
        scratch_shapes=[pltpu.VMEM((d, tf), F32), pltpu.VMEM((d, tf), F32),
                        pltpu.VMEM((d, tf), BF16), pltpu.VMEM((d, tf), BF16),
                        pltpu.SemaphoreType.DMA((2,))])
    return pl.pallas_call(
        _moe_up_kernel, grid_spec=grid_spec,
        out_shape=jax.ShapeDtypeStruct((n_rows, f), BF16),
        compiler_params=_cparams(2), name="moe_up")(*plan, xs, w1, w3)


def _moe_down_kernel(te_ref, first_ref, next_e_ref, last_ref, live_ref, g_ref, w_hbm, o_ref,
                     wst_ref, wbf_ref, sems):
    _grouped_weights((te_ref, first_ref, next_e_ref, last_ref), (w_hbm,), (wst_ref,), (wbf_ref,), sems)

    def compute(rows):
        return jnp.dot(g_ref[rows], wbf_ref[...], preferred_element_type=F32)

    _by_live_rows(live_ref[pl.program_id(1)], o_ref, compute)


def _moe_down(g, w2, plan):
    n_rows, f = g.shape
    d = w2.shape[2]
    tm, tn = MOE_TM, 1024
    grid_spec = pltpu.PrefetchScalarGridSpec(
        num_scalar_prefetch=len(plan), grid=(d // tn, n_rows // tm),
        in_specs=[pl.BlockSpec((tm, f), lambda j, m, *_: (m, 0)),
                  pl.BlockSpec(memory_space=pl.ANY)],
        out_specs=pl.BlockSpec((tm, tn), lambda j, m, *_: (m, j)),
        scratch_shapes=[pltpu.VMEM((f, tn), F32), pltpu.VMEM((f, tn), BF16),
                        pltpu.SemaphoreType.DMA((1,))])
    return pl.pallas_call(
        _moe_down_kernel, grid_spec=grid_spec,
        out_shape=jax.ShapeDtypeStruct((n_rows, d), F32),
        compiler_params=_cparams(2), name="moe_down")(*plan, g, w2)


def _combine_kernel(x_ref, ya_ref, yb_ref, route_ref, gate_ref, *rest):
    ga = route_ref[:, TOP_K:TOP_K + 1]
    gb = route_ref[:, TOP_K + 1:TOP_K + 2]
    v = x_ref[...] + gate_ref[...] * (ga * ya_ref[...] + gb * yb_ref[...])
    if len(rest) == 2:
        g_ref, o_ref = rest
        o_ref[...] = _rms(v) * g_ref[...]
    else:
        rest[0][...] = v


def _moe_combine(x, ya, yb, route, mods3, layer, n_prompt_rows, dec_seq, row0, n_rows, final_g):
    d = x.shape[1]
    tm = 512
    off = row0 // tm
    base = layer * MOD_ROWS * 6
    row = pl.BlockSpec((tm, d), lambda m: (m + off, 0))
    in_specs = [row, row, row, pl.BlockSpec((tm, LANE), lambda m: (m + off, 0)),
                pl.BlockSpec((None, 1, d),
                             lambda m: (base + _mod_row(m + off, tm, n_prompt_rows, dec_seq) * 6 + 5, 0, 0))]
    args = [x, ya, yb, route, mods3]
    if final_g is not None:
        in_specs.append(pl.BlockSpec((1, d), lambda m: (0, 0)))
        args.append(final_g.reshape(1, d))
    return pl.pallas_call(
        _combine_kernel, grid=(n_rows // tm,), in_specs=in_specs,
        out_specs=pl.BlockSpec((tm, d), lambda m: (m, 0)),
        out_shape=jax.ShapeDtypeStruct((n_rows, d), F32),
        compiler_params=_cparams(1), name="moe_combine")(*args)


def _take_rows(a, idx):
    return a.at[idx].get(mode="promise_in_bounds")


def _moe_ffn(x, h, route, w1, w3, w2, mods3, layer, n_prompt_rows, dec_seq, final_g=None):
    m_rows, d = x.shape
    tm = MOE_TM
    n_slots = m_rows * TOP_K
    n_rows = n_slots + N_EXPERTS * tm
    n_tiles = n_rows // tm
    top_i = route[:, 0:TOP_K].astype(jnp.int32)
    flat_e = top_i.reshape(n_slots)
    onehot = (flat_e[:, None] == jnp.arange(N_EXPERTS, dtype=jnp.int32)[None, :]).astype(jnp.int32)
    counts = jnp.sum(onehot, axis=0)
    rank = jnp.sum((jnp.cumsum(onehot, axis=0) - onehot) * onehot, axis=1)
    padded = ((counts + tm - 1) // tm) * tm
    group_end = jnp.cumsum(padded)
    group_start = group_end - padded
    pos = group_start[flat_e] + rank
    src_token = (jnp.arange(n_rows, dtype=jnp.int32) % m_rows).at[pos].set(
        jnp.arange(n_slots, dtype=jnp.int32) // TOP_K, unique_indices=True, mode="promise_in_bounds")
    n_used = (group_end[-1] // tm).astype(jnp.int32)
    tiles = jnp.arange(n_tiles, dtype=jnp.int32)
    tile_start = jnp.minimum(tiles, n_used - 1) * tm
    tile_expert = jnp.sum((tile_start[:, None] >= group_end[None, :]).astype(jnp.int32), axis=1)
    tile_expert = jnp.minimum(tile_expert, N_EXPERTS - 1).astype(jnp.int32)
    prev_expert = jnp.concatenate([jnp.full((1,), -1, jnp.int32), tile_expert[:-1]])
    first = ((tile_expert != prev_expert) & (tiles < n_used)).astype(jnp.int32)
    ids = jnp.arange(N_EXPERTS, dtype=jnp.int32)
    later = jnp.where((ids[None, :] > ids[:, None]) & (counts[None, :] > 0), ids[None, :], N_EXPERTS)
    next_expert = jnp.min(later, axis=1)
    is_last = next_expert == N_EXPERTS
    next_expert = jnp.where(is_last, tile_expert[0], next_expert)
    live = jnp.clip(group_start[tile_expert] + counts[tile_expert] - tiles * tm, 0, tm)
    live = jnp.where(tiles < n_used, live, 0).astype(jnp.int32)
    plan = (tile_expert, first, next_expert[tile_expert], is_last.astype(jnp.int32)[tile_expert], live)
    xs = _take_rows(h, src_token)
    g = _moe_up(xs, w1, w3, plan)
    ys = _moe_down(g, w2, plan)
    pos2 = pos.reshape(m_rows, TOP_K)
    ya = _take_rows(ys, pos2[:, 0])
    yb = _take_rows(ys, pos2[:, 1])
    common = (x, ya, yb, route, mods3, layer, n_prompt_rows, dec_seq)
    if final_g is None:
        return _moe_combine(*common, 0, m_rows, None)
    return (_moe_combine(*common, 0, n_prompt_rows, final_g),
            _moe_combine(*common, n_prompt_rows, m_rows - n_prompt_rows, final_g))


def _conv_cols(x_ref, w_ref, b_ref, o_ref, seg, act):
    length, ch = x_ref.shape
    t = lax.broadcasted_iota(jnp.int32, (length, LANE), 0) % seg

    def body(cb, carry):
        c0 = pl.multiple_of(cb * LANE, LANE)
        x = x_ref[:, pl.ds(c0, LANE)]
        w = w_ref[:, pl.ds(c0, LANE)]
        xm1 = jnp.where(t >= 1, pltpu.roll(x, 1, 0), 0.0)
        xp1 = jnp.where(t < seg - 1, pltpu.roll(x, length - 1, 0), 0.0)
        xp2 = jnp.where(t < seg - 2, pltpu.roll(x, length - 2, 0), 0.0)
        y = b_ref[:, pl.ds(c0, LANE)] + (w[0:1] * xm1 + w[1:2] * x + w[2:3] * xp1 + w[3:4] * xp2)
        if act:
            y = _silu(y)
        o_ref[:, pl.ds(c0, LANE)] = y
        return carry

    lax.fori_loop(0, ch // LANE, body, 0)


def _tri_masks(t_len):
    r = lax.broadcasted_iota(jnp.int32, (t_len, t_len), 0)
    c = lax.broadcasted_iota(jnp.int32, (t_len, t_len), 1)
    return r >= c, r <= c


def _mixer_call(body, name, proj, seq_cols, consts, init, st_shape, width, scratch, *, n_seq, length,
                row_off, n_rows_total, y_prev, emit_state, prev_state=None, **static):
    st_nd = len(st_shape)
    sps = static.get('sps')
    seq_dim = None if sps is None else sps
    sps = 1 if sps is None else sps
    assert n_seq % sps == 0 and row_off % sps == 0
    rows = sps * length
    row_off = row_off // sps
    n_steps = n_seq // sps
    n_fill = 0
    if y_prev is None:
        assert row_off == 0 and n_rows_total % rows == 0
        n_fill = n_rows_total // rows - n_steps
    own = lambda b: jnp.minimum(b, n_steps - 1)
    in_specs = [pl.BlockSpec((rows, w), lambda b, cb=cb: (own(b) + row_off, cb)) for w, cb in seq_cols]
    args = [proj] * len(seq_cols)
    for c in consts:
        in_specs.append(pl.BlockSpec(c.shape, lambda b, nd=c.ndim: (0,) * nd))
        args.append(c)
    if init is not None:
        in_specs.append(pl.BlockSpec((seq_dim,) + st_shape, lambda b: (own(b),) + (0,) * st_nd))
        args.append(init)
    n_prev = 0 if prev_state is None else prev_state.shape[1]
    if n_prev:
        in_specs.append(pl.BlockSpec((seq_dim, n_prev) + st_shape, lambda b: (own(b),) + (0,) * (st_nd + 1)))
        args.append(prev_state)
    aliases = {}
    n_alias = 0
    if y_prev is not None:
        aliases[len(args)] = 0
        in_specs.append(pl.BlockSpec(memory_space=pl.ANY))
        args.append(y_prev)
        n_alias += 1
    out_specs = [pl.BlockSpec((rows, width), lambda b: (b + row_off, 0))]
    out_shape = [jax.ShapeDtypeStruct((n_rows_total, width), BF16)]
    if emit_state:
        out_specs.append(pl.BlockSpec((seq_dim, n_prev + 1) + st_shape, lambda b: (own(b),) + (0,) * (st_nd + 1)))
        out_shape.append(jax.ShapeDtypeStruct((n_seq, n_prev + 1) + st_shape, F32))
        static = dict(static, st_layer=n_prev)
    layout = dict(n_seq_in=len(seq_cols), n_const=len(consts), has_init=init is not None, n_alias=n_alias,
                  emit_state=emit_state)
    prev_pos = len(seq_cols) + len(consts) + int(init is not None)

    def kern(*refs):
        step = pl.program_id(0)
        body_refs = refs[:prev_pos] + refs[prev_pos + 1:] if n_prev else refs

        @pl.when(step < n_steps)
        def _():
            if n_prev:
                st_out = _split_refs(body_refs, **layout)[4]
                if seq_dim is None:
                    st_out[0:n_prev] = refs[prev_pos][...]
                else:
                    st_out[:, 0:n_prev] = refs[prev_pos][...]
            body(*body_refs, **layout, **static)

        if n_fill:
            @pl.when(step >= n_steps)
            def _():
                y_ref = _split_refs(body_refs, **layout)[3]
                y_ref[...] = jnp.zeros_like(y_ref)

    return pl.pallas_call(
        kern, grid=(n_steps + n_fill,), in_specs=in_specs, out_specs=tuple(out_specs),
        out_shape=tuple(out_shape), scratch_shapes=scratch, input_output_aliases=aliases,
        compiler_params=_cparams(1), name=name)(*args)


def _split_refs(refs, n_seq_in, n_const, has_init, n_alias, emit_state):
    seq = refs[:n_seq_in]
    consts = refs[n_seq_in:n_seq_in + n_const]
    pos = n_seq_in + n_const
    init = refs[pos] if has_init else None
    pos += int(has_init) + n_alias
    y_ref = refs[pos]
    st_out = refs[pos + 1] if emit_state else None
    pos += 1 + int(emit_state)
    return seq, consts, init, y_ref, st_out, refs[pos:]


def _ssd_kernel(*refs, seg, st_layer=0, **layout):
    (z_ref, xs_ref, bc_ref, small_ref), consts, h0_ref, y_ref, hout_ref, scratch = _split_refs(refs, **layout)
    wx_ref, bx_ref, wbc_ref, bbc_ref, dtb_ref, nega_ref, dvec_ref, ng_ref = consts
    xc_ref, bcc_ref, yacc_ref, stt_ref = scratch
    length = z_ref.shape[0]
    t_len = SSD_T
    n_chunks = length // t_len
    n_pairs = SSD_H // 2
    _conv_cols(xs_ref, wx_ref, bx_ref, xc_ref, seg, True)
    _conv_cols(bc_ref, wbc_ref, bbc_ref, bcc_ref, seg, True)
    if h0_ref is None:
        stt_ref[...] = jnp.zeros_like(stt_ref)
    else:
        for d in range(2):
            for hp in range(n_pairs):
                pair = jnp.concatenate([h0_ref[d, 2 * hp], h0_ref[d, 2 * hp + 1]], axis=0)
                stt_ref[d, :, hp * LANE:(hp + 1) * LANE] = pair.T
    yacc_ref[...] = xc_ref[...] * dvec_ref[...]
    lower, upper = _tri_masks(t_len)
    tri = (lower.astype(BF16), upper.astype(BF16))
    masks = (lower, upper)
    lane_lo = lax.broadcasted_iota(jnp.int32, (t_len, LANE), 1) < SSD_P
    gn = SSD_N

    def chunk(c, d):
        r0 = pl.multiple_of(c * t_len, t_len)
        dtv = _softplus(small_ref[pl.ds(r0, t_len), :] + dtb_ref[d])
        cs = _exact_lhs_dot(tri[d], nega_ref[d] * dtv)
        cst = cs.T
        dtt = dtv.T
        end = t_len - 1 if d == 0 else 0
        bcv = bcc_ref[pl.ds(r0, t_len), :]
        groups = []
        for g in range(SSD_G):
            bg = bcv[:, g * gn:(g + 1) * gn]
            cg = bcv[:, (SSD_G + g) * gn:(SSD_G + g + 1) * gn]
            groups.append((_bdot_nt(cg, bg), bg.T, cg))
        for hp in range(n_pairs):
            gm, bgt, cg = groups[(2 * hp) // (SSD_H // SSD_G)]
            cols = slice(hp * LANE, (hp + 1) * LANE)
            x = xc_ref[pl.ds(r0, t_len), cols]
            st = stt_ref[d, :, cols]
            x_lo = jnp.where(lane_lo, x, 0.0).astype(BF16)
            x_hi = jnp.where(lane_lo, 0.0, x).astype(BF16)
            s_lo = jnp.where(lane_lo, st, 0.0).astype(BF16)
            s_hi = jnp.where(lane_lo, 0.0, st).astype(BF16)
            intra, carry_in, upd, edec = [], [], [], []
            for h in (2 * hp, 2 * hp + 1):
                colx = jnp.broadcast_to(cs[:, h:h + 1], (t_len, t_len))
                row = cst[h:h + 1, :]
                dtr = dtt[h:h + 1, :]
                cend = cst[h:h + 1, end:end + 1]
                intra.append((jnp.where(masks[d], jnp.exp(colx - row), 0.0) * (gm * dtr)).astype(BF16))
                carry_in.append((cg * jnp.exp(colx)).astype(BF16))
                upd.append((bgt * (jnp.exp(cend - row) * dtr)).astype(BF16))
                edec.append(jnp.exp(cend))
            xblk = jnp.concatenate([x_lo, x_hi], axis=0)
            y = jnp.dot(jnp.concatenate(intra + carry_in, axis=1),
                        jnp.concatenate([xblk, s_lo, s_hi], axis=0), preferred_element_type=F32)
            snew = jnp.dot(jnp.concatenate(upd, axis=1), xblk, preferred_element_type=F32)
            stt_ref[d, :, cols] = st * jnp.where(lane_lo[0:1], edec[0], edec[1]) + snew
            yacc_ref[pl.ds(r0, t_len), cols] += y

    def body(j, carry):
        chunk(j, 0)
        chunk(n_chunks - 1 - j, 1)
        return carry

    lax.fori_loop(0, n_chunks, body, 0)
    if hout_ref is not None:
        for d in range(2):
            for hp in range(n_pairs):
                pair = stt_ref[d, :, hp * LANE:(hp + 1) * LANE].T
                hout_ref[st_layer, d, 2 * hp] = pair[:SSD_P]
                hout_ref[st_layer, d, 2 * hp + 1] = pair[SSD_P:]

    def finish(c, carry):
        r0 = pl.multiple_of(c * t_len, t_len)
        y = yacc_ref[pl.ds(r0, t_len), :] * _silu(z_ref[pl.ds(r0, t_len), :])
        y_ref[pl.ds(r0, t_len), :] = (_rms(y) * ng_ref[...]).astype(y_ref.dtype)
        return carry

    lax.fori_loop(0, n_chunks, finish, 0)


def _ssd_call(proj, p, h0, **where):
    length = where['length']
    dtb = jnp.zeros((2, 1, LANE), F32).at[:, 0, :SSD_H].set(p['ssd_dt_bias'].astype(F32))
    nega = jnp.zeros((2, 1, LANE), F32).at[:, 0, :SSD_H].set(-jnp.exp(p['ssd_A_log'].astype(F32)))
    dvec = jnp.repeat(p['ssd_D'].astype(F32), SSD_P).reshape(1, SSD_D)
    cw = p['ssd_conv_w'].astype(F32)
    cb = p['ssd_conv_b'].astype(F32).reshape(1, SSD_CONV_CH)
    consts = [cw[:, :SSD_D], cb[:, :SSD_D], cw[:, SSD_D:], cb[:, SSD_D:], dtb, nega, dvec,
              p['ssd_norm_g'].astype(F32).reshape(1, SSD_D)]
    seq_cols = [(SSD_D, COL_Z // SSD_D), (SSD_D, COL_XS // SSD_D), (512, COL_BC // 512),
                (LANE, COL_SMALL // LANE)]
    scratch = [pltpu.VMEM((length, SSD_D), F32), pltpu.VMEM((length, 512), F32),
               pltpu.VMEM((length, SSD_D), F32), pltpu.VMEM((2, SSD_N, SSD_D), F32)]
    return _mixer_call(_ssd_kernel, "ssd_mixer", proj, seq_cols, consts, h0, (2, SSD_H, SSD_P, SSD_N),
                       SSD_D, scratch, **where)


def _log_sigmoid(x):
    return jnp.minimum(x, 0.0) - jnp.log1p(jnp.exp(-jnp.abs(x)))


def _gla_kernel(*refs, sps, st_layer=0, **layout):
    (q_ref, k_ref, v_ref, g_ref, small_ref), consts, s0_ref, y_ref, sout_ref, scratch = _split_refs(refs, **layout)
    gw_ref, gb_ref, ng_ref = consts
    oacc_ref, st_ref, bc_ref = scratch
    n_rows = q_ref.shape[0]
    length = n_rows // sps
    t_len = GLA_T
    n_chunks = length // t_len
    st_ref[...] = jnp.zeros_like(st_ref)
    if s0_ref is not None:
        for s in range(sps):
            for d in range(2):
                for h in range(GLA_H):
                    st_ref[s, d, h * GLA_DV:(h + 1) * GLA_DV, h * GLA_DK:(h + 1) * GLA_DK] = s0_ref[s, d, h]
    oacc_ref[...] = jnp.zeros_like(oacc_ref)
    lower, upper = _tri_masks(t_len)
    masks = tuple(jnp.concatenate([m] * GLA_H, axis=0) for m in (lower, upper))
    q_head = (lax.broadcasted_iota(jnp.int32, (GLA_H * t_len, GLA_KW), 0) // t_len
              == lax.broadcasted_iota(jnp.int32, (GLA_H * t_len, GLA_KW), 1) // GLA_DK)
    st_diag = (lax.broadcasted_iota(jnp.int32, (GLA_VW, GLA_KW), 0) // GLA_DV
               == lax.broadcasted_iota(jnp.int32, (GLA_VW, GLA_KW), 1) // GLA_DK)
    scale = GLA_DK ** -0.5

    blk = 256
    rr = lax.broadcasted_iota(jnp.int32, (blk, blk), 0)
    cc = lax.broadcasted_iota(jnp.int32, (blk, blk), 1)
    same_chunk = rr // t_len == cc // t_len
    tri_blk = ((same_chunk & (rr >= cc)).astype(BF16), (same_chunk & (rr <= cc)).astype(BF16))
    gate_w = []
    for d in range(2):
        w1, w2 = _split2(gw_ref[d])
        gate_w.append(jnp.concatenate([w1, w1, w2], axis=0))

    def decay_sums(b, carry):
        r0 = pl.multiple_of(b * blk, blk)
        s1, s2 = _split2(small_ref[pl.ds(r0, blk), :])
        lhs = jnp.concatenate([s1, s2, s1], axis=1)
        for d in range(2):
            logit = jnp.dot(lhs, gate_w[d], preferred_element_type=F32) + gb_ref[d]
            la = _log_sigmoid(logit) / GLA_TAU
            bc3 = jnp.dot(tri_blk[d], jnp.concatenate(_split3(la), axis=1), preferred_element_type=F32)
            bc_ref[d, pl.ds(r0, blk), :] = bc3[:, :GLA_KW] + bc3[:, GLA_KW:2 * GLA_KW] + bc3[:, 2 * GLA_KW:]
        return carry

    lax.fori_loop(0, n_rows // blk, decay_sums, 0)

    def chunk(c, d, s):
        r0 = pl.multiple_of(s * length + c * t_len, t_len)
        bc = bc_ref[d, pl.ds(r0, t_len), :]
        end = t_len - 1 if d == 0 else 0
        mid = bc[t_len // 2:t_len // 2 + 1, :]
        tot = bc[end:end + 1, :]
        q = q_ref[pl.ds(r0, t_len), :] * scale
        k = k_ref[pl.ds(r0, t_len), :]
        v = v_ref[pl.ds(r0, t_len), :].astype(BF16)
        qt = q * jnp.exp(bc - mid)
        kt = k * jnp.exp(mid - bc)
        qs = q * jnp.exp(bc)
        kd = k * jnp.exp(tot - bc)
        q_rows = jnp.where(q_head, jnp.concatenate([qt] * GLA_H, axis=0), 0.0)
        att = jnp.where(masks[d], _bdot_nt(q_rows, kt), 0.0)
        o_all = jnp.dot(att.astype(BF16), v, preferred_element_type=F32)
        o = jnp.concatenate([o_all[h * t_len:(h + 1) * t_len, h * GLA_DV:(h + 1) * GLA_DV]
                             for h in range(GLA_H)], axis=1)
        st = st_ref[s, d]
        o = o + _bdot_nt(qs, st)
        upd = lax.dot_general(v, kd.astype(BF16), (((0,), (0,)), ((), ())), preferred_element_type=F32)
        st_ref[s, d] = st * jnp.exp(tot) + jnp.where(st_diag, upd, 0.0)
        oacc_ref[pl.ds(r0, t_len), :] += o

    def body(j, carry):
        for s in range(sps):
            chunk(j, 0, s)
            chunk(n_chunks - 1 - j, 1, s)
        return carry

    lax.fori_loop(0, n_chunks, body, 0)
    if sout_ref is not None:
        for s in range(sps):
            for d in range(2):
                for h in range(GLA_H):
                    sout_ref[s, st_layer, d, h] = st_ref[s, d, h * GLA_DV:(h + 1) * GLA_DV,
                                                         h * GLA_DK:(h + 1) * GLA_DK]

    def finish(c, carry):
        r0 = pl.multiple_of(c * t_len, t_len)
        gate = _silu(g_ref[pl.ds(r0, t_len), :])
        for h in range(GLA_H):
            vs = slice(h * GLA_DV, (h + 1) * GLA_DV)
            o = _rms(oacc_ref[pl.ds(r0, t_len), vs]) * ng_ref[...]
            y_ref[pl.ds(r0, t_len), vs] = (o * gate[:, vs]).astype(y_ref.dtype)
        return carry

    lax.fori_loop(0, sps * n_chunks, finish, 0)


def _gla_call(proj, p, s0t, **where):
    length = where['length']
    sps = where['sps']
    gw = jnp.zeros((2, LANE, GLA_KW), F32).at[:, GLA_RANK:2 * GLA_RANK, :].set(p['gla_gate_w'].astype(F32))
    consts = [gw, p['gla_gate_b'].astype(F32).reshape(2, 1, GLA_KW),
              p['gla_norm_g'].astype(F32).reshape(1, GLA_DV)]
    seq_cols = [(GLA_KW, COL_Q // GLA_KW), (GLA_KW, COL_K // GLA_KW), (GLA_VW, COL_V // GLA_VW),
                (GLA_VW, COL_G // GLA_VW), (LANE, COL_SMALL // LANE)]
    st_shape = (2, GLA_H, GLA_DV, GLA_DK)
    scratch = [pltpu.VMEM((sps * length, GLA_VW), F32), pltpu.VMEM((sps, 2, GLA_VW, GLA_KW), F32),
               pltpu.VMEM((2, sps * length, GLA_KW), F32)]
    return _mixer_call(_gla_kernel, "gla_mixer", proj, seq_cols, consts, s0t, st_shape, GLA_VW, scratch, **where)


def _lru_kernel(*refs, seg, sps, st_layer=0, **layout):
    (xb_ref, gb_ref), consts, h0_ref, y_ref, hout_ref, scratch = _split_refs(refs, **layout)
    cw_ref, cb_ref, w_ref, bias_ref, sp_ref = consts
    xr_ref, a_ref, u_ref = scratch
    n_rows = xb_ref.shape[0]
    length = n_rows // sps
    blk = 256
    _conv_cols(xb_ref, cw_ref, cb_ref, xr_ref, seg, False)

    def gates(c, carry):
        r0 = pl.multiple_of(c * blk, blk)
        xr = xr_ref[pl.ds(r0, blk), :]
        xbf = xr.astype(BF16)
        for d in range(2):
            pre = jnp.dot(xbf, w_ref[d], preferred_element_type=F32) + bias_ref[d]
            gates = 0.5 * (jnp.tanh(0.5 * pre) + 1.0)
            r = gates[:, :LRU_W]
            i = gates[:, LRU_W:]
            log_a = (-LRU_C) * r * sp_ref[d]
            a = jnp.exp(log_a)
            a_ref[d, pl.ds(r0, blk), :] = a
            u_ref[d, pl.ds(r0, blk), :] = jnp.sqrt(-jnp.tanh(log_a) * (a * a + 1.0)) * (i * xr)
        return carry

    lax.fori_loop(0, n_rows // blk, gates, 0)

    def scan(t, carry):
        out = []
        for s in range(sps):
            for d in range(2):
                row = s * length + (t if d == 0 else length - 1 - t)
                h = a_ref[d, pl.ds(row, 1), :] * carry[2 * s + d] + u_ref[d, pl.ds(row, 1), :]
                u_ref[d, pl.ds(row, 1), :] = h
                out.append(h)
        return tuple(out)

    if h0_ref is None:
        start = tuple(jnp.zeros((1, LRU_W), F32) for _ in range(2 * sps))
    else:
        start = tuple(h0_ref[s, d] for s in range(sps) for d in range(2))
    final = lax.fori_loop(0, length, scan, start)
    if hout_ref is not None:
        for s in range(sps):
            for d in range(2):
                hout_ref[s, st_layer, d] = final[2 * s + d]

    def finish(c, carry):
        r0 = pl.multiple_of(c * blk, blk)
        y = (u_ref[0, pl.ds(r0, blk), :] + u_ref[1, pl.ds(r0, blk), :]) * jax.nn.gelu(gb_ref[pl.ds(r0, blk), :])
        y_ref[pl.ds(r0, blk), :] = y.astype(y_ref.dtype)
        return carry

    lax.fori_loop(0, n_rows // blk, finish, 0)


def _block_diag(w):
    nb, bw, _ = w.shape
    eye = jnp.eye(nb, dtype=w.dtype)
    return (eye[:, None, :, None] * w[:, :, None, :]).reshape(nb * bw, nb * bw)


def _lru_call(proj, p, h0, **where):
    length = where['length']
    w = jnp.stack([jnp.concatenate([_block_diag(p['lru_wa'][d].astype(F32)),
                                    _block_diag(p['lru_wx'][d].astype(F32))], axis=1) for d in range(2)])
    bias = jnp.stack([jnp.concatenate([p['lru_ba'][d], p['lru_bx'][d]]) for d in range(2)]).astype(F32)
    sp = jax.nn.softplus(-p['lru_lambda'].astype(F32)).reshape(2, 1, LRU_W)
    consts = [p['lru_conv_w'].astype(F32), p['lru_conv_b'].astype(F32).reshape(1, LRU_W), w.astype(BF16),
              bias.reshape(2, 1, 2 * LRU_W), sp]
    seq_cols = [(LRU_W, COL_XB // LRU_W), (LRU_W, COL_GB // LRU_W)]
    rows = where['sps'] * length
    scratch = [pltpu.VMEM((rows, LRU_W), F32), pltpu.VMEM((2, rows, LRU_W), F32),
               pltpu.VMEM((2, rows, LRU_W), F32)]
    return _mixer_call(_lru_kernel, "lru_mixer", proj, seq_cols, consts, h0, (2, 1, LRU_W), LRU_W, scratch,
                       **where)


def kernel(x_prompt, x_sample, state_ssd, state_gla, state_lru, c, c_ctx, mod_w, mod_b, norm1_g, norm2_g, in_w, ssd_conv_w, ssd_conv_b, ssd_A_log, ssd_dt_bias, ssd_D, ssd_norm_g, gla_gate_w, gla_gate_b, gla_norm_g, lru_conv_w, lru_conv_b, lru_wa, lru_ba, lru_wx, lru_bx, lru_lambda, out_w, ffn_w1, ffn_w3, ffn_w2, moe_router, moe_w1, moe_w3, moe_w2, final_norm_g):
    bp, lp, d = x_prompt.shape
    bs, ls, _ = x_sample.shape
    n_p = bp * lp
    n_s = bs * ls
    n_all = n_p + n_s
    depth = in_w.shape[0]
    assert n_p % ls == 0 and 1 + bs <= MOD_ROWS

    cvec = jnp.zeros((MOD_ROWS, d), F32).at[0].set(c_ctx.astype(F32)).at[1:1 + bs].set(c.astype(F32))
    mods = _adaln(cvec, mod_w, mod_b)
    mods3 = mods.reshape(depth * MOD_ROWS * 6, 1, d)
    x = (x_prompt.reshape(n_p, d).astype(F32), x_sample.reshape(n_s, d).astype(F32))

    in_w_t = jnp.swapaxes(in_w, 1, 2)
    st_ssd = st_gla = st_lru = None
    for i in range(depth):
        p = {'ssd_conv_w': ssd_conv_w[i], 'ssd_conv_b': ssd_conv_b[i], 'ssd_A_log': ssd_A_log[i],
             'ssd_dt_bias': ssd_dt_bias[i], 'ssd_D': ssd_D[i], 'ssd_norm_g': ssd_norm_g[i],
             'gla_gate_w': gla_gate_w[i], 'gla_gate_b': gla_gate_b[i], 'gla_norm_g': gla_norm_g[i],
             'lru_conv_w': lru_conv_w[i], 'lru_conv_b': lru_conv_b[i], 'lru_wa': lru_wa[i],
             'lru_ba': lru_ba[i], 'lru_wx': lru_wx[i], 'lru_bx': lru_bx[i], 'lru_lambda': lru_lambda[i]}
        h = _norm_mod(x, norm1_g[i], mods3, i, 0, n_p, ls)
        proj = _in_proj(h, in_w_t, i)
        ctx = dict(n_seq=bp, length=lp, row_off=0, n_rows_total=n_all, y_prev=None, emit_state=True)
        lat = dict(n_seq=bs, length=ls, row_off=n_p // ls, n_rows_total=n_all, emit_state=False)
        y_ssd, st_ssd = _ssd_call(proj, p, None, seg=lp, prev_state=st_ssd, **ctx)
        y_ssd, = _ssd_call(proj, p, state_ssd[:, i].astype(F32), seg=GRID_W, y_prev=y_ssd, **lat)
        y_gla, st_gla = _gla_call(proj, p, None, sps=MIX_SPS, prev_state=st_gla, **ctx)
        y_gla, = _gla_call(proj, p, jnp.swapaxes(state_gla[:, i].astype(F32), -1, -2), sps=MIX_SPS,
                           y_prev=y_gla, **lat)
        y_lru, st_lru = _lru_call(proj, p, None, seg=lp, sps=MIX_SPS, prev_state=st_lru, **ctx)
        y_lru, = _lru_call(proj, p, state_lru[:, i].astype(F32).reshape(bs, 2, 1, LRU_W), seg=GRID_W,
                           sps=MIX_SPS, y_prev=y_lru, **lat)
        j = i // 2
        if i % 2 == 1:
            x, h2, route = _out_proj(y_ssd, y_gla, y_lru, out_w, x, norm2_g[i], mods3, i, n_p, ls,
                                     router=moe_router[j])
            x = _moe_ffn(x, h2, route, moe_w1[j], moe_w3[j], moe_w2[j], mods3, i, n_p, ls,
                         final_g=final_norm_g if i == depth - 1 else None)
        else:
            x, h2 = _out_proj(y_ssd, y_gla, y_lru, out_w, x, norm2_g[i], mods3, i, n_p, ls)
            x = _ffn_down(_ffn_up(h2, ffn_w1[j], ffn_w3[j]), ffn_w2[j], x, mods3, i, n_p, ls)
    if isinstance(x, tuple):
        y_p, y_s = x
    else:
        y_p = _final_norm(x, final_norm_g, 0, n_p)
        y_s = _final_norm(x, final_norm_g, n_p, n_s)
    return (y_p.reshape(bp, lp, d), y_s.reshape(bs, ls, d), st_ssd, jnp.swapaxes(st_gla, -1, -2),
            st_lru.reshape(bp, depth, 2, LRU_W))
```

```python
import functools

import jax
import jax.numpy as jnp
from jax import lax
from jax.experimental import pallas as pl
from jax.experimental.pallas import tpu as pltpu

F32 = jnp.float32
BF16 = jnp.bfloat16

D_MODEL = 2048
GRID_W = 64
SSD_D = D_MODEL // 2
SSD_P = 64
SSD_H = SSD_D // SSD_P
SSD_G = 2
SSD_N = 128
GLA_H = 4
GLA_VW = D_MODEL // 4
GLA_DV = GLA_VW // GLA_H
GLA_DK = GLA_DV // 2
GLA_KW = GLA_H * GLA_DK
GLA_RANK = 16
GLA_TAU = 16.0
LRU_W = D_MODEL // 4
LRU_NB = 8
LRU_BW = LRU_W // LRU_NB
LRU_C = 8.0
SSD_CONV_CH = SSD_D + 2 * SSD_G * SSD_N
N_EXPERTS = 8
TOP_K = 2
EPS = 1e-6

LANE = 128
SSD_T = 128
GLA_T = 64
MIX_SPS = 2
MOD_ROWS = 8
VMEM_LIMIT = 56 * 1024 * 1024

COL_Z = 0
COL_XS = 1024
COL_BC = 2048
COL_Q = 2560
COL_K = 2816
COL_V = 3072
COL_G = 3584
COL_XB = 4096
COL_GB = 4608
COL_SMALL = 5120
IN_PAD = 5632


def _cparams(n_axes):
    return pltpu.CompilerParams(dimension_semantics=("arbitrary",) * n_axes,
                                vmem_limit_bytes=VMEM_LIMIT)


def _bdot(a, b):
    return jnp.dot(a.astype(BF16), b.astype(BF16), preferred_element_type=F32)


def _bdot_nt(a, b):
    return lax.dot_general(a.astype(BF16), b.astype(BF16), (((1,), (1,)), ((), ())),
                           preferred_element_type=F32)


def _bdot_tn(a, b):
    return lax.dot_general(a.astype(BF16), b.astype(BF16), (((0,), (0,)), ((), ())),
                           preferred_element_type=F32)


def _split2(a):
    a1 = a.astype(BF16)
    a2 = (a - a1.astype(F32)).astype(BF16)
    return a1, a2


def _split3(a):
    a1 = a.astype(BF16)
    r = a - a1.astype(F32)
    a2 = r.astype(BF16)
    a3 = (r - a2.astype(F32)).astype(BF16)
    return a1, a2, a3


def _exact_lhs_dot(m_bf16, a):
    a1, a2, a3 = _split3(a)
    f = lambda z: jnp.dot(m_bf16, z, preferred_element_type=F32)
    return f(a1) + f(a2) + f(a3)


def _dot3(a, b):
    a1, a2 = _split2(a)
    b1, b2 = _split2(b)
    f = lambda x, y: jnp.dot(x, y, preferred_element_type=F32)
    rows = a.shape[0]
    hi = f(jnp.concatenate([a1, a2], axis=0), b1)
    return hi[:rows] + (f(a1, b2) + hi[rows:])


def _softplus(x):
    return jnp.maximum(x, 0.0) + jnp.log1p(jnp.exp(-jnp.abs(x)))


def _silu(x):
    return x * jax.nn.sigmoid(x)


def _mod_row(m, tm, n_prompt_rows, dec_seq):
    r0 = m * tm
    return jnp.where(r0 < n_prompt_rows, 0, 1 + (r0 - n_prompt_rows) // dec_seq)


def _adaln_kernel(c_ref, w_ref, b_ref, o_ref):
    s = _silu(c_ref[...])
    o_ref[...] = _dot3(s, w_ref[...]) + b_ref[...]


def _adaln(cvec8, mod_w, mod_b):
    depth, d, n = mod_w.shape
    tn = 1024
    return pl.pallas_call(
        _adaln_kernel,
        grid=(depth, n // tn),
        in_specs=[pl.BlockSpec((MOD_ROWS, d), lambda i, j: (0, 0)),
                  pl.BlockSpec((None, d, tn), lambda i, j: (i, 0, j)),
                  pl.BlockSpec((None, 1, tn), lambda i, j: (i, 0, j))],
        out_specs=pl.BlockSpec((None, MOD_ROWS, tn), lambda i, j: (i, 0, j)),
        out_shape=jax.ShapeDtypeStruct((depth, MOD_ROWS, n), F32),
        compiler_params=_cparams(2),
        name="adaln",
    )(cvec8, mod_w, mod_b.reshape(depth, 1, n))


def _rms(x):
    return x * lax.rsqrt(jnp.mean(x * x, axis=-1, keepdims=True) + EPS)


def _stacked_specs(x, block, row_tile, col_block):
    if not isinstance(x, tuple):
        return [pl.BlockSpec(block, lambda *g: (row_tile(*g), col_block(*g)))], [x], 0
    split = x[0].shape[0] // block[0]
    specs = [pl.BlockSpec(block, lambda *g: (jnp.minimum(row_tile(*g), split - 1), col_block(*g))),
             pl.BlockSpec(block, lambda *g: (jnp.maximum(row_tile(*g) - split, 0), col_block(*g)))]
    return specs, list(x), split


def _stacked_tile(x_refs, m, split):
    if len(x_refs) == 1:
        return x_refs[0][...]
    return jnp.where(m < split, x_refs[0][...], x_refs[1][...])


def _norm_mod_kernel(*refs, split):
    g_ref, sh_ref, sc_ref, o_ref = refs[-4:]
    y = _rms(_stacked_tile(refs[:-4], pl.program_id(0), split)) * g_ref[...]
    o_ref[...] = (y * (1.0 + sc_ref[...]) + sh_ref[...]).astype(o_ref.dtype)


def _top2_route(h, router_pad):
    logits = _dot3(h, router_pad)
    lane = lax.broadcasted_iota(jnp.int32, logits.shape, 1)
    neg = jnp.float32(-jnp.inf)
    lg = jnp.where(lane < N_EXPERTS, logits, neg)
    m1 = jnp.max(lg, axis=-1, keepdims=True)
    i1 = jnp.min(jnp.where(lg == m1, lane, LANE), axis=-1, keepdims=True)
    lg2 = jnp.where(lane == i1, neg, lg)
    m2 = jnp.max(lg2, axis=-1, keepdims=True)
    i2 = jnp.min(jnp.where(lg2 == m2, lane, LANE), axis=-1, keepdims=True)
    e2 = jnp.exp(m2 - m1)
    den = 1.0 + e2
    g1 = 1.0 / den
    g2 = e2 / den
    return jnp.where(lane == 0, i1.astype(F32),
                     jnp.where(lane == 1, i2.astype(F32),
                               jnp.where(lane == 2, g1, jnp.where(lane == 3, g2, 0.0))))


def _norm_mod(x, g, mods3, layer, which, n_prompt_rows, dec_seq):
    d = g.shape[0]
    m_rows = sum(part.shape[0] for part in x) if isinstance(x, tuple) else x.shape[0]
    tm = 512
    base = layer * MOD_ROWS * 6

    def mod_spec(k):
        return pl.BlockSpec((None, 1, d),
                            lambda m: (base + _mod_row(m, tm, n_prompt_rows, dec_seq) * 6 + k, 0, 0))

    x_specs, x_args, split = _stacked_specs(x, (tm, d), lambda m: m, lambda m: 0)
    in_specs = x_specs + [pl.BlockSpec((1, d), lambda m: (0, 0)), mod_spec(3 * which), mod_spec(3 * which + 1)]
    args = x_args + [g.reshape(1, d), mods3, mods3]
    return pl.pallas_call(
        functools.partial(_norm_mod_kernel, split=split), grid=(m_rows // tm,), in_specs=in_specs,
        out_specs=pl.BlockSpec((tm, d), lambda m: (m, 0)), out_shape=jax.ShapeDtypeStruct((m_rows, d), BF16),
        compiler_params=_cparams(1), name="norm_mod")(*args)


def _final_norm_kernel(x_ref, g_ref, o_ref):
    o_ref[...] = _rms(x_ref[...]) * g_ref[...]


def _final_norm(x, g, row0, n_rows):
    d = x.shape[1]
    tm = 512
    off = row0 // tm
    return pl.pallas_call(
        _final_norm_kernel, grid=(n_rows // tm,),
        in_specs=[pl.BlockSpec((tm, d), lambda m: (m + off, 0)), pl.BlockSpec((1, d), lambda m: (0, 0))],
        out_specs=pl.BlockSpec((tm, d), lambda m: (m, 0)),
        out_shape=jax.ShapeDtypeStruct((n_rows, d), F32),
        compiler_params=_cparams(1), name="final_norm")(x, g.reshape(1, d))


IN_TN = 512
IN_DT_COL = SSD_D + SSD_CONV_CH
IN_GRAW_COL = IN_DT_COL + SSD_H + 2 * GLA_KW + 2 * GLA_VW


IN_UNITS = 2


IN_RING = 3


def _in_proj_kernel(x_hbm, *rest):
    unit_refs = [rest[2 * u:2 * u + 2] for u in range(IN_UNITS)]
    wdt_ref, wgr_ref, o_ref, wbf_ref, xbuf_ref, xsem = rest[2 * IN_UNITS:]
    j = pl.program_id(0)
    small_unit = COL_SMALL // IN_TN

    n_row_tiles = pl.num_programs(1)
    n_steps = pl.num_programs(0) * n_row_tiles
    step = j * n_row_tiles + pl.program_id(1)
    tm = xbuf_ref.shape[1]

    def x_copy(s):
        r0 = pl.multiple_of((s % n_row_tiles) * tm, tm)
        slot = s % IN_RING
        return pltpu.make_async_copy(x_hbm.at[pl.ds(r0, tm)], xbuf_ref.at[slot], xsem.at[slot])

    @pl.when(step == 0)
    def _():
        for s in range(IN_RING - 1):
            x_copy(s).start()

    @pl.when(step + IN_RING - 1 < n_steps)
    def _():
        x_copy(step + IN_RING - 1).start()

    x_copy(step).wait()
    x_ref = xbuf_ref.at[step % IN_RING]

    @pl.when(pl.program_id(1) == 0)
    def _():
        for u, (wm_ref, wn_ref) in enumerate(unit_refs):
            unit = j * IN_UNITS + u
            rows = slice(u * IN_TN, (u + 1) * IN_TN)

            def shifted(s, wm_ref=wm_ref, wn_ref=wn_ref, rows=rows):
                w = jnp.concatenate([wm_ref[...], wn_ref[...]], axis=0)
                wbf_ref[rows] = w[s:s + IN_TN].astype(BF16)

            @pl.when(unit < COL_Q // IN_TN)
            def _(wm_ref=wm_ref, rows=rows):
                wbf_ref[rows] = wm_ref[...].astype(BF16)

            @pl.when((unit >= COL_Q // IN_TN) & (unit < COL_XB // IN_TN))
            def _(shifted=shifted):
                shifted(SSD_H)

            @pl.when((unit >= COL_XB // IN_TN) & (unit < small_unit))
            def _(shifted=shifted):
                shifted(SSD_H + GLA_RANK)

            @pl.when(unit >= small_unit)
            def _(rows=rows):
                wbf_ref[rows] = jnp.zeros((IN_TN, wbf_ref.shape[1]), BF16)

            @pl.when(unit == small_unit)
            def _(u=u):
                wbf_ref[u * IN_TN:u * IN_TN + SSD_H] = wdt_ref[...].astype(BF16)
                wbf_ref[u * IN_TN + SSD_H:u * IN_TN + SSD_H + GLA_RANK] = wgr_ref[...].astype(BF16)

    nt = (((1,), (1,)), ((), ()))
    has_small = (j + 1) * IN_UNITS > small_unit

    @pl.when(jnp.logical_not(has_small))
    def _():
        o_ref[...] = lax.dot_general(x_ref[...], wbf_ref[...], nt, preferred_element_type=F32)

    @pl.when(has_small)
    def _():
        full = (small_unit % IN_UNITS) * IN_TN
        o_ref[:, :full + LANE] = lax.dot_general(x_ref[...], wbf_ref[0:full + LANE], nt,
                                                 preferred_element_type=F32)
        o_ref[:, full + LANE:] = jnp.zeros((o_ref.shape[0], o_ref.shape[1] - full - LANE), F32)


def _in_proj(h, in_w_t, layer):
    m_rows, k = h.shape
    tm, tn = 1024, IN_TN * IN_UNITS
    n_main = COL_SMALL // IN_TN
    shift_max = SSD_H + GLA_RANK
    last_next_block = in_w_t.shape[1] // shift_max - 1
    unit_specs = []
    for u in range(IN_UNITS):
        unit_specs += [
            pl.BlockSpec((None, IN_TN, k),
                         lambda j, m, u=u: (layer, jnp.minimum(j * IN_UNITS + u, n_main - 1), 0)),
            pl.BlockSpec((None, shift_max, k),
                         lambda j, m, u=u: (layer, jnp.minimum((j * IN_UNITS + u + 1) * (IN_TN // shift_max),
                                                               last_next_block), 0))]
    return pl.pallas_call(
        _in_proj_kernel, grid=(pl.cdiv(IN_PAD, tn), m_rows // tm),
        in_specs=[pl.BlockSpec(memory_space=pl.ANY)] + unit_specs + [
                  pl.BlockSpec((None, SSD_H, k), lambda j, m: (layer, IN_DT_COL // SSD_H, 0)),
                  pl.BlockSpec((None, GLA_RANK, k), lambda j, m: (layer, IN_GRAW_COL // GLA_RANK, 0))],
        out_specs=pl.BlockSpec((tm, tn), lambda j, m: (m, j)),
        out_shape=jax.ShapeDtypeStruct((m_rows, IN_PAD), F32),
        scratch_shapes=[pltpu.VMEM((tn, k), BF16), pltpu.VMEM((IN_RING, tm, k), BF16),
                        pltpu.SemaphoreType.DMA((IN_RING,))],
        compiler_params=_cparams(2), name="in_proj")(h, *([in_w_t] * (2 * IN_UNITS + 2)))


def _out_proj_kernel(y1_ref, y2_ref, y3_ref, w_hbm, *rest, n_res, split, layer, route):
    res_refs = rest[:n_res]
    gate_ref, g_ref, sh_ref, sc_ref = rest[n_res:n_res + 4]
    pos = n_res + 4
    r_ref = rest[pos] if route else None
    pos += int(route)
    x_out, h_out = rest[pos:pos + 2]
    pos += 2
    route_out = rest[pos] if route else None
    pos += int(route)
    wst_ref, wbf_ref, sem = rest[pos:]
    m = pl.program_id(0)

    @pl.when(m == 0)
    def _():
        copy = pltpu.make_async_copy(w_hbm.at[layer], wst_ref, sem.at[0])
        copy.start()
        copy.wait()
        wbf_ref[...] = wst_ref[...].astype(BF16)

    k1 = y1_ref.shape[1]
    k2 = k1 + y2_ref.shape[1]
    acc = jnp.dot(y1_ref[...], wbf_ref[0:k1, :], preferred_element_type=F32)
    acc += jnp.dot(y2_ref[...], wbf_ref[k1:k2, :], preferred_element_type=F32)
    acc += jnp.dot(y3_ref[...], wbf_ref[k2:, :], preferred_element_type=F32)
    x_new = _stacked_tile(res_refs, m, split) + gate_ref[...] * acc
    x_out[...] = x_new
    h = (_rms(x_new) * g_ref[...]) * (1.0 + sc_ref[...]) + sh_ref[...]
    h_out[...] = h.astype(h_out.dtype)
    if route:
        route_out[...] = _top2_route(h, r_ref[...])


def _out_proj(y1, y2, y3, w, res, norm_g, mods3, layer, n_prompt_rows, dec_seq, router=None):
    m_rows = y1.shape[0]
    k, d = w.shape[1:]
    tm = 256
    base = layer * MOD_ROWS * 6
    route = router is not None

    def mod_spec(which):
        return pl.BlockSpec((None, 1, d),
                            lambda m: (base + _mod_row(m, tm, n_prompt_rows, dec_seq) * 6 + which, 0, 0))

    row = lambda width: pl.BlockSpec((tm, width), lambda m: (m, 0))
    res_specs, res_args, split = _stacked_specs(res, (tm, d), lambda m: m, lambda m: 0)
    in_specs = [row(y1.shape[1]), row(y2.shape[1]), row(y3.shape[1]), pl.BlockSpec(memory_space=pl.ANY)]
    in_specs += res_specs + [mod_spec(2), pl.BlockSpec((1, d), lambda m: (0, 0)), mod_spec(3), mod_spec(4)]
    args = [y1, y2, y3, w] + res_args + [mods3, norm_g.reshape(1, d), mods3, mods3]
    out_specs = [row(d), row(d)]
    out_shape = [jax.ShapeDtypeStruct((m_rows, d), F32), jax.ShapeDtypeStruct((m_rows, d), F32 if route else BF16)]
    if route:
        in_specs.append(pl.BlockSpec((d, LANE), lambda m: (0, 0)))
        args.append(jnp.zeros((d, LANE), F32).at[:, :N_EXPERTS].set(router.astype(F32)))
        out_specs.append(row(LANE))
        out_shape.append(jax.ShapeDtypeStruct((m_rows, LANE), F32))
    kern = functools.partial(_out_proj_kernel, n_res=len(res_args), split=split, layer=layer, route=route)
    return pl.pallas_call(
        kern, grid=(m_rows // tm,), in_specs=in_specs, out_specs=tuple(out_specs), out_shape=tuple(out_shape),
        scratch_shapes=[pltpu.VMEM((k, d), F32), pltpu.VMEM((k, d), BF16), pltpu.SemaphoreType.DMA((1,))],
        compiler_params=_cparams(1), name="out_proj")(*args)


def _ffn_up_kernel(x_ref, w1_ref, w3_ref, o_ref, w1bf_ref, w3bf_ref):
    @pl.when(pl.program_id(1) == 0)
    def _():
        w1bf_ref[...] = w1_ref[...].astype(BF16)
        w3bf_ref[...] = w3_ref[...].astype(BF16)

    a = jnp.dot(x_ref[...], w1bf_ref[...], preferred_element_type=F32)
    b = jnp.dot(x_ref[...], w3bf_ref[...], preferred_element_type=F32)
    o_ref[...] = (_silu(a) * b).astype(o_ref.dtype)


def _ffn_up(h, w1, w3):
    m_rows, d = h.shape
    f = w1.shape[1]
    tm, tf = 1024, 512
    return pl.pallas_call(
        _ffn_up_kernel, grid=(pl.cdiv(f, tf), m_rows // tm),
        in_specs=[pl.BlockSpec((tm, d), lambda j, m: (m, 0)),
                  pl.BlockSpec((d, tf), lambda j, m: (0, j)),
                  pl.BlockSpec((d, tf), lambda j, m: (0, j))],
        out_specs=pl.BlockSpec((tm, tf), lambda j, m: (m, j)),
        out_shape=jax.ShapeDtypeStruct((m_rows, f), BF16),
        scratch_shapes=[pltpu.VMEM((d, tf), BF16), pltpu.VMEM((d, tf), BF16)],
        compiler_params=_cparams(2), name="ffn_up")(h, w1, w3)


def _ffn_down_kernel(g_ref, w_ref, res_ref, gate_ref, o_ref, wbf_ref):
    @pl.when(pl.program_id(1) == 0)
    def _():
        wbf_ref[...] = w_ref[...].astype(BF16)

    acc = jnp.dot(g_ref[...], wbf_ref[...], preferred_element_type=F32)
    o_ref[...] = res_ref[...] + gate_ref[...] * acc


def _ffn_down(g, w2, res, mods3, layer, n_prompt_rows, dec_seq):
    m_rows, d = res.shape
    f = w2.shape[0]
    tm, tn = 512, 512
    base = layer * MOD_ROWS * 6
    return pl.pallas_call(
        _ffn_down_kernel, grid=(d // tn, m_rows // tm),
        in_specs=[pl.BlockSpec((tm, f), lambda j, m: (m, 0)),
                  pl.BlockSpec((f, tn), lambda j, m: (0, j)),
                  pl.BlockSpec((tm, tn), lambda j, m: (m, j)),
                  pl.BlockSpec((None, 1, tn),
                               lambda j, m: (base + _mod_row(m, tm, n_prompt_rows, dec_seq) * 6 + 5, 0, j))],
        out_specs=pl.BlockSpec((tm, tn), lambda j, m: (m, j)),
        out_shape=jax.ShapeDtypeStruct((m_rows, d), F32),
        scratch_shapes=[pltpu.VMEM((f, tn), BF16)],
        compiler_params=_cparams(2), name="ffn_down")(g, w2, res, mods3)


MOE_TM = 256
MOE_ROW_QUARTERS = 4


def _grouped_weights(plan_refs, w_hbm, stages, casts, sems):
    te_ref, first_ref, next_e_ref, last_ref = plan_refs
    j = pl.program_id(0)
    m = pl.program_id(1)
    n_col_tiles = pl.num_programs(0)

    def fetch(e, jj, k):
        width = stages[k].shape[1]
        src = w_hbm[k].at[e, :, pl.ds(pl.multiple_of(jj * width, width), width)]
        return pltpu.make_async_copy(src, stages[k], sems.at[k])

    @pl.when((j == 0) & (m == 0))
    def _():
        for k in range(len(stages)):
            fetch(te_ref[0], 0, k).start()

    @pl.when(first_ref[m] == 1)
    def _():
        for k in range(len(stages)):
            fetch(te_ref[m], j, k).wait()
            casts[k][...] = stages[k][...].astype(BF16)
        next_j = j + last_ref[m]

        @pl.when(next_j < n_col_tiles)
        def _():
            for k in range(len(stages)):
                fetch(next_e_ref[m], next_j, k).start()


def _by_live_rows(live, o_ref, compute):
    rows = o_ref.shape[0]
    step = rows // MOE_ROW_QUARTERS
    for q in range(1, MOE_ROW_QUARTERS + 1):
        top = q * step

        @pl.when((live > top - step) & (live <= top))
        def _(top=top):
            o_ref[0:top] = compute(slice(0, top))
            if top < rows:
                o_ref[top:] = jnp.zeros((rows - top, o_ref.shape[1]), o_ref.dtype)

    @pl.when(live == 0)
    def _():
        o_ref[...] = jnp.zeros_like(o_ref)


def _moe_up_kernel(te_ref, first_ref, next_e_ref, last_ref, live_ref, x_ref, w1_hbm, w3_hbm, o_ref,
                   w1st_ref, w3st_ref, w1bf_ref, w3bf_ref, sems):
    _grouped_weights((te_ref, first_ref, next_e_ref, last_ref), (w1_hbm, w3_hbm), (w1st_ref, w3st_ref),
                     (w1bf_ref, w3bf_ref), sems)

    def compute(rows):
        x = x_ref[rows].astype(BF16)
        a = jnp.dot(x, w1bf_ref[...], preferred_element_type=F32)
        b = jnp.dot(x, w3bf_ref[...], preferred_element_type=F32)
        return (_silu(a) * b).astype(o_ref.dtype)

    _by_live_rows(live_ref[pl.program_id(1)], o_ref, compute)


def _moe_up(xs, w1, w3, plan):
    n_rows, d = xs.shape
    f = w1.shape[2]
    tm, tf = MOE_TM, 1792
    n_plan = len(plan)
    grid_spec = pltpu.PrefetchScalarGridSpec(
        num_scalar_prefetch=n_plan, grid=(f // tf, n_rows // tm),
        in_specs=[pl.BlockSpec((tm, d), lambda j, m, *_: (m, 0)),
                  pl.BlockSpec(memory_space=pl.ANY), pl.BlockSpec(memory_space=pl.ANY)],
        out_specs=pl.BlockSpec((tm, tf), lambda j, m, *_: (m, j)),
        scratch_shapes=[pltpu.VMEM((d, tf), F32), pltpu.VMEM((d, tf), F32),
                        pltpu.VMEM((d, tf), BF16), pltpu.VMEM((d, tf), BF16),
                        pltpu.SemaphoreType.DMA((2,))])
    return pl.pallas_call(
        _moe_up_kernel, grid_spec=grid_spec,
        out_shape=jax.ShapeDtypeStruct((n_rows, f), BF16),
        compiler_params=_cparams(2), name="moe_up")(*plan, xs, w1, w3)


def _moe_down_kernel(te_ref, first_ref, next_e_ref, last_ref, live_ref, g_ref, w_hbm, o_ref,
                     wst_ref, wbf_ref, sems):
    _grouped_weights((te_ref, first_ref, next_e_ref, last_ref), (w_hbm,), (wst_ref,), (wbf_ref,), sems)

    def compute(rows):
        return jnp.dot(g_ref[rows], wbf_ref[...], preferred_element_type=F32)

    _by_live_rows(live_ref[pl.program_id(1)], o_ref, compute)


def _moe_down(g, w2, plan):
    n_rows, f = g.shape
    d = w2.shape[2]
    tm, tn = MOE_TM, 1024
    grid_spec = pltpu.PrefetchScalarGridSpec(
        num_scalar_prefetch=len(plan), grid=(d // tn, n_rows // tm),
        in_specs=[pl.BlockSpec((tm, f), lambda j, m, *_: (m, 0)),
                  pl.BlockSpec(memory_space=pl.ANY)],
        out_specs=pl.BlockSpec((tm, tn), lambda j, m, *_: (m, j)),
        scratch_shapes=[pltpu.VMEM((f, tn), F32), pltpu.VMEM((f, tn), BF16),
                        pltpu.SemaphoreType.DMA((1,))])
    return pl.pallas_call(
        _moe_down_kernel, grid_spec=grid_spec,
        out_shape=jax.ShapeDtypeStruct((n_rows, d), F32),
        compiler_params=_cparams(2), name="moe_down")(*plan, g, w2)


def _combine_kernel(x_ref, ya_ref, yb_ref, route_ref, gate_ref, *rest):
    ga = route_ref[:, TOP_K:TOP_K + 1]
    gb = route_ref[:, TOP_K + 1:TOP_K + 2]
    v = x_ref[...] + gate_ref[...] * (ga * ya_ref[...] + gb * yb_ref[...])
    if len(rest) == 2:
        g_ref, o_ref = rest
        o_ref[...] = _rms(v) * g_ref[...]
    else:
        rest[0][...] = v


def _moe_combine(x, ya, yb, route, mods3, layer, n_prompt_rows, dec_seq, row0, n_rows, final_g):
    d = x.shape[1]
    tm = 512
    off = row0 // tm
    base = layer * MOD_ROWS * 6
    row = pl.BlockSpec((tm, d), lambda m: (m + off, 0))
    in_specs = [row, row, row, pl.BlockSpec((tm, LANE), lambda m: (m + off, 0)),
                pl.BlockSpec((None, 1, d),
                             lambda m: (base + _mod_row(m + off, tm, n_prompt_rows, dec_seq) * 6 + 5, 0, 0))]
    args = [x, ya, yb, route, mods3]
    if final_g is not None:
        in_specs.append(pl.BlockSpec((1, d), lambda m: (0, 0)))
        args.append(final_g.reshape(1, d))
    return pl.pallas_call(
        _combine_kernel, grid=(n_rows // tm,), in_specs=in_specs,
        out_specs=pl.BlockSpec((tm, d), lambda m: (m, 0)),
        out_shape=jax.ShapeDtypeStruct((n_rows, d), F32),
        compiler_params=_cparams(1), name="moe_combine")(*args)


def _take_rows(a, idx):
    return a.at[idx].get(mode="promise_in_bounds")


def _moe_ffn(x, h, route, w1, w3, w2, mods3, layer, n_prompt_rows, dec_seq, final_g=None):
    m_rows, d = x.shape
    tm = MOE_TM
    n_slots = m_rows * TOP_K
    n_rows = n_slots + N_EXPERTS * tm
    n_tiles = n_rows // tm
    top_i = route[:, 0:TOP_K].astype(jnp.int32)
    flat_e = top_i.reshape(n_slots)
    onehot = (flat_e[:, None] == jnp.arange(N_EXPERTS, dtype=jnp.int32)[None, :]).astype(jnp.int32)
    counts = jnp.sum(onehot, axis=0)
    rank = jnp.sum((jnp.cumsum(onehot, axis=0) - onehot) * onehot, axis=1)
    padded = ((counts + tm - 1) // tm) * tm
    group_end = jnp.cumsum(padded)
    group_start = group_end - padded
    pos = group_start[flat_e] + rank
    src_token = (jnp.arange(n_rows, dtype=jnp.int32) % m_rows).at[pos].set(
        jnp.arange(n_slots, dtype=jnp.int32) // TOP_K, unique_indices=True, mode="promise_in_bounds")
    n_used = (group_end[-1] // tm).astype(jnp.int32)
    tiles = jnp.arange(n_tiles, dtype=jnp.int32)
    tile_start = jnp.minimum(tiles, n_used - 1) * tm
    tile_expert = jnp.sum((tile_start[:, None] >= group_end[None, :]).astype(jnp.int32), axis=1)
    tile_expert = jnp.minimum(tile_expert, N_EXPERTS - 1).astype(jnp.int32)
    prev_expert = jnp.concatenate([jnp.full((1,), -1, jnp.int32), tile_expert[:-1]])
    first = ((tile_expert != prev_expert) & (tiles < n_used)).astype(jnp.int32)
    ids = jnp.arange(N_EXPERTS, dtype=jnp.int32)
    later = jnp.where((ids[None, :] > ids[:, None]) & (counts[None, :] > 0), ids[None, :], N_EXPERTS)
    next_expert = jnp.min(later, axis=1)
    is_last = next_expert == N_EXPERTS
    next_expert = jnp.where(is_last, tile_expert[0], next_expert)
    live = jnp.clip(group_start[tile_expert] + counts[tile_expert] - tiles * tm, 0, tm)
    live = jnp.where(tiles < n_used, live, 0).astype(jnp.int32)
    plan = (tile_expert, first, next_expert[tile_expert], is_last.astype(jnp.int32)[tile_expert], live)
    xs = _take_rows(h, src_token)
    g = _moe_up(xs, w1, w3, plan)
    ys = _moe_down(g, w2, plan)
    pos2 = pos.reshape(m_rows, TOP_K)
    ya = _take_rows(ys, pos2[:, 0])
    yb = _take_rows(ys, pos2[:, 1])
    common = (x, ya, yb, route, mods3, layer, n_prompt_rows, dec_seq)
    if final_g is None:
        return _moe_combine(*common, 0, m_rows, None)
    return (_moe_combine(*common, 0, n_prompt_rows, final_g),
            _moe_combine(*common, n_prompt_rows, m_rows - n_prompt_rows, final_g))


def _conv_cols(x_ref, w_ref, b_ref, o_ref, seg, act):
    length, ch = x_ref.shape
    t = lax.broadcasted_iota(jnp.int32, (length, LANE), 0) % seg

    def body(cb, carry):
        c0 = pl.multiple_of(cb * LANE, LANE)
        x = x_ref[:, pl.ds(c0, LANE)]
        w = w_ref[:, pl.ds(c0, LANE)]
        xm1 = jnp.where(t >= 1, pltpu.roll(x, 1, 0), 0.0)
        xp1 = jnp.where(t < seg - 1, pltpu.roll(x, length - 1, 0), 0.0)
        xp2 = jnp.where(t < seg - 2, pltpu.roll(x, length - 2, 0), 0.0)
        y = b_ref[:, pl.ds(c0, LANE)] + (w[0:1] * xm1 + w[1:2] * x + w[2:3] * xp1 + w[3:4] * xp2)
        if act:
            y = _silu(y)
        o_ref[:, pl.ds(c0, LANE)] = y
        return carry

    lax.fori_loop(0, ch // LANE, body, 0)


def _tri_masks(t_len):
    r = lax.broadcasted_iota(jnp.int32, (t_len, t_len), 0)
    c = lax.broadcasted_iota(jnp.int32, (t_len, t_len), 1)
    return r >= c, r <= c


def _mixer_call(body, name, proj, seq_cols, consts, init, st_shape, width, scratch, *, n_seq, length,
                row_off, n_rows_total, y_prev, emit_state, prev_state=None, **static):
    st_nd = len(st_shape)
    sps = static.get('sps')
    seq_dim = None if sps is None else sps
    sps = 1 if sps is None else sps
    assert n_seq % sps == 0 and row_off % sps == 0
    rows = sps * length
    row_off = row_off // sps
    n_steps = n_seq // sps
    n_fill = 0
    if y_prev is None:
        assert row_off == 0 and n_rows_total % rows == 0
        n_fill = n_rows_total // rows - n_steps
    own = lambda b: jnp.minimum(b, n_steps - 1)
    in_specs = [pl.BlockSpec((rows, w), lambda b, cb=cb: (own(b) + row_off, cb)) for w, cb in seq_cols]
    args = [proj] * len(seq_cols)
    for c in consts:
        in_specs.append(pl.BlockSpec(c.shape, lambda b, nd=c.ndim: (0,) * nd))
        args.append(c)
    if init is not None:
        in_specs.append(pl.BlockSpec((seq_dim,) + st_shape, lambda b: (own(b),) + (0,) * st_nd))
        args.append(init)
    n_prev = 0 if prev_state is None else prev_state.shape[1]
    if n_prev:
        in_specs.append(pl.BlockSpec((seq_dim, n_prev) + st_shape, lambda b: (own(b),) + (0,) * (st_nd + 1)))
        args.append(prev_state)
    aliases = {}
    n_alias = 0
    if y_prev is not None:
        aliases[len(args)] = 0
        in_specs.append(pl.BlockSpec(memory_space=pl.ANY))
        args.append(y_prev)
        n_alias += 1
    out_specs = [pl.BlockSpec((rows, width), lambda b: (b + row_off, 0))]
    out_shape = [jax.ShapeDtypeStruct((n_rows_total, width), BF16)]
    if emit_state:
        out_specs.append(pl.BlockSpec((seq_dim, n_prev + 1) + st_shape, lambda b: (own(b),) + (0,) * (st_nd + 1)))
        out_shape.append(jax.ShapeDtypeStruct((n_seq, n_prev + 1) + st_shape, F32))
        static = dict(static, st_layer=n_prev)
    layout = dict(n_seq_in=len(seq_cols), n_const=len(consts), has_init=init is not None, n_alias=n_alias,
                  emit_state=emit_state)
    prev_pos = len(seq_cols) + len(consts) + int(init is not None)

    def kern(*refs):
        step = pl.program_id(0)
        body_refs = refs[:prev_pos] + refs[prev_pos + 1:] if n_prev else refs

        @pl.when(step < n_steps)
        def _():
            if n_prev:
                st_out = _split_refs(body_refs, **layout)[4]
                if seq_dim is None:
                    st_out[0:n_prev] = refs[prev_pos][...]
                else:
                    st_out[:, 0:n_prev] = refs[prev_pos][...]
            body(*body_refs, **layout, **static)

        if n_fill:
            @pl.when(step >= n_steps)
            def _():
                y_ref = _split_refs(body_refs, **layout)[3]
                y_ref[...] = jnp.zeros_like(y_ref)

    return pl.pallas_call(
        kern, grid=(n_steps + n_fill,), in_specs=in_specs, out_specs=tuple(out_specs),
        out_shape=tuple(out_shape), scratch_shapes=scratch, input_output_aliases=aliases,
        compiler_params=_cparams(1), name=name)(*args)


def _split_refs(refs, n_seq_in, n_const, has_init, n_alias, emit_state):
    seq = refs[:n_seq_in]
    consts = refs[n_seq_in:n_seq_in + n_const]
    pos = n_seq_in + n_const
    init = refs[pos] if has_init else None
    pos += int(has_init) + n_alias
    y_ref = refs[pos]
    st_out = refs[pos + 1] if emit_state else None
    pos += 1 + int(emit_state)
    return seq, consts, init, y_ref, st_out, refs[pos:]


def _ssd_kernel(*refs, seg, st_layer=0, **layout):
    (z_ref, xs_ref, bc_ref, small_ref), consts, h0_ref, y_ref, hout_ref, scratch = _split_refs(refs, **layout)
    wx_ref, bx_ref, wbc_ref, bbc_ref, dtb_ref, nega_ref, dvec_ref, ng_ref = consts
    xc_ref, bcc_ref, yacc_ref, stt_ref = scratch
    length = z_ref.shape[0]
    t_len = SSD_T
    n_chunks = length // t_len
    n_pairs = SSD_H // 2
    _conv_cols(xs_ref, wx_ref, bx_ref, xc_ref, seg, True)
    _conv_cols(bc_ref, wbc_ref, bbc_ref, bcc_ref, seg, True)
    if h0_ref is None:
        stt_ref[...] = jnp.zeros_like(stt_ref)
    else:
        for d in range(2):
            for hp in range(n_pairs):
                pair = jnp.concatenate([h0_ref[d, 2 * hp], h0_ref[d, 2 * hp + 1]], axis=0)
                stt_ref[d, :, hp * LANE:(hp + 1) * LANE] = pair.T
    yacc_ref[...] = xc_ref[...] * dvec_ref[...]
    lower, upper = _tri_masks(t_len)
    tri = (lower.astype(BF16), upper.astype(BF16))
    masks = (lower, upper)
    lane_lo = lax.broadcasted_iota(jnp.int32, (t_len, LANE), 1) < SSD_P
    gn = SSD_N

    def chunk(c, d):
        r0 = pl.multiple_of(c * t_len, t_len)
        dtv = _softplus(small_ref[pl.ds(r0, t_len), :] + dtb_ref[d])
        cs = _exact_lhs_dot(tri[d], nega_ref[d] * dtv)
        cst = cs.T
        dtt = dtv.T
        end = t_len - 1 if d == 0 else 0
        bcv = bcc_ref[pl.ds(r0, t_len), :]
        groups = []
        for g in range(SSD_G):
            bg = bcv[:, g * gn:(g + 1) * gn]
            cg = bcv[:, (SSD_G + g) * gn:(SSD_G + g + 1) * gn]
            groups.append((_bdot_nt(cg, bg), bg.T, cg))
        for hp in range(n_pairs):
            gm, bgt, cg = groups[(2 * hp) // (SSD_H // SSD_G)]
            cols = slice(hp * LANE, (hp + 1) * LANE)
            x = xc_ref[pl.ds(r0, t_len), cols]
            st = stt_ref[d, :, cols]
            x_lo = jnp.where(lane_lo, x, 0.0).astype(BF16)
            x_hi = jnp.where(lane_lo, 0.0, x).astype(BF16)
            s_lo = jnp.where(lane_lo, st, 0.0).astype(BF16)
            s_hi = jnp.where(lane_lo, 0.0, st).astype(BF16)
            intra, carry_in, upd, edec = [], [], [], []
            for h in (2 * hp, 2 * hp + 1):
                colx = jnp.broadcast_to(cs[:, h:h + 1], (t_len, t_len))
                row = cst[h:h + 1, :]
                dtr = dtt[h:h + 1, :]
                cend = cst[h:h + 1, end:end + 1]
                intra.append((jnp.where(masks[d], jnp.exp(colx - row), 0.0) * (gm * dtr)).astype(BF16))
                carry_in.append((cg * jnp.exp(colx)).astype(BF16))
                upd.append((bgt * (jnp.exp(cend - row) * dtr)).astype(BF16))
                edec.append(jnp.exp(cend))
            xblk = jnp.concatenate([x_lo, x_hi], axis=0)
            y = jnp.dot(jnp.concatenate(intra + carry_in, axis=1),
                        jnp.concatenate([xblk, s_lo, s_hi], axis=0), preferred_element_type=F32)
            snew = jnp.dot(jnp.concatenate(upd, axis=1), xblk, preferred_element_type=F32)
            stt_ref[d, :, cols] = st * jnp.where(lane_lo[0:1], edec[0], edec[1]) + snew
            yacc_ref[pl.ds(r0, t_len), cols] += y

    def body(j, carry):
        chunk(j, 0)
        chunk(n_chunks - 1 - j, 1)
        return carry

    lax.fori_loop(0, n_chunks, body, 0)
    if hout_ref is not None:
        for d in range(2):
            for hp in range(n_pairs):
                pair = stt_ref[d, :, hp * LANE:(hp + 1) * LANE].T
                hout_ref[st_layer, d, 2 * hp] = pair[:SSD_P]
                hout_ref[st_layer, d, 2 * hp + 1] = pair[SSD_P:]

    def finish(c, carry):
        r0 = pl.multiple_of(c * t_len, t_len)
        y = yacc_ref[pl.ds(r0, t_len), :] * _silu(z_ref[pl.ds(r0, t_len), :])
        y_ref[pl.ds(r0, t_len), :] = (_rms(y) * ng_ref[...]).astype(y_ref.dtype)
        return carry

    lax.fori_loop(0, n_chunks, finish, 0)


def _ssd_call(proj, p, h0, **where):
    length = where['length']
    dtb = jnp.zeros((2, 1, LANE), F32).at[:, 0, :SSD_H].set(p['ssd_dt_bias'].astype(F32))
    nega = jnp.zeros((2, 1, LANE), F32).at[:, 0, :SSD_H].set(-jnp.exp(p['ssd_A_log'].astype(F32)))
    dvec = jnp.repeat(p['ssd_D'].astype(F32), SSD_P).reshape(1, SSD_D)
    cw = p['ssd_conv_w'].astype(F32)
    cb = p['ssd_conv_b'].astype(F32).reshape(1, SSD_CONV_CH)
    consts = [cw[:, :SSD_D], cb[:, :SSD_D], cw[:, SSD_D:], cb[:, SSD_D:], dtb, nega, dvec,
              p['ssd_norm_g'].astype(F32).reshape(1, SSD_D)]
    seq_cols = [(SSD_D, COL_Z // SSD_D), (SSD_D, COL_XS // SSD_D), (512, COL_BC // 512),
                (LANE, COL_SMALL // LANE)]
    scratch = [pltpu.VMEM((length, SSD_D), F32), pltpu.VMEM((length, 512), F32),
               pltpu.VMEM((length, SSD_D), F32), pltpu.VMEM((2, SSD_N, SSD_D), F32)]
    return _mixer_call(_ssd_kernel, "ssd_mixer", proj, seq_cols, consts, h0, (2, SSD_H, SSD_P, SSD_N),
                       SSD_D, scratch, **where)


def _log_sigmoid(x):
    return jnp.minimum(x, 0.0) - jnp.log1p(jnp.exp(-jnp.abs(x)))


def _gla_kernel(*refs, sps, st_layer=0, **layout):
    (q_ref, k_ref, v_ref, g_ref, small_ref), consts, s0_ref, y_ref, sout_ref, scratch = _split_refs(refs, **layout)
    gw_ref, gb_ref, ng_ref = consts
    oacc_ref, st_ref, bc_ref = scratch
    n_rows = q_ref.shape[0]
    length = n_rows // sps
    t_len = GLA_T
    n_chunks = length // t_len
    st_ref[...] = jnp.zeros_like(st_ref)
    if s0_ref is not None:
        for s in range(sps):
            for d in range(2):
                for h in range(GLA_H):
                    st_ref[s, d, h * GLA_DV:(h + 1) * GLA_DV, h * GLA_DK:(h + 1) * GLA_DK] = s0_ref[s, d, h]
    oacc_ref[...] = jnp.zeros_like(oacc_ref)
    lower, upper = _tri_masks(t_len)
    masks = tuple(jnp.concatenate([m] * GLA_H, axis=0) for m in (lower, upper))
    q_head = (lax.broadcasted_iota(jnp.int32, (GLA_H * t_len, GLA_KW), 0) // t_len
              == lax.broadcasted_iota(jnp.int32, (GLA_H * t_len, GLA_KW), 1) // GLA_DK)
    st_diag = (lax.broadcasted_iota(jnp.int32, (GLA_VW, GLA_KW), 0) // GLA_DV
               == lax.broadcasted_iota(jnp.int32, (GLA_VW, GLA_KW), 1) // GLA_DK)
    scale = GLA_DK ** -0.5

    blk = 256
    rr = lax.broadcasted_iota(jnp.int32, (blk, blk), 0)
    cc = lax.broadcasted_iota(jnp.int32, (blk, blk), 1)
    same_chunk = rr // t_len == cc // t_len
    tri_blk = ((same_chunk & (rr >= cc)).astype(BF16), (same_chunk & (rr <= cc)).astype(BF16))
    gate_w = []
    for d in range(2):
        w1, w2 = _split2(gw_ref[d])
        gate_w.append(jnp.concatenate([w1, w1, w2], axis=0))

    def decay_sums(b, carry):
        r0 = pl.multiple_of(b * blk, blk)
        s1, s2 = _split2(small_ref[pl.ds(r0, blk), :])
        lhs = jnp.concatenate([s1, s2, s1], axis=1)
        for d in range(2):
            logit = jnp.dot(lhs, gate_w[d], preferred_element_type=F32) + gb_ref[d]
            la = _log_sigmoid(logit) / GLA_TAU
            bc3 = jnp.dot(tri_blk[d], jnp.concatenate(_split3(la), axis=1), preferred_element_type=F32)
            bc_ref[d, pl.ds(r0, blk), :] = bc3[:, :GLA_KW] + bc3[:, GLA_KW:2 * GLA_KW] + bc3[:, 2 * GLA_KW:]
        return carry

    lax.fori_loop(0, n_rows // blk, decay_sums, 0)

    def chunk(c, d, s):
        r0 = pl.multiple_of(s * length + c * t_len, t_len)
        bc = bc_ref[d, pl.ds(r0, t_len), :]
        end = t_len - 1 if d == 0 else 0
        mid = bc[t_len // 2:t_len // 2 + 1, :]
        tot = bc[end:end + 1, :]
        q = q_ref[pl.ds(r0, t_len), :] * scale
        k = k_ref[pl.ds(r0, t_len), :]
        v = v_ref[pl.ds(r0, t_len), :].astype(BF16)
        qt = q * jnp.exp(bc - mid)
        kt = k * jnp.exp(mid - bc)
        qs = q * jnp.exp(bc)
        kd = k * jnp.exp(tot - bc)
        q_rows = jnp.where(q_head, jnp.concatenate([qt] * GLA_H, axis=0), 0.0)
        att = jnp.where(masks[d], _bdot_nt(q_rows, kt), 0.0)
        o_all = jnp.dot(att.astype(BF16), v, preferred_element_type=F32)
        o = jnp.concatenate([o_all[h * t_len:(h + 1) * t_len, h * GLA_DV:(h + 1) * GLA_DV]
                             for h in range(GLA_H)], axis=1)
        st = st_ref[s, d]
        o = o + _bdot_nt(qs, st)
        upd = lax.dot_general(v, kd.astype(BF16), (((0,), (0,)), ((), ())), preferred_element_type=F32)
        st_ref[s, d] = st * jnp.exp(tot) + jnp.where(st_diag, upd, 0.0)
        oacc_ref[pl.ds(r0, t_len), :] += o

    def body(j, carry):
        for s in range(sps):
            chunk(j, 0, s)
            chunk(n_chunks - 1 - j, 1, s)
        return carry

    lax.fori_loop(0, n_chunks, body, 0)
    if sout_ref is not None:
        for s in range(sps):
            for d in range(2):
                for h in range(GLA_H):
                    sout_ref[s, st_layer, d, h] = st_ref[s, d, h * GLA_DV:(h + 1) * GLA_DV,
                                                         h * GLA_DK:(h + 1) * GLA_DK]

    def finish(c, carry):
        r0 = pl.multiple_of(c * t_len, t_len)
        gate = _silu(g_ref[pl.ds(r0, t_len), :])
        for h in range(GLA_H):
            vs = slice(h * GLA_DV, (h + 1) * GLA_DV)
            o = _rms(oacc_ref[pl.ds(r0, t_len), vs]) * ng_ref[...]
            y_ref[pl.ds(r0, t_len), vs] = (o * gate[:, vs]).astype(y_ref.dtype)
        return carry

    lax.fori_loop(0, sps * n_chunks, finish, 0)


def _gla_call(proj, p, s0t, **where):
    length = where['length']
    sps = where['sps']
    gw = jnp.zeros((2, LANE, GLA_KW), F32).at[:, GLA_RANK:2 * GLA_RANK, :].set(p['gla_gate_w'].astype(F32))
    consts = [gw, p['gla_gate_b'].astype(F32).reshape(2, 1, GLA_KW),
              p['gla_norm_g'].astype(F32).reshape(1, GLA_DV)]
    seq_cols = [(GLA_KW, COL_Q // GLA_KW), (GLA_KW, COL_K // GLA_KW), (GLA_VW, COL_V // GLA_VW),
                (GLA_VW, COL_G // GLA_VW), (LANE, COL_SMALL // LANE)]
    st_shape = (2, GLA_H, GLA_DV, GLA_DK)
    scratch = [pltpu.VMEM((sps * length, GLA_VW), F32), pltpu.VMEM((sps, 2, GLA_VW, GLA_KW), F32),
               pltpu.VMEM((2, sps * length, GLA_KW), F32)]
    return _mixer_call(_gla_kernel, "gla_mixer", proj, seq_cols, consts, s0t, st_shape, GLA_VW, scratch, **where)


def _lru_kernel(*refs, seg, sps, st_layer=0, **layout):
    (xb_ref, gb_ref), consts, h0_ref, y_ref, hout_ref, scratch = _split_refs(refs, **layout)
    cw_ref, cb_ref, w_ref, bias_ref, sp_ref = consts
    xr_ref, a_ref, u_ref = scratch
    n_rows = xb_ref.shape[0]
    length = n_rows // sps
    blk = 256
    _conv_cols(xb_ref, cw_ref, cb_ref, xr_ref, seg, False)

    def gates(c, carry):
        r0 = pl.multiple_of(c * blk, blk)
        xr = xr_ref[pl.ds(r0, blk), :]
        xbf = xr.astype(BF16)
        for d in range(2):
            pre = jnp.dot(xbf, w_ref[d], preferred_element_type=F32) + bias_ref[d]
            gates = 0.5 * (jnp.tanh(0.5 * pre) + 1.0)
            r = gates[:, :LRU_W]
            i = gates[:, LRU_W:]
            log_a = (-LRU_C) * r * sp_ref[d]
            a = jnp.exp(log_a)
            a_ref[d, pl.ds(r0, blk), :] = a
            u_ref[d, pl.ds(r0, blk), :] = jnp.sqrt(-jnp.tanh(log_a) * (a * a + 1.0)) * (i * xr)
        return carry

    lax.fori_loop(0, n_rows // blk, gates, 0)

    def scan(t, carry):
        out = []
        for s in range(sps):
            for d in range(2):
                row = s * length + (t if d == 0 else length - 1 - t)
                h = a_ref[d, pl.ds(row, 1), :] * carry[2 * s + d] + u_ref[d, pl.ds(row, 1), :]
                u_ref[d, pl.ds(row, 1), :] = h
                out.append(h)
        return tuple(out)

    if h0_ref is None:
        start = tuple(jnp.zeros((1, LRU_W), F32) for _ in range(2 * sps))
    else:
        start = tuple(h0_ref[s, d] for s in range(sps) for d in range(2))
    final = lax.fori_loop(0, length, scan, start)
    if hout_ref is not None:
        for s in range(sps):
            for d in range(2):
                hout_ref[s, st_layer, d] = final[2 * s + d]

    def finish(c, carry):
        r0 = pl.multiple_of(c * blk, blk)
        y = (u_ref[0, pl.ds(r0, blk), :] + u_ref[1, pl.ds(r0, blk), :]) * jax.nn.gelu(gb_ref[pl.ds(r0, blk), :])
        y_ref[pl.ds(r0, blk), :] = y.astype(y_ref.dtype)
        return carry

    lax.fori_loop(0, n_rows // blk, finish, 0)


def _block_diag(w):
    nb, bw, _ = w.shape
    eye = jnp.eye(nb, dtype=w.dtype)
    return (eye[:, None, :, None] * w[:, :, None, :]).reshape(nb * bw, nb * bw)


def _lru_call(proj, p, h0, **where):
    length = where['length']
    w = jnp.stack([jnp.concatenate([_block_diag(p['lru_wa'][d].astype(F32)),
                                    _block_diag(p['lru_wx'][d].astype(F32))], axis=1) for d in range(2)])
    bias = jnp.stack([jnp.concatenate([p['lru_ba'][d], p['lru_bx'][d]]) for d in range(2)]).astype(F32)
    sp = jax.nn.softplus(-p['lru_lambda'].astype(F32)).reshape(2, 1, LRU_W)
    consts = [p['lru_conv_w'].astype(F32), p['lru_conv_b'].astype(F32).reshape(1, LRU_W), w.astype(BF16),
              bias.reshape(2, 1, 2 * LRU_W), sp]
    seq_cols = [(LRU_W, COL_XB // LRU_W), (LRU_W, COL_GB // LRU_W)]
    rows = where['sps'] * length
    scratch = [pltpu.VMEM((rows, LRU_W), F32), pltpu.VMEM((2, rows, LRU_W), F32),
               pltpu.VMEM((2, rows, LRU_W), F32)]
    return _mixer_call(_lru_kernel, "lru_mixer", proj, seq_cols, consts, h0, (2, 1, LRU_W), LRU_W, scratch,
                       **where)


def kernel(x_prompt, x_sample, state_ssd, state_gla, state_lru, c, c_ctx, mod_w, mod_b, norm1_g, norm2_g, in_w, ssd_conv_w, ssd_conv_b, ssd_A_log, ssd_dt_bias, ssd_D, ssd_norm_g, gla_gate_w, gla_gate_b, gla_norm_g, lru_conv_w, lru_conv_b, lru_wa, lru_ba, lru_wx, lru_bx, lru_lambda, out_w, ffn_w1, ffn_w3, ffn_w2, moe_router, moe_w1, moe_w3, moe_w2, final_norm_g):
    bp, lp, d = x_prompt.shape
    bs, ls, _ = x_sample.shape
    n_p = bp * lp
    n_s = bs * ls
    n_all = n_p + n_s
    depth = in_w.shape[0]
    assert n_p % ls == 0 and 1 + bs <= MOD_ROWS

    cvec = jnp.zeros((MOD_ROWS, d), F32).at[0].set(c_ctx.astype(F32)).at[1:1 + bs].set(c.astype(F32))
    mods = _adaln(cvec, mod_w, mod_b)
    mods3 = mods.reshape(depth * MOD_ROWS * 6, 1, d)
    x = (x_prompt.reshape(n_p, d).astype(F32), x_sample.reshape(n_s, d).astype(F32))

    in_w_t = jnp.swapaxes(in_w, 1, 2)
    st_ssd = st_gla = st_lru = None
    for i in range(depth):
        p = {'ssd_conv_w': ssd_conv_w[i], 'ssd_conv_b': ssd_conv_b[i], 'ssd_A_log': ssd_A_log[i],
             'ssd_dt_bias': ssd_dt_bias[i], 'ssd_D': ssd_D[i], 'ssd_norm_g': ssd_norm_g[i],
             'gla_gate_w': gla_gate_w[i], 'gla_gate_b': gla_gate_b[i], 'gla_norm_g': gla_norm_g[i],
             'lru_conv_w': lru_conv_w[i], 'lru_conv_b': lru_conv_b[i], 'lru_wa': lru_wa[i],
             'lru_ba': lru_ba[i], 'lru_wx': lru_wx[i], 'lru_bx': lru_bx[i], 'lru_lambda': lru_lambda[i]}
        h = _norm_mod(x, norm1_g[i], mods3, i, 0, n_p, ls)
        proj = _in_proj(h, in_w_t, i)
        ctx = dict(n_seq=bp, length=lp, row_off=0, n_rows_total=n_all, y_prev=None, emit_state=True)
        lat = dict(n_seq=bs, length=ls, row_off=n_p // ls, n_rows_total=n_all, emit_state=False)
        y_ssd, st_ssd = _ssd_call(proj, p, None, seg=lp, prev_state=st_ssd, **ctx)
        y_ssd, = _ssd_call(proj, p, state_ssd[:, i].astype(F32), seg=GRID_W, y_prev=y_ssd, **lat)
        y_gla, st_gla = _gla_call(proj, p, None, sps=MIX_SPS, prev_state=st_gla, **ctx)
        y_gla, = _gla_call(proj, p, jnp.swapaxes(state_gla[:, i].astype(F32), -1, -2), sps=MIX_SPS,
                           y_prev=y_gla, **lat)
        y_lru, st_lru = _lru_call(proj, p, None, seg=lp, sps=MIX_SPS, prev_state=st_lru, **ctx)
        y_lru, = _lru_call(proj, p, state_lru[:, i].astype(F32).reshape(bs, 2, 1, LRU_W), seg=GRID_W,
                           sps=MIX_SPS, y_prev=y_lru, **lat)
        j = i // 2
        if i % 2 == 1:
            x, h2, route = _out_proj(y_ssd, y_gla, y_lru, out_w, x, norm2_g[i], mods3, i, n_p, ls,
                                     router=moe_router[j])
            x = _moe_ffn(x, h2, route, moe_w1[j], moe_w3[j], moe_w2[j], mods3, i, n_p, ls,
                         final_g=final_norm_g if i == depth - 1 else None)
        else:
            x, h2 = _out_proj(y_ssd, y_gla, y_lru, out_w, x, norm2_g[i], mods3, i, n_p, ls)
            x = _ffn_down(_ffn_up(h2, ffn_w1[j], ffn_w3[j]), ffn_w2[j], x, mods3, i, n_p, ls)
    if isinstance(x, tuple):
        y_p, y_s = x
    else:
        y_p = _final_norm(x, final_norm_g, 0, n_p)
        y_s = _final_norm(x, final_norm_g, n_p, n_s)
    return (y_p.reshape(bp, lp, d), y_s.reshape(bs, ls, d), st_ssd, jnp.swapaxes(st_gla, -1, -2),
            st_lru.reshape(bp, depth, 2, LRU_W))
```
